```python
import math
import jax, jax.numpy as jnp
from jax import lax
import numpy as np

D_MODEL = 1024
BATCH = 2
SEQ = 8192
DEPTH = 1
DEC_BATCH = 32
DEC_SEQ = 1
PAST_LEN = 8192
PAGE_SIZE = 128

HEAD_DIM = 64
N_NSA_HEADS = 8
N_KV = 2
HPG = N_NSA_HEADS // N_KV
N_RET_HEADS = 8
RET_DK = 64
RET_DV = 64
NSA_WIDTH = N_NSA_HEADS * HEAD_DIM
RET_QK = N_RET_HEADS * RET_DK
RET_WIDTH = N_RET_HEADS * RET_DV
MIX_WIDTH = NSA_WIDTH + RET_WIDTH
KV_WIDTH = N_KV * HEAD_DIM
N_IN = NSA_WIDTH + 6 * KV_WIDTH + 3 * N_NSA_HEADS + 2 * RET_QK + 2 * RET_WIDTH
D_FF = 4 * D_MODEL
CMP_STRIDE = 16
CMP_BLOCK = 2 * CMP_STRIDE
SEL_BLOCK = 64
TOP_N = 16
WINDOW = 512
Q_BLOCK = 128
RET_CHUNK = 128
ROPE_THETA = 500000.0
ROPE_DIMS = HEAD_DIM // 4
RET_THETA = 10000.0
ALPHA = (2 * DEPTH) ** 0.25
BETA = (8 * DEPTH) ** -0.25
LN_EPS = 1e-5
FORCE_BONUS = 1e4
NEG = -1e30

kernel_name = 'nsa_retention_parallel_heads_step'


def layer_norm(x, g, b):
    xf = x.astype(jnp.float32)
    mu = xf.mean(-1, keepdims=True)
    var = jnp.square(xf - mu).mean(-1, keepdims=True)
    return ((xf - mu) * lax.rsqrt(var + LN_EPS) * g + b).astype(x.dtype)


def rope(x, pos, theta, n_rot):
    half = n_rot // 2
    inv = theta ** (-jnp.arange(half, dtype=jnp.float32) / half)
    ang = pos.astype(jnp.float32)[:, None] * inv[None, :]
    cos = jnp.cos(ang)[:, None, :]
    sin = jnp.sin(ang)[:, None, :]
    xr = x[..., :n_rot].astype(jnp.float32)
    x1, x2 = xr[..., :half], xr[..., half:]
    rot = jnp.concatenate([x1 * cos - x2 * sin, x1 * sin + x2 * cos], -1).astype(x.dtype)
    return jnp.concatenate([rot, x[..., n_rot:]], -1)


def masked_softmax(s, mask):
    s = jnp.where(mask, s, NEG)
    m = jnp.max(s, -1, keepdims=True)
    p = jnp.exp(s - m) * mask
    return p / jnp.maximum(p.sum(-1, keepdims=True), 1e-30)


def compress(rows, pos_emb, w):
    B, L, G, d = rows.shape
    ch = rows.reshape(B, L // CMP_STRIDE, CMP_STRIDE, G, d)
    blocks = jnp.concatenate([ch[:, :-1], ch[:, 1:]], axis=2)
    blocks = blocks + pos_emb[None, None, :, None, :]
    return jnp.einsum('bnrgd,rde->bnge', blocks, w)


def nsa_query_block(q, gates, qpos, kc, vc, ks_blk, vs_blk, kw, vw, wpos):
    B, Q, H, d = q.shape
    scale = d ** -0.5
    qg = q.reshape(B, Q, N_KV, HPG, d)
    t = qpos[:, None]
    nc = kc.shape[1]
    cend = jnp.arange(nc) * CMP_STRIDE + CMP_BLOCK - 1
    s_c = jnp.einsum('bqghd,bngd->bghqn', qg, kc).astype(jnp.float32) * scale
    p_c = masked_softmax(s_c, cend[None, :] <= t)
    o_c = jnp.einsum('bghqn,bngd->bqghd', p_c.astype(vc.dtype), vc)
    ns = ks_blk.shape[2]
    imp = jnp.pad(p_c.sum(2), ((0, 0), (0, 0), (0, 0), (1, 1)))
    chunk = imp[..., 1:] + imp[..., :-1]
    imp_s = chunk.reshape(B, N_KV, Q, ns, SEL_BLOCK // CMP_STRIDE).sum(-1)
    blk = jnp.arange(ns)[None, :]
    cur = (qpos // SEL_BLOCK)[:, None]
    forced = (blk == 0) | (blk == cur) | (blk == cur - 1)
    score = jnp.where(blk <= cur, imp_s + FORCE_BONUS * forced, NEG)
    k_top = min(TOP_N, ns)
    _, idx = lax.top_k(score, k_top)
    flat = idx.reshape(B, N_KV, Q * k_top)
    b_i = jnp.arange(B)[:, None, None]
    g_i = jnp.arange(N_KV)[None, :, None]
    ksel = ks_blk[b_i, g_i, flat].reshape(B, N_KV, Q, k_top * SEL_BLOCK, d)
    vsel = vs_blk[b_i, g_i, flat].reshape(B, N_KV, Q, k_top * SEL_BLOCK, d)
    kpos = (idx[..., None] * SEL_BLOCK + jnp.arange(SEL_BLOCK)).reshape(B, N_KV, Q, k_top * SEL_BLOCK)
    s_s = jnp.einsum('bqghd,bgqnd->bghqn', qg, ksel).astype(jnp.float32) * scale
    p_s = masked_softmax(s_s, (kpos <= t)[:, :, None])
    o_s = jnp.einsum('bghqn,bgqnd->bqghd', p_s.astype(vsel.dtype), vsel)
    dist = t - wpos[None, :]
    wmask = (dist >= 0) & (dist < WINDOW) & (wpos[None, :] >= 0)
    s_w = jnp.einsum('bqghd,bwgd->bghqw', qg, kw).astype(jnp.float32) * scale
    p_w = masked_softmax(s_w, wmask)
    o_w = jnp.einsum('bghqw,bwgd->bqghd', p_w.astype(vw.dtype), vw)
    g = gates.reshape(B, Q, N_KV, HPG, 3)
    o = g[..., 0:1] * o_c + g[..., 1:2] * o_s + g[..., 2:3] * o_w
    return o.reshape(B, Q, H * d)


def project(h, pos, w_in):
    B, T, _ = h.shape
    z = h @ w_in
    p0 = NSA_WIDTH
    p1 = p0 + 6 * KV_WIDTH
    p2 = p1 + 3 * N_NSA_HEADS
    p3 = p2 + RET_QK
    p4 = p3 + RET_QK
    p5 = p4 + RET_WIDTH
    q, kv, gt, rq, rk, rv, rg = jnp.split(z, [p0, p1, p2, p3, p4, p5], axis=-1)
    q = rope(q.reshape(B, T, N_NSA_HEADS, HEAD_DIM), pos, ROPE_THETA, ROPE_DIMS)
    kv = kv.reshape(B, T, 6, N_KV, HEAD_DIM)
    k_slc = rope(kv[:, :, 2], pos, ROPE_THETA, ROPE_DIMS)
    k_win = rope(kv[:, :, 4], pos, ROPE_THETA, ROPE_DIMS)
    kv4 = jnp.stack([kv[:, :, 0], kv[:, :, 1], k_slc, kv[:, :, 3]], axis=2)
    win = jnp.stack([k_win, kv[:, :, 5]], axis=2)
    gates = jax.nn.sigmoid(gt.astype(jnp.float32)).astype(h.dtype).reshape(B, T, N_NSA_HEADS, 3)
    rq = rope(rq.reshape(B, T, N_RET_HEADS, RET_DK), pos, RET_THETA, RET_DK)
    rk = rope(rk.reshape(B, T, N_RET_HEADS, RET_DK), pos, RET_THETA, RET_DK) * (RET_DK ** -0.5)
    rv = rv.reshape(B, T, N_RET_HEADS, RET_DV)
    return q, gates, kv4, win, rq, rk, rv, rg


def nsa_prompt(q, gates, kv4, win, w_cmp_k, w_cmp_v, pos_k, pos_v):
    B, T, H, d = q.shape
    kc = compress(kv4[:, :, 0], pos_k, w_cmp_k)
    vc = compress(kv4[:, :, 1], pos_v, w_cmp_v)
    ns = T // SEL_BLOCK
    ks_blk = kv4[:, :, 2].reshape(B, ns, SEL_BLOCK, N_KV, d).transpose(0, 3, 1, 2, 4)
    vs_blk = kv4[:, :, 3].reshape(B, ns, SEL_BLOCK, N_KV, d).transpose(0, 3, 1, 2, 4)
    kw_pad = jnp.pad(win[:, :, 0], ((0, 0), (WINDOW, 0), (0, 0), (0, 0)))
    vw_pad = jnp.pad(win[:, :, 1], ((0, 0), (WINDOW, 0), (0, 0), (0, 0)))
    nb = T // Q_BLOCK
    qb = q.reshape(B, nb, Q_BLOCK, H, d).swapaxes(0, 1)
    gb = gates.reshape(B, nb, Q_BLOCK, H, 3).swapaxes(0, 1)

    def one(args):
        qi, gi, bi = args
        start = bi * Q_BLOCK
        qpos = start + jnp.arange(Q_BLOCK)
        kw = lax.dynamic_slice_in_dim(kw_pad, start, WINDOW + Q_BLOCK, axis=1)
        vw = lax.dynamic_slice_in_dim(vw_pad, start, WINDOW + Q_BLOCK, axis=1)
        wpos = start - WINDOW + jnp.arange(WINDOW + Q_BLOCK)
        return nsa_query_block(qi, gi, qpos, kc, vc, ks_blk, vs_blk, kw, vw, wpos)

    o = lax.map(one, (qb, gb, jnp.arange(nb)))
    return o.swapaxes(0, 1).reshape(B, T, NSA_WIDTH)


def nsa_sample(q, gates, kv4_new, win_new, cache_kv_l, page_table, cache_win_l, w_cmp_k, w_cmp_v, pos_k, pos_v):
    Bd, S, H, d = q.shape
    past = page_table.shape[1] * PAGE_SIZE
    rows = cache_kv_l[page_table].reshape(Bd, past, 4, N_KV, d)
    full = jnp.concatenate([rows, kv4_new.astype(rows.dtype)], axis=1)
    L = past + S
    Lp = -(-L // SEL_BLOCK) * SEL_BLOCK
    full = jnp.pad(full, ((0, 0), (0, Lp - L), (0, 0), (0, 0), (0, 0)))
    kc = compress(full[:, :, 0], pos_k, w_cmp_k)
    vc = compress(full[:, :, 1], pos_v, w_cmp_v)
    ns = Lp // SEL_BLOCK
    ks_blk = full[:, :, 2].reshape(Bd, ns, SEL_BLOCK, N_KV, d).transpose(0, 3, 1, 2, 4)
    vs_blk = full[:, :, 3].reshape(Bd, ns, SEL_BLOCK, N_KV, d).transpose(0, 3, 1, 2, 4)
    wb = cache_win_l.shape[1]
    wall = jnp.concatenate([cache_win_l, win_new.astype(cache_win_l.dtype)], axis=1)
    wpos = past - wb + jnp.arange(wb + S)
    qpos = past + jnp.arange(S)
    o = nsa_query_block(q, gates, qpos, kc, vc, ks_blk, vs_blk, wall[:, :, 0], wall[:, :, 1], wpos)
    return o, wall[:, S:]


def retention_chunk(q, k, v, state, log_gamma):
    C = q.shape[1]
    qf, kf, vf = q.astype(jnp.float32), k.astype(jnp.float32), v.astype(jnp.float32)
    i = jnp.arange(C, dtype=jnp.float32)
    diff = i[:, None] - i[None, :]
    decay = jnp.where(diff >= 0, jnp.exp(log_gamma[:, None, None] * jnp.maximum(diff, 0.0)), 0.0)
    inner = jnp.einsum('bihd,bjhd->bhij', qf, kf) * decay
    o = jnp.einsum('bhij,bjhe->bihe', inner, vf)
    q_dec = jnp.exp((i + 1.0)[:, None] * log_gamma[None, :])[None, :, :, None]
    o = o + jnp.einsum('bihd,bhde->bihe', qf * q_dec, state)
    k_dec = jnp.exp((C - 1.0 - i)[:, None] * log_gamma[None, :])[None, :, :, None]
    new_state = jnp.exp(C * log_gamma)[None, :, None, None] * state + jnp.einsum('bjhd,bjhe->bhde', kf * k_dec, vf)
    return o, new_state


def retention_prompt(q, k, v, log_gamma):
    B, T, H, dk = q.shape
    n = T // RET_CHUNK

    def split(a):
        return a.reshape(B, n, RET_CHUNK, H, a.shape[-1]).swapaxes(0, 1)

    def step(st, xs):
        qc, kc, vc = xs
        o, st = retention_chunk(qc, kc, vc, st, log_gamma)
        return st, o

    s0 = jnp.zeros((B, H, dk, RET_DV), jnp.float32)
    st, o = lax.scan(step, s0, (split(q), split(k), split(v)))
    return o.swapaxes(0, 1).reshape(B, T, H, RET_DV), st


def layer_tail(x, o_nsa, o_ret, rg, ret_norm_g, w_o, ln1_g, ln1_b, w_up, w_down, ln2_g, ln2_b):
    B, T = x.shape[:2]
    mu = o_ret.mean(-1, keepdims=True)
    var = jnp.square(o_ret - mu).mean(-1, keepdims=True)
    gn = ((o_ret - mu) * lax.rsqrt(var + LN_EPS)).reshape(B, T, RET_WIDTH) * ret_norm_g
    ret = jax.nn.silu(rg.astype(jnp.float32)) * gn
    mix = jnp.concatenate([o_nsa.astype(x.dtype), ret.astype(x.dtype)], axis=-1) @ w_o
    x1 = layer_norm(ALPHA * x + mix, ln1_g, ln1_b)
    f = jnp.square(jax.nn.relu(x1 @ w_up)) @ w_down
    return layer_norm(ALPHA * x1 + f, ln2_g, ln2_b)


def setup_inputs(seed: int = 0) -> dict:
    key = jax.random.key(seed)
    ks = jax.random.split(key, 24)
    n_pages = PAST_LEN // PAGE_SIZE
    n_used = DEC_BATCH * n_pages
    n_phys = n_used + (n_used + 3) // 4
    perm = jax.random.permutation(ks[0], n_phys)
    page_table = perm[:n_used].reshape(DEC_BATCH, n_pages).astype(jnp.int32)
    wb = min(WINDOW, PAST_LEN)
    nrm = jax.random.normal
    f32 = jnp.float32
    return {
        'x_prompt': nrm(ks[1], (BATCH, SEQ, D_MODEL), f32),
        'x_sample': nrm(ks[2], (DEC_BATCH, DEC_SEQ, D_MODEL), f32),
        'cache_kv': nrm(ks[3], (DEPTH, n_phys, PAGE_SIZE, 4, N_KV, HEAD_DIM), f32),
        'cache_win': nrm(ks[4], (DEPTH, DEC_BATCH, wb, 2, N_KV, HEAD_DIM), f32),
        'state_ret': 0.5 * nrm(ks[5], (DEPTH, DEC_BATCH, N_RET_HEADS, RET_DK, RET_DV), f32),
        'page_table': page_table,
        'w_in': nrm(ks[6], (DEPTH, D_MODEL, N_IN), f32) * D_MODEL ** -0.5,
        'w_cmp_k': nrm(ks[7], (DEPTH, CMP_BLOCK, HEAD_DIM, HEAD_DIM), f32) * (CMP_BLOCK * HEAD_DIM) ** -0.5,
        'w_cmp_v': nrm(ks[8], (DEPTH, CMP_BLOCK, HEAD_DIM, HEAD_DIM), f32) * (CMP_BLOCK * HEAD_DIM) ** -0.5,
        'pos_cmp_k': 0.5 * nrm(ks[9], (DEPTH, CMP_BLOCK, HEAD_DIM), f32),
        'pos_cmp_v': 0.5 * nrm(ks[10], (DEPTH, CMP_BLOCK, HEAD_DIM), f32),
        'ret_norm_g': 1.0 + 0.1 * nrm(ks[11], (DEPTH, RET_WIDTH), f32),
        'w_o': nrm(ks[12], (DEPTH, MIX_WIDTH, D_MODEL), f32) * MIX_WIDTH ** -0.5 * BETA,
        'ln1_g': 1.0 + 0.1 * nrm(ks[13], (DEPTH, D_MODEL), f32),
        'ln1_b': 0.02 * nrm(ks[14], (DEPTH, D_MODEL), f32),
        'w_up': nrm(ks[15], (DEPTH, D_MODEL, D_FF), f32) * D_MODEL ** -0.5,
        'w_down': nrm(ks[16], (DEPTH, D_FF, D_MODEL), f32) * D_FF ** -0.5 * BETA,
        'ln2_g': 1.0 + 0.1 * nrm(ks[17], (DEPTH, D_MODEL), f32),
        'ln2_b': 0.02 * nrm(ks[18], (DEPTH, D_MODEL), f32),
    }


def reference(x_prompt, x_sample, cache_kv, cache_win, state_ret, page_table, w_in, w_cmp_k, w_cmp_v, pos_cmp_k, pos_cmp_v, ret_norm_g, w_o, ln1_g, ln1_b, w_up, w_down, ln2_g, ln2_b):
    T = x_prompt.shape[1]
    S = x_sample.shape[1]
    past = page_table.shape[1] * PAGE_SIZE
    pos_p = jnp.arange(T, dtype=jnp.int32)
    pos_s = past + jnp.arange(S, dtype=jnp.int32)
    log_gamma = jnp.log1p(-jnp.exp2(-5.0 - jnp.arange(N_RET_HEADS, dtype=jnp.float32)))
    hp, hs = x_prompt, x_sample
    kv_p, kv_s, win_p, win_s, ret_p, ret_s = [], [], [], [], [], []
    for l in range(DEPTH):
        q, g, kv4, win, rq, rk, rv, rg = project(hp, pos_p, w_in[l])
        o_nsa = nsa_prompt(q, g, kv4, win, w_cmp_k[l], w_cmp_v[l], pos_cmp_k[l], pos_cmp_v[l])
        o_ret, st = retention_prompt(rq, rk, rv, log_gamma)
        kv_p.append(kv4)
        win_p.append(win[:, T - min(WINDOW, T):])
        ret_p.append(st)
        hp = layer_tail(hp, o_nsa, o_ret, rg, ret_norm_g[l], w_o[l], ln1_g[l], ln1_b[l], w_up[l], w_down[l], ln2_g[l], ln2_b[l])
        q, g, kv4, win, rq, rk, rv, rg = project(hs, pos_s, w_in[l])
        o_nsa, new_win = nsa_sample(q, g, kv4, win, cache_kv[l], page_table, cache_win[l], w_cmp_k[l], w_cmp_v[l], pos_cmp_k[l], pos_cmp_v[l])
        o_ret, st = retention_chunk(rq, rk, rv, state_ret[l].astype(jnp.float32), log_gamma)
        kv_s.append(kv4)
        win_s.append(new_win)
        ret_s.append(st)
        hs = layer_tail(hs, o_nsa, o_ret.reshape(hs.shape[0], S, N_RET_HEADS, RET_DV), rg, ret_norm_g[l], w_o[l], ln1_g[l], ln1_b[l], w_up[l], w_down[l], ln2_g[l], ln2_b[l])
    y_prompt = hp
    y_sample = hs
    kv_prompt = jnp.stack(kv_p)
    kv_sample = jnp.stack(kv_s)
    win_prompt = jnp.stack(win_p)
    win_sample = jnp.stack(win_s)
    ret_prompt = jnp.stack(ret_p)
    ret_sample = jnp.stack(ret_s)
    return (y_prompt, y_sample, kv_prompt, kv_sample, win_prompt, win_sample, ret_prompt, ret_sample)
```

```python
import functools

import jax
import jax.numpy as jnp
from jax import lax
from jax.experimental import pallas as pl
from jax.experimental.pallas import tpu as pltpu

F32 = jnp.float32
BF16 = jnp.bfloat16

D_MODEL = 1024
HEAD_DIM = 64
N_NSA_HEADS = 8
N_KV = 2
HPG = N_NSA_HEADS // N_KV
N_RET_HEADS = 8
RET_DK = 64
RET_DV = 64
NSA_WIDTH = N_NSA_HEADS * HEAD_DIM
RET_WIDTH = N_RET_HEADS * RET_DV
KV_WIDTH = N_KV * HEAD_DIM
D_FF = 4 * D_MODEL
CMP_STRIDE = 16
CMP_BLOCK = 32
SEL_BLOCK = 64
TOP_N = 16
WINDOW = 512
RET_CHUNK = 128
PAGE_SIZE = 128
ROPE_THETA = 500000.0
ROPE_DIMS = HEAD_DIM // 4
RET_THETA = 10000.0
LN_EPS = 1e-5
FORCE_BONUS = 1e4
NEG = -1e30
BELOW_NEG = -3e38

LANES = 128
SUBLANES = 8
VMEM_LIMIT = 56 * 1024 * 1024

Q_TILE = 128
KEY_TILE = 512
N_PAIR = N_RET_HEADS // 2
N_SEL_BLOCKS = 128

U_Q, U_KV, U_RQ, U_RK, U_RV, U_RG, U_G, U_TOTAL = 0, 4, 10, 14, 18, 22, 26, 27
HEAD_PERM = (0, 4, 1, 5, 2, 6, 3, 7)


def _dot(a, b):
    return jnp.dot(a, b, preferred_element_type=F32)


def _dot_nt(a, b):
    return lax.dot_general(a, b, (((1,), (1,)), ((), ())), preferred_element_type=F32)


def _cparams(*sem):
    return pltpu.CompilerParams(dimension_semantics=sem, vmem_limit_bytes=VMEM_LIMIT)


def _const_spec(shape):
    nd = len(shape)
    return pl.BlockSpec(shape, lambda *_: (0,) * nd)


def _rope(u, c, s_fwd, s_bwd, half):
    return u * c + pltpu.roll(u, LANES - half, 1) * s_fwd + pltpu.roll(u, half, 1) * s_bwd


def _proj_kernel(x_ref, w_ref, cq_ref, sqf_ref, sqb_ref, cr_ref, srf_ref, srb_ref,
                 q_o, kv4_o, win_o, kaug_o, vs_o, kw_o, vw_o, g_o, rq_o, rk_o, rv_o, rg_o, *, tm, seq):
    xb = x_ref[...].astype(BF16)

    def seg(u0, n):
        return _dot(xb, w_ref[:, u0 * LANES:(u0 + n) * LANES])

    def unit(z, u):
        return z[:, u * LANES:(u + 1) * LANES]

    cq, sqf, sqb = cq_ref[...], sqf_ref[...], sqb_ref[...]
    cr, srf, srb = cr_ref[...], srf_ref[...], srb_ref[...]
    nsa_half, ret_half = ROPE_DIMS // 2, RET_DK // 2

    zq = seg(U_Q, 4)
    for c in range(4):
        q_o[:, c * LANES:(c + 1) * LANES] = (_rope(unit(zq, c), cq, sqf, sqb, nsa_half) * (HEAD_DIM ** -0.5)).astype(BF16)

    zkv = seg(U_KV, 6)
    k_slc = _rope(unit(zkv, 2), cq, sqf, sqb, nsa_half)
    k_win = _rope(unit(zkv, 4), cq, sqf, sqb, nsa_half)
    kv4_o[:, 0:2 * LANES] = zkv[:, 0:2 * LANES]
    kv4_o[:, 2 * LANES:3 * LANES] = k_slc
    kv4_o[:, 3 * LANES:4 * LANES] = unit(zkv, 3)
    win_o[:, 0:LANES] = k_win
    win_o[:, LANES:2 * LANES] = unit(zkv, 5)
    row = pl.program_id(0) * tm + lax.broadcasted_iota(jnp.int32, (tm, 1), 0)
    blk = (row & (seq - 1)) >> 6
    onehot = lax.broadcasted_iota(jnp.int32, (tm, LANES), 1) == blk
    kaug_o[:, 0:LANES] = k_slc.astype(BF16)
    kaug_o[:, LANES:2 * LANES] = jnp.where(onehot, 1.0, 0.0).astype(BF16)
    vs_o[...] = unit(zkv, 3).astype(BF16)
    kw_o[...] = k_win.astype(BF16)
    vw_o[...] = unit(zkv, 5).astype(BF16)

    g_o[...] = jax.nn.sigmoid(seg(U_G, 1))

    zrq = seg(U_RQ, 4)
    zrk = seg(U_RK, 4)
    for c in range(4):
        sl = slice(c * LANES, (c + 1) * LANES)
        rq_o[:, sl] = _rope(unit(zrq, c), cr, srf, srb, ret_half)
        rk_o[:, sl] = _rope(unit(zrk, c), cr, srf, srb, ret_half) * (RET_DK ** -0.5)
    rv_o[...] = seg(U_RV, 4)
    rg_o[...] = seg(U_RG, 4)


def _rope_tables(pos, theta, n_rot):
    half = n_rot // 2
    inv = theta ** (-jnp.arange(half, dtype=F32) / half)
    ang = pos.astype(F32)[:, None] * inv[None, :]
    cos, sin = jnp.cos(ang), jnp.sin(ang)
    n = pos.shape[0]
    rest = HEAD_DIM - n_rot
    c = jnp.concatenate([cos, cos, jnp.ones((n, rest), F32)], 1)
    s_fwd = jnp.concatenate([-sin, jnp.zeros((n, half + rest), F32)], 1)
    s_bwd = jnp.concatenate([jnp.zeros((n, half), F32), sin, jnp.zeros((n, rest), F32)], 1)
    rep = LANES // HEAD_DIM
    return tuple(jnp.tile(t, (1, rep)) for t in (c, s_fwd, s_bwd))


def _project(x2d, w_perm, tables, *, tm, seq):
    rows = x2d.shape[0]
    n_tab_tiles = tables[0].shape[0] // tm
    grid = (rows // tm,)
    row_spec = lambda w: pl.BlockSpec((tm, w), lambda i: (i, 0))
    tab_spec = pl.BlockSpec((tm, LANES), lambda i: (i % n_tab_tiles, 0))
    out_defs = [(4 * LANES, BF16), (4 * LANES, F32), (2 * LANES, F32), (2 * LANES, BF16), (LANES, BF16), (LANES, BF16),
                (LANES, BF16), (LANES, F32), (4 * LANES, F32), (4 * LANES, F32), (4 * LANES, F32), (4 * LANES, F32)]
    return pl.pallas_call(
        functools.partial(_proj_kernel, tm=tm, seq=seq),
        grid=grid,
        in_specs=[row_spec(D_MODEL), _const_spec(w_perm.shape)] + [tab_spec] * 6,
        out_specs=[row_spec(w) for w, _ in out_defs],
        out_shape=[jax.ShapeDtypeStruct((rows, w), dt) for w, dt in out_defs],
        compiler_params=_cparams("parallel"),
        name="proj",
    )(x2d, w_perm, *tables)


def _cmp_kernel(pt_ref, cache_ref, wlo_ref, whi_ref, plo_ref, phi_ref, out_ref, slab, sem, *, n_pages):
    b = pl.program_id(0)
    nb = pl.num_programs(0)
    slot = b % 2

    def page_copy(bb, sl, p, kv):
        return pltpu.make_async_copy(
            cache_ref.at[pt_ref[bb, p], :, pl.ds(kv * LANES, LANES)],
            slab.at[sl, kv, pl.ds(p * PAGE_SIZE, PAGE_SIZE), :],
            sem.at[sl])

    def start_all(bb, sl):
        for p in range(n_pages):
            for kv in range(2):
                page_copy(bb, sl, p, kv).start()

    @pl.when(b == 0)
    def _():
        start_all(0, 0)

    @pl.when(b + 1 < nb)
    def _():
        start_all(b + 1, 1 - slot)

    for p in range(n_pages):
        for kv in range(2):
            page_copy(b, slot, p, kv).wait()

    n_chunks = n_pages * PAGE_SIZE // CMP_STRIDE
    acc_lo = jnp.zeros((n_chunks, 2 * LANES), F32)
    acc_hi = jnp.zeros((n_chunks, 2 * LANES), F32)
    for r in range(CMP_STRIDE):
        xr = jnp.concatenate([slab[slot, kv, pl.ds(r, n_chunks, stride=CMP_STRIDE), :] for kv in range(2)], axis=1)
        acc_lo = acc_lo + _dot((xr + plo_ref[r:r + 1, :]).astype(BF16), wlo_ref[r])
        acc_hi = acc_hi + _dot((xr + phi_ref[r:r + 1, :]).astype(BF16), whi_ref[r])
    nxt = pltpu.roll(acc_hi, n_chunks - 1, 0)
    last = lax.broadcasted_iota(jnp.int32, (n_chunks, 1), 0) == n_chunks - 1
    out_ref[...] = jnp.where(last, 0.0, acc_lo + nxt)


def _compress(page_table, cache4, wlo, whi, plo, phi):
    nb, n_pages = page_table.shape
    n_chunks = n_pages * PAGE_SIZE // CMP_STRIDE
    grid_spec = pltpu.PrefetchScalarGridSpec(
        num_scalar_prefetch=1,
        grid=(nb,),
        in_specs=[pl.BlockSpec(memory_space=pl.ANY),
                  pl.BlockSpec(wlo.shape, lambda b, pt: (0, 0, 0)),
                  pl.BlockSpec(whi.shape, lambda b, pt: (0, 0, 0)),
                  pl.BlockSpec(plo.shape, lambda b, pt: (0, 0)),
                  pl.BlockSpec(phi.shape, lambda b, pt: (0, 0))],
        out_specs=pl.BlockSpec((None, n_chunks, 2 * LANES), lambda b, pt: (b, 0, 0)),
        scratch_shapes=[pltpu.VMEM((2, 2, n_pages * PAGE_SIZE, LANES), F32), pltpu.SemaphoreType.DMA((2,))],
    )
    return pl.pallas_call(
        functools.partial(_cmp_kernel, n_pages=n_pages),
        grid_spec=grid_spec,
        out_shape=jax.ShapeDtypeStruct((nb, n_chunks, 2 * LANES), F32),
        compiler_params=_cparams("arbitrary"),
        name="compress",
    )(page_table, cache4, wlo, whi, plo, phi)


def _compress_weights(w_k, w_v, pos_k, pos_v):
    def bd(wk_r, wv_r):
        z = jnp.zeros((HEAD_DIM, HEAD_DIM), F32)
        rows = [[wk_r, z, z, z], [z, wk_r, z, z], [z, z, wv_r, z], [z, z, z, wv_r]]
        return jnp.block(rows)

    w_all = jax.vmap(bd)(w_k, w_v).astype(BF16)
    p_all = jnp.concatenate([pos_k, pos_k, pos_v, pos_v], axis=1)
    return w_all[:CMP_STRIDE], w_all[CMP_STRIDE:], p_all[:CMP_STRIDE], p_all[CMP_STRIDE:]


def _masked_softmax(s, mask):
    s = jnp.where(mask, s, NEG)
    m = jnp.max(s, -1, keepdims=True)
    p = jnp.exp(s - m) * mask.astype(F32)
    return p / jnp.maximum(jnp.sum(p, -1, keepdims=True), 1e-30)


def _split3(x):
    hi = x.astype(BF16)
    r1 = x - hi.astype(F32)
    mid = r1.astype(BF16)
    lo = (r1 - mid.astype(F32)).astype(BF16)
    return hi, mid, lo


def _importance_matrix(n_cmp):
    n = jnp.arange(n_cmp)[:, None]
    j = jnp.arange(N_SEL_BLOCKS)[None, :]
    inner = (n >= 4 * j) & (n <= 4 * j + 2)
    edge = (n == 4 * j - 1) | (n == 4 * j + 3)
    return jnp.where(inner, 2.0, jnp.where(edge, 1.0, 0.0)).astype(F32)


def _nsa_kernel(q_ref, g_ref, kcvc_ref, kaug_ref, vs_ref, kw_ref, vw_ref, mt_ref, out_ref, m_s, l_s, acc_s, *, seq):
    i = pl.program_id(1)
    t0 = i * Q_TILE
    n_rows = N_NSA_HEADS * Q_TILE
    lane = lax.broadcasted_iota(jnp.int32, (1, LANES), 1)
    low = lane < HEAD_DIM

    q = q_ref[...]
    zero = jnp.zeros((), BF16)
    pieces = []
    for g in range(N_KV):
        for c in range(HPG):
            qc = q[:, c * LANES:(c + 1) * LANES]
            pieces.append(jnp.where(low if g == 0 else ~low, qc, zero))
    qs = jnp.concatenate(pieces, axis=0)
    tq1 = t0 + lax.broadcasted_iota(jnp.int32, (Q_TILE, 1), 0)
    tq = jnp.concatenate([tq1] * N_NSA_HEADS, axis=0)

    kcvc = kcvc_ref[...]
    kc = kcvc[:, 0:LANES].astype(BF16)
    vc = kcvc[:, LANES:2 * LANES].astype(BF16)
    s_c = _dot_nt(qs, kc)
    cend = lax.broadcasted_iota(jnp.int32, (1, kcvc.shape[0]), 1) * CMP_STRIDE + (CMP_BLOCK - 1)
    p_c = _masked_softmax(s_c, cend <= tq)
    o_c = _dot(p_c.astype(BF16), vc)

    blk = lax.broadcasted_iota(jnp.int32, (N_SEL_BLOCKS, 1), 0)
    blk_f = blk.astype(F32)
    cur = (t0 + lane) >> 6
    forced = (blk == 0) | (blk == cur) | (blk == cur - 1)
    sel_rows = []
    for g in range(N_KV):
        base = g * HPG * Q_TILE
        psum = p_c[base:base + Q_TILE]
        for c in range(1, HPG):
            psum = psum + p_c[base + c * Q_TILE:base + (c + 1) * Q_TILE]
        mt = mt_ref[...]
        imp_t = sum(_dot_nt(mt, part) for part in _split3(psum))
        score = jnp.where(blk <= cur, imp_t + FORCE_BONUS * forced.astype(F32), NEG)
        sel = jnp.zeros(score.shape, jnp.bool_)
        for _ in range(TOP_N):
            best = jnp.max(score, axis=0, keepdims=True)
            first = jnp.min(jnp.where(score == best, blk_f, float(N_SEL_BLOCKS)), axis=0, keepdims=True)
            pick = blk_f == first
            sel = sel | pick
            score = jnp.where(pick, BELOW_NEG, score)
        bias_t = jnp.where(sel, 0.0, NEG)
        bias = jnp.transpose(bias_t).astype(BF16)
        sel_rows.extend([bias] * HPG)
    q_aug = jnp.concatenate([qs, jnp.concatenate(sel_rows, axis=0)], axis=1)

    m_s[...] = jnp.full(m_s.shape, NEG, F32)
    l_s[...] = jnp.zeros(l_s.shape, F32)
    acc_s[...] = jnp.zeros(acc_s.shape, F32)

    def sweep(kt, causal):
        k0 = pl.multiple_of(kt * KEY_TILE, KEY_TILE)
        s = _dot_nt(q_aug, kaug_ref[pl.ds(k0, KEY_TILE), :])
        if causal:
            kpos = k0 + lax.broadcasted_iota(jnp.int32, (1, KEY_TILE), 1)
            s = jnp.where(kpos <= tq, s, NEG)
        m_old = m_s[...]
        m_new = jnp.maximum(m_old, jnp.max(s, -1, keepdims=True))
        alpha = jnp.exp(m_old - m_new)
        p = jnp.exp(s - m_new)
        l_s[...] = alpha * l_s[...] + jnp.sum(p, -1, keepdims=True)
        acc_s[...] = alpha * acc_s[...] + _dot(p.astype(BF16), vs_ref[pl.ds(k0, KEY_TILE), :])
        m_s[...] = m_new

    last = (t0 + Q_TILE - 1) // KEY_TILE

    def body(kt, carry):
        sweep(kt, False)
        return carry

    lax.fori_loop(0, last, body, 0)
    sweep(last, True)
    o_s = acc_s[...] / l_s[...]

    w0 = pl.multiple_of(jnp.maximum(t0 - WINDOW, 0), Q_TILE)
    n_win = WINDOW + Q_TILE
    s_w = _dot_nt(qs, kw_ref[pl.ds(w0, n_win), :])
    dist = tq - (w0 + lax.broadcasted_iota(jnp.int32, (1, n_win), 1))
    p_w = _masked_softmax(s_w, (dist >= 0) & (dist < WINDOW))
    o_w = _dot(p_w.astype(BF16), vw_ref[pl.ds(w0, n_win), :])

    gates = g_ref[...]
    for c in range(HPG):
        halves = []
        for g in range(N_KV):
            r0 = (g * HPG + c) * Q_TILE
            h = g * HPG + c
            gc, gs, gw = (gates[:, 3 * h + k:3 * h + k + 1] for k in range(3))
            rows = slice(r0, r0 + Q_TILE)
            halves.append(gc * o_c[rows] + gs * o_s[rows] + gw * o_w[rows])
        out_ref[:, c * LANES:(c + 1) * LANES] = jnp.where(low, halves[0], halves[1]).astype(BF16)


def _nsa_prompt(q_s, gates, kcvc, kaug, vs, kw, vw, mt, *, batch, seq):
    nq = seq // Q_TILE
    n_rows = N_NSA_HEADS * Q_TILE
    row_spec = lambda w: pl.BlockSpec((Q_TILE, w), lambda b, i: (b * nq + i, 0))
    seq_spec = lambda w: pl.BlockSpec((seq, w), lambda b, i: (b, 0))
    return pl.pallas_call(
        functools.partial(_nsa_kernel, seq=seq),
        grid=(batch, nq),
        in_specs=[row_spec(4 * LANES), row_spec(LANES),
                  pl.BlockSpec((None,) + kcvc.shape[1:], lambda b, i: (b, 0, 0)),
                  seq_spec(2 * LANES), seq_spec(LANES), seq_spec(LANES), seq_spec(LANES),
                  _const_spec(mt.shape)],
        out_specs=row_spec(4 * LANES),
        out_shape=jax.ShapeDtypeStruct((batch * seq, 4 * LANES), BF16),
        scratch_shapes=[pltpu.VMEM((n_rows, 1), F32), pltpu.VMEM((n_rows, 1), F32), pltpu.VMEM((n_rows, LANES), F32)],
        compiler_params=_cparams("parallel", "arbitrary"),
        name="nsa_prompt",
    )(q_s, gates, kcvc, kaug, vs, kw, vw, mt)


def _head_sum(a, low):
    s0 = jnp.sum(jnp.where(low, a, 0.0), -1, keepdims=True)
    s1 = jnp.sum(jnp.where(low, 0.0, a), -1, keepdims=True)
    return jnp.where(low, s0, s1)


def _gated_group_norm(o, rg, norm_g, low):
    mu = _head_sum(o, low) * (1.0 / RET_DV)
    d = o - mu
    var = _head_sum(d * d, low) * (1.0 / RET_DV)
    gn = d * lax.rsqrt(var + LN_EPS) * norm_g
    return (rg * jax.nn.sigmoid(rg)) * gn


def _ret_kernel(rq_ref, rk_ref, rv_ref, rg_ref, decay_ref, qdec_ref, kdec_ref, gcl_ref, bd_ref, ng_ref,
                ret_o, st_o, st_s):
    c = pl.program_id(1)

    @pl.when(c == 0)
    def _():
        st_s[...] = jnp.zeros(st_s.shape, F32)

    low = lax.broadcasted_iota(jnp.int32, (1, LANES), 1) < RET_DK
    zero = jnp.zeros((), BF16)
    for p in range(N_PAIR):
        sl = slice(p * LANES, (p + 1) * LANES)
        q, k, v = rq_ref[:, sl], rk_ref[:, sl], rv_ref[:, sl]
        qb, kb, vb = q.astype(BF16), k.astype(BF16), v.astype(BF16)
        in0 = _dot_nt(jnp.where(low, qb, zero), kb) * decay_ref[2 * p]
        in1 = _dot_nt(jnp.where(low, zero, qb), kb) * decay_ref[2 * p + 1]
        inner = jnp.concatenate([in0.astype(BF16), in1.astype(BF16)], axis=1)
        v_split = jnp.concatenate([jnp.where(low, vb, zero), jnp.where(low, zero, vb)], axis=0)
        state = st_s[p]
        o = _dot(inner, v_split) + _dot((q * qdec_ref[:, sl]).astype(BF16), state.astype(BF16))
        kd_t = jnp.transpose(k * kdec_ref[:, sl]).astype(BF16)
        st_s[p] = state * gcl_ref[:, sl] + _dot(kd_t, vb) * bd_ref[...]
        ret_o[:, sl] = _gated_group_norm(o, rg_ref[:, sl], ng_ref[:, sl], low).astype(BF16)

    @pl.when(c == pl.num_programs(1) - 1)
    def _():
        st_o[...] = st_s[...]


def _retention_tables(log_gamma, chunk):
    i = jnp.arange(chunk, dtype=F32)
    diff = i[:, None] - i[None, :]
    decay = jnp.where(diff >= 0, jnp.exp(log_gamma[:, None, None] * jnp.maximum(diff, 0.0)), 0.0)
    per_lane = lambda t: jnp.repeat(t, RET_DK, axis=-1)
    qdec = per_lane(jnp.exp((i + 1.0)[:, None] * log_gamma[None, :]))
    kdec = per_lane(jnp.exp((chunk - 1.0 - i)[:, None] * log_gamma[None, :]))
    gcl = per_lane(jnp.exp(chunk * log_gamma)[None, :])
    r = jnp.arange(LANES)
    bd = ((r[:, None] // RET_DK) == (r[None, :] // RET_DV)).astype(F32)
    return decay, qdec, kdec, gcl, bd


def _retention_prompt(rq, rk, rv, rg, tables, norm_g, *, batch, seq):
    decay, qdec, kdec, gcl, bd = tables
    nc = seq // RET_CHUNK
    row_spec = pl.BlockSpec((RET_CHUNK, RET_WIDTH), lambda b, c: (b * nc + c, 0))
    return pl.pallas_call(
        _ret_kernel,
        grid=(batch, nc),
        in_specs=[row_spec] * 4 + [_const_spec(decay.shape), _const_spec(qdec.shape), _const_spec(kdec.shape),
                                   _const_spec(gcl.shape), _const_spec(bd.shape), _const_spec(norm_g.shape)],
        out_specs=[row_spec, pl.BlockSpec((None, N_PAIR, LANES, LANES), lambda b, c: (b, 0, 0, 0))],
        out_shape=[jax.ShapeDtypeStruct((batch * seq, RET_WIDTH), BF16),
                   jax.ShapeDtypeStruct((batch, N_PAIR, LANES, LANES), F32)],
        scratch_shapes=[pltpu.VMEM((N_PAIR, LANES, LANES), F32)],
        compiler_params=_cparams("parallel", "arbitrary"),
        name="retention",
    )(rq, rk, rv, rg, decay, qdec, kdec, gcl, bd, norm_g)


def _layer_norm(v, g, b):
    mu = jnp.mean(v, -1, keepdims=True)
    d = v - mu
    var = jnp.mean(d * d, -1, keepdims=True)
    return d * lax.rsqrt(var + LN_EPS) * g + b


def _tail_kernel(x_ref, nsa_ref, ret_ref, wo_ref, g1_ref, b1_ref, wu_ref, wd_ref, g2_ref, b2_ref, y_ref, *, alpha):
    mix = _dot(nsa_ref[...], wo_ref[0:NSA_WIDTH, :]) + _dot(ret_ref[...], wo_ref[NSA_WIDTH:, :])
    x1 = _layer_norm(alpha * x_ref[...] + mix, g1_ref[...], b1_ref[...])
    x1b = x1.astype(BF16)
    f = jnp.zeros(x1.shape, F32)
    for j in range(D_FF // D_MODEL):
        sl = slice(j * D_MODEL, (j + 1) * D_MODEL)
        h = jnp.maximum(_dot(x1b, wu_ref[:, sl]), 0.0)
        f = f + _dot((h * h).astype(BF16), wd_ref[sl, :])
    y_ref[...] = _layer_norm(alpha * x1 + f, g2_ref[...], b2_ref[...])


def _tail(x2d, nsa, ret, wo, g1, b1, wu, wd, g2, b2, *, tm, alpha):
    rows = x2d.shape[0]
    row_spec = lambda w: pl.BlockSpec((tm, w), lambda i: (i, 0))
    consts = (wo, g1, b1, wu, wd, g2, b2)
    return pl.pallas_call(
        functools.partial(_tail_kernel, alpha=alpha),
        grid=(rows // tm,),
        in_specs=[row_spec(D_MODEL), row_spec(NSA_WIDTH), row_spec(RET_WIDTH)] + [_const_spec(c.shape) for c in consts],
        out_specs=row_spec(D_MODEL),
        out_shape=jax.ShapeDtypeStruct((rows, D_MODEL), F32),
        compiler_params=_cparams("parallel"),
        name="tail",
    )(x2d, nsa, ret, *consts)


DEC_ROWS = 16
N_DEC_SEL = TOP_N - 1


def _dec_stack_q(q, low):
    rows = lax.broadcasted_iota(jnp.int32, (DEC_ROWS, 1), 0)
    qs = jnp.zeros((DEC_ROWS, LANES), F32)
    for g in range(N_KV):
        for c in range(HPG):
            piece = jnp.where(low if g == 0 else ~low, q[:, c * LANES:(c + 1) * LANES], 0.0)
            qs = jnp.where(rows == g * HPG + c, piece, qs)
    return qs, rows


def _dec_score_kernel(q_ref, kcvc_ref, m_ref, oc_o, idx_o, *, past):
    lane = lax.broadcasted_iota(jnp.int32, (1, LANES), 1)
    low = lane < HEAD_DIM
    qs, rows = _dec_stack_q(q_ref[...], low)
    qsb = qs.astype(BF16)
    kcvc = kcvc_ref[...]
    kc = kcvc[:, 0:LANES].astype(BF16)
    vc = kcvc[:, LANES:2 * LANES].astype(BF16)
    n_cmp = kcvc.shape[0]
    cend = lax.broadcasted_iota(jnp.int32, (1, n_cmp), 1) * CMP_STRIDE + (CMP_BLOCK - 1)
    p = _masked_softmax(_dot_nt(qsb, kc), jnp.broadcast_to(cend <= past, (DEC_ROWS, n_cmp)))
    oc_o[...] = _dot(p.astype(BF16), vc)

    psum = jnp.zeros((DEC_ROWS, n_cmp), F32)
    for g in range(N_KV):
        acc = p[g * HPG:g * HPG + 1]
        for c in range(1, HPG):
            acc = acc + p[g * HPG + c:g * HPG + c + 1]
        psum = jnp.where(rows == g, acc, psum)
    m = m_ref[...]
    imp = sum(_dot(part, m) for part in _split3(psum))
    lane_f = lane.astype(F32)
    forced = (lane == 0) | (lane == N_SEL_BLOCKS - 1)
    score = imp + FORCE_BONUS * forced.astype(F32)
    picked = jnp.zeros((DEC_ROWS, LANES), F32)
    for r in range(N_DEC_SEL):
        best = jnp.max(score, -1, keepdims=True)
        first = jnp.min(jnp.where(score == best, lane_f, float(LANES)), -1, keepdims=True)
        picked = jnp.where(lane == r, first, picked)
        score = jnp.where(lane_f == first, BELOW_NEG, score)
    idx_o[...] = picked.astype(jnp.int32)


def _dec_score(q3, kcvc, m, *, past):
    nb = q3.shape[0]
    n_cmp = kcvc.shape[1]
    return pl.pallas_call(
        functools.partial(_dec_score_kernel, past=past),
        grid=(nb,),
        in_specs=[pl.BlockSpec((None, 1, 4 * LANES), lambda b: (b, 0, 0)),
                  pl.BlockSpec((None, n_cmp, 2 * LANES), lambda b: (b, 0, 0)),
                  _const_spec(m.shape)],
        out_specs=[pl.BlockSpec((None, DEC_ROWS, LANES), lambda b: (b, 0, 0))] * 2,
        out_shape=[jax.ShapeDtypeStruct((nb, DEC_ROWS, LANES), F32), jax.ShapeDtypeStruct((nb, DEC_ROWS, LANES), jnp.int32)],
        compiler_params=_cparams("parallel"),
        name="dec_score",
    )(q3, kcvc, m)


def _round_bf16(x):
    return x.astype(BF16).astype(F32)


def _dec_attn_kernel(pt_ref, idx_ref, q_ref, g_ref, oc_ref, kvn_ref, wn_ref, cache_ref, cw_ref,
                     rq_ref, rk_ref, rv_ref, rg_ref, st_ref, gl_ref, ng_ref,
                     nsa_o, ret_o, win_o, st_o, kbuf, sem):
    b = pl.program_id(0)
    nb = pl.num_programs(0)
    slot = b % 2

    def blk_copy(bb, sl, g, k):
        j = idx_ref[bb, g, k]
        off = pl.multiple_of((j & 1) * SEL_BLOCK, SEL_BLOCK)
        return pltpu.make_async_copy(
            cache_ref.at[pt_ref[bb, j >> 1], pl.ds(off, SEL_BLOCK), pl.ds(2 * LANES, 2 * LANES)],
            kbuf.at[sl, g, pl.ds(k * SEL_BLOCK, SEL_BLOCK), :],
            sem.at[sl])

    def start_all(bb, sl):
        for g in range(N_KV):
            for k in range(N_DEC_SEL):
                blk_copy(bb, sl, g, k).start()

    @pl.when(b == 0)
    def _():
        start_all(0, 0)

    @pl.when(b + 1 < nb)
    def _():
        start_all(b + 1, 1 - slot)

    lane = lax.broadcasted_iota(jnp.int32, (1, LANES), 1)
    low = lane < HEAD_DIM
    qs, rows = _dec_stack_q(q_ref[...], low)
    qsb = qs.astype(BF16)
    g0_rows = rows < HPG

    cw = cw_ref[...]
    wb = cw.shape[0]
    wn = wn_ref[...]
    col = lax.broadcasted_iota(jnp.int32, (1, wb), 1)
    valid = jnp.broadcast_to((wb - col) < WINDOW, (DEC_ROWS, wb))
    s_w = jnp.where(valid, _dot_nt(qsb, cw[:, 0:LANES].astype(BF16)), NEG)
    s_wn = jnp.sum(qs * _round_bf16(wn[:, 0:LANES]), -1, keepdims=True)
    m_w = jnp.maximum(jnp.max(s_w, -1, keepdims=True), s_wn)
    p_w = jnp.exp(s_w - m_w) * valid.astype(F32)
    p_wn = jnp.exp(s_wn - m_w)
    l_w = jnp.maximum(jnp.sum(p_w, -1, keepdims=True) + p_wn, 1e-30)
    o_w = _dot((p_w / l_w).astype(BF16), cw[:, LANES:2 * LANES].astype(BF16)) + _round_bf16(p_wn / l_w) * _round_bf16(wn[:, LANES:2 * LANES])
    shifted = pltpu.roll(cw, wb - 1, 0)
    win_o[...] = jnp.where(lax.broadcasted_iota(jnp.int32, (wb, 1), 0) == wb - 1, wn, shifted)

    for g in range(N_KV):
        for k in range(N_DEC_SEL):
            blk_copy(b, slot, g, k).wait()
    kvn = kvn_ref[...]
    s_g = [_dot_nt(qsb, kbuf[slot, g, :, 0:LANES].astype(BF16)) for g in range(N_KV)]
    s_s = jnp.where(g0_rows, s_g[0], s_g[1])
    s_sn = jnp.sum(qs * _round_bf16(kvn[:, 2 * LANES:3 * LANES]), -1, keepdims=True)
    m_sel = jnp.maximum(jnp.max(s_s, -1, keepdims=True), s_sn)
    p_s = jnp.exp(s_s - m_sel)
    p_sn = jnp.exp(s_sn - m_sel)
    l_sel = jnp.maximum(jnp.sum(p_s, -1, keepdims=True) + p_sn, 1e-30)
    pn = (p_s / l_sel).astype(BF16)
    o_g = [_dot(pn, kbuf[slot, g, :, LANES:2 * LANES].astype(BF16)) for g in range(N_KV)]
    o_s = jnp.where(g0_rows, o_g[0], o_g[1]) + _round_bf16(p_sn / l_sel) * _round_bf16(kvn[:, 3 * LANES:4 * LANES])

    gates = g_ref[...]
    gcol = [jnp.zeros((DEC_ROWS, 1), F32) for _ in range(3)]
    for h in range(N_NSA_HEADS):
        for k in range(3):
            gcol[k] = jnp.where(rows == h, gates[:, 3 * h + k:3 * h + k + 1], gcol[k])
    o = gcol[0] * oc_ref[...] + gcol[1] * o_s + gcol[2] * o_w
    for c in range(HPG):
        nsa_o[:, c * LANES:(c + 1) * LANES] = jnp.where(low, o[c:c + 1], o[HPG + c:HPG + c + 1])

    for p in range(N_PAIR):
        sl = slice(p * LANES, (p + 1) * LANES)
        q, k, v, gl = rq_ref[:, sl], rk_ref[:, sl], rv_ref[:, sl], gl_ref[:, sl]
        state = st_ref[p]
        inner = _head_sum(q * k, low)

        def per_dk(row):
            t = jnp.transpose(jnp.broadcast_to(row, (LANES, LANES)))
            return jnp.where(low, t[0:RET_DK], t[RET_DK:2 * RET_DK])

        o_r = inner * v + jnp.sum(per_dk(q * gl) * state, axis=0, keepdims=True)
        st_o[p] = gl * state + per_dk(k) * v
        ret_o[:, sl] = _gated_group_norm(o_r, rg_ref[:, sl], ng_ref[:, sl], low)


def _dec_attn(page_table, idx, q3, gates3, oc, kvn3, wn3, cache4, cwin, rq3, rk3, rv3, rg3, state, gl, ng):
    nb = q3.shape[0]
    wb = cwin.shape[1]
    row3 = lambda w: pl.BlockSpec((None, 1, w), lambda b, *_: (b, 0, 0))
    st_spec = pl.BlockSpec((None, N_PAIR, RET_DK, LANES), lambda b, *_: (b, 0, 0, 0))
    win_spec = pl.BlockSpec((None, wb, 2 * LANES), lambda b, *_: (b, 0, 0))
    grid_spec = pltpu.PrefetchScalarGridSpec(
        num_scalar_prefetch=2,
        grid=(nb,),
        in_specs=[row3(4 * LANES), row3(LANES), pl.BlockSpec((None, DEC_ROWS, LANES), lambda b, *_: (b, 0, 0)),
                  row3(4 * LANES), row3(2 * LANES), pl.BlockSpec(memory_space=pl.ANY), win_spec,
                  row3(4 * LANES), row3(4 * LANES), row3(4 * LANES), row3(4 * LANES), st_spec,
                  pl.BlockSpec(gl.shape, lambda b, *_: (0, 0)), pl.BlockSpec(ng.shape, lambda b, *_: (0, 0))],
        out_specs=[row3(4 * LANES), row3(4 * LANES), win_spec, st_spec],
        scratch_shapes=[pltpu.VMEM((2, N_KV, N_DEC_SEL * SEL_BLOCK, 2 * LANES), F32), pltpu.SemaphoreType.DMA((2,))],
    )
    return pl.pallas_call(
        _dec_attn_kernel,
        grid_spec=grid_spec,
        out_shape=[jax.ShapeDtypeStruct((nb, 1, 4 * LANES), F32), jax.ShapeDtypeStruct((nb, 1, 4 * LANES), F32),
                   jax.ShapeDtypeStruct((nb, wb, 2 * LANES), F32), jax.ShapeDtypeStruct(state.shape, F32)],
        compiler_params=_cparams("arbitrary"),
        name="dec_attn",
    )(page_table, idx, q3, gates3, oc, kvn3, wn3, cache4, cwin, rq3, rk3, rv3, rg3, state, gl, ng)


def _permute_w_in(w):
    p0 = NSA_WIDTH
    p1 = p0 + 6 * KV_WIDTH
    p2 = p1 + 3 * N_NSA_HEADS
    wq = w[:, :p0].reshape(D_MODEL, N_NSA_HEADS, HEAD_DIM)[:, jnp.array(HEAD_PERM)].reshape(D_MODEL, p0)
    w_gates = jnp.pad(w[:, p1:p2], ((0, 0), (0, LANES - 3 * N_NSA_HEADS)))
    return jnp.concatenate([wq, w[:, p0:p1], w[:, p2:], w_gates], axis=1).astype(BF16)


def _permute_w_o(w):
    w_nsa = w[:NSA_WIDTH].reshape(N_NSA_HEADS, HEAD_DIM, D_MODEL)[jnp.array(HEAD_PERM)].reshape(NSA_WIDTH, D_MODEL)
    return jnp.concatenate([w_nsa, w[NSA_WIDTH:]], axis=0).astype(BF16)


def kernel(x_prompt, x_sample, cache_kv, cache_win, state_ret, page_table, w_in, w_cmp_k, w_cmp_v, pos_cmp_k, pos_cmp_v, ret_norm_g, w_o, ln1_g, ln1_b, w_up, w_down, ln2_g, ln2_b):
    depth = w_in.shape[0]
    assert depth == 1 and x_sample.shape[1] == 1
    batch, seq, _ = x_prompt.shape
    nb = x_sample.shape[0]
    n_pages = page_table.shape[1]
    past = n_pages * PAGE_SIZE
    assert past // SEL_BLOCK == N_SEL_BLOCKS and cache_win.shape[2] == WINDOW
    alpha = (2 * depth) ** 0.25
    log_gamma = jnp.log1p(-jnp.exp2(-5.0 - jnp.arange(N_RET_HEADS, dtype=F32)))

    w_perm = _permute_w_in(w_in[0])
    wo = _permute_w_o(w_o[0])
    wu, wd = w_up[0].astype(BF16), w_down[0].astype(BF16)
    vec = lambda a: a[0].reshape(1, -1)
    cmp_w = _compress_weights(w_cmp_k[0], w_cmp_v[0], pos_cmp_k[0], pos_cmp_v[0])
    tail_consts = (wo, vec(ln1_g), vec(ln1_b), wu, wd, vec(ln2_g), vec(ln2_b))
    norm_g = vec(ret_norm_g)

    tm = 512
    pos_p = jnp.arange(seq, dtype=jnp.int32)
    tabs_p = _rope_tables(pos_p, ROPE_THETA, ROPE_DIMS) + _rope_tables(pos_p, RET_THETA, RET_DK)
    x2d = x_prompt.reshape(batch * seq, D_MODEL)
    q_s, kv4, win, kaug, vs, kw, vw, gates, rq, rk, rv, rg = _project(x2d, w_perm, tabs_p, tm=tm, seq=seq)
    seq_pages = seq // PAGE_SIZE
    pt_p = jnp.arange(batch * seq_pages, dtype=jnp.int32).reshape(batch, seq_pages)
    kcvc_p = _compress(pt_p, kv4.reshape(batch * seq_pages, PAGE_SIZE, 4 * KV_WIDTH), *cmp_w)
    o_nsa = _nsa_prompt(q_s, gates, kcvc_p, kaug, vs, kw, vw, _importance_matrix(seq // CMP_STRIDE).T.astype(BF16), batch=batch, seq=seq)
    ret, st_p = _retention_prompt(rq, rk, rv, rg, _retention_tables(log_gamma, RET_CHUNK), norm_g, batch=batch, seq=seq)
    y_p = _tail(x2d, o_nsa, ret, *tail_consts, tm=tm, alpha=alpha)

    y_prompt = y_p.reshape(batch, seq, D_MODEL)
    kv_prompt = kv4.reshape(1, batch, seq, 4, N_KV, HEAD_DIM)
    wlen = min(WINDOW, seq)
    win_prompt = win.reshape(batch, seq, 2, N_KV, HEAD_DIM)[None, :, seq - wlen:]
    st6 = st_p.reshape(batch, N_PAIR, 2, RET_DK, 2, RET_DV)
    ret_prompt = jnp.stack([st6[:, :, 0, :, 0, :], st6[:, :, 1, :, 1, :]], axis=2).reshape(1, batch, N_RET_HEADS, RET_DK, RET_DV)

    pos_s = jnp.full((nb,), past, dtype=jnp.int32)
    tabs_s = _rope_tables(pos_s, ROPE_THETA, ROPE_DIMS) + _rope_tables(pos_s, RET_THETA, RET_DK)
    xs2d = x_sample.reshape(nb, D_MODEL)
    q_d, kv4_d, win_d, _, _, _, _, gates_d, rq_d, rk_d, rv_d, rg_d = _project(xs2d, w_perm, tabs_s, tm=nb, seq=seq)
    cache4 = cache_kv[0].reshape(cache_kv.shape[1], PAGE_SIZE, 4 * KV_WIDTH)
    kcvc_d = _compress(page_table, cache4, *cmp_w)
    row3 = lambda a: a.reshape(nb, 1, a.shape[-1])
    q3 = row3(q_d.astype(F32))
    oc_d, idx_d = _dec_score(q3, kcvc_d, _importance_matrix(past // CMP_STRIDE).astype(BF16), past=past)
    idx = idx_d[:, :N_KV, :TOP_N]
    state = state_ret[0].astype(F32).reshape(nb, N_PAIR, 2, RET_DK, RET_DV).transpose(0, 1, 3, 2, 4).reshape(nb, N_PAIR, RET_DK, LANES)
    gl = jnp.repeat(jnp.exp(log_gamma), RET_DK)[None, :]
    cwin = cache_win[0].reshape(nb, WINDOW, 2 * KV_WIDTH)
    nsa_d, ret_d, win_s, st_d = _dec_attn(page_table, idx, q3, row3(gates_d), oc_d, row3(kv4_d), row3(win_d), cache4, cwin,
                                          row3(rq_d), row3(rk_d), row3(rv_d), row3(rg_d), state, gl, norm_g)
    y_s = _tail(xs2d, nsa_d.reshape(nb, NSA_WIDTH).astype(BF16), ret_d.reshape(nb, RET_WIDTH).astype(BF16), *tail_consts, tm=nb, alpha=alpha)

    y_sample = y_s.reshape(nb, 1, D_MODEL)
    kv_sample = kv4_d.reshape(1, nb, 1, 4, N_KV, HEAD_DIM)
    win_sample = win_s.reshape(1, nb, WINDOW, 2, N_KV, HEAD_DIM)
    ret_sample = st_d.reshape(nb, N_PAIR, RET_DK, 2, RET_DV).transpose(0, 1, 3, 2, 4).reshape(1, nb, N_RET_HEADS, RET_DK, RET_DV)
    return (y_prompt, y_sample, kv_prompt, kv_sample, win_prompt, win_sample, ret_prompt, ret_sample)
```

```python
import functools

import jax
import jax.numpy as jnp
from jax import lax
from jax.experimental import pallas as pl
from jax.experimental.pallas import tpu as pltpu

F32 = jnp.float32
BF16 = jnp.bfloat16

D_MODEL = 1024
HEAD_DIM = 64
N_NSA_HEADS = 8
N_KV = 2
HPG = N_NSA_HEADS // N_KV
N_RET_HEADS = 8
RET_DK = 64
RET_DV = 64
NSA_WIDTH = N_NSA_HEADS * HEAD_DIM
RET_WIDTH = N_RET_HEADS * RET_DV
KV_WIDTH = N_KV * HEAD_DIM
D_FF = 4 * D_MODEL
CMP_STRIDE = 16
CMP_BLOCK = 32
SEL_BLOCK = 64
TOP_N = 16
WINDOW = 512
RET_CHUNK = 128
PAGE_SIZE = 128
ROPE_THETA = 500000.0
ROPE_DIMS = HEAD_DIM // 4
RET_THETA = 10000.0
LN_EPS = 1e-5
FORCE_BONUS = 1e4
NEG = -1e30
BELOW_NEG = -3e38

LANES = 128
SUBLANES = 8
VMEM_LIMIT = 56 * 1024 * 1024

Q_TILE = 128
KEY_TILE = 512
N_PAIR = N_RET_HEADS // 2
N_SEL_BLOCKS = 128

U_Q, U_KV, U_RQ, U_RK, U_RV, U_RG, U_G, U_TOTAL = 0, 4, 10, 14, 18, 22, 26, 27
HEAD_PERM = (0, 4, 1, 5, 2, 6, 3, 7)


def _dot(a, b):
    return jnp.dot(a, b, preferred_element_type=F32)


def _dot_nt(a, b):
    return lax.dot_general(a, b, (((1,), (1,)), ((), ())), preferred_element_type=F32)


def _cparams(*sem):
    return pltpu.CompilerParams(dimension_semantics=sem, vmem_limit_bytes=VMEM_LIMIT)


def _const_spec(shape):
    nd = len(shape)
    return pl.BlockSpec(shape, lambda *_: (0,) * nd)


def _rope(u, c, s_fwd, s_bwd, half):
    return u * c + pltpu.roll(u, LANES - half, 1) * s_fwd + pltpu.roll(u, half, 1) * s_bwd


def _proj_kernel(x_ref, w_ref, cq_ref, sqf_ref, sqb_ref, cr_ref, srf_ref, srb_ref, *outs, tm, seq, prompt):
    if prompt:
        kv4_o, win_o, rq_o, rk_o, rv_o, rg_o, qt_o, gt_o, kaug_o, vst_o, kw_o, vwt_o = outs
    else:
        kv4_o, win_o, rq_o, rk_o, rv_o, rg_o, q_o, g_o = outs
    xb = x_ref[...].astype(BF16)

    def seg(u0, n):
        return _dot(xb, w_ref[:, u0 * LANES:(u0 + n) * LANES])

    def unit(z, u):
        return z[:, u * LANES:(u + 1) * LANES]

    cq, sqf, sqb = cq_ref[...], sqf_ref[...], sqb_ref[...]
    cr, srf, srb = cr_ref[...], srf_ref[...], srb_ref[...]
    nsa_half, ret_half = ROPE_DIMS // 2, RET_DK // 2

    zq = seg(U_Q, 4)
    q = jnp.concatenate([_rope(unit(zq, c), cq, sqf, sqb, nsa_half) for c in range(4)], axis=1) * (HEAD_DIM ** -0.5)
    gates = jax.nn.sigmoid(seg(U_G, 1))
    zkv = seg(U_KV, 6)
    k_slc = _rope(unit(zkv, 2), cq, sqf, sqb, nsa_half)
    k_win = _rope(unit(zkv, 4), cq, sqf, sqb, nsa_half)
    kv4_o[:, 0:2 * LANES] = zkv[:, 0:2 * LANES]
    kv4_o[:, 2 * LANES:3 * LANES] = k_slc
    kv4_o[:, 3 * LANES:4 * LANES] = unit(zkv, 3)
    win_o[:, 0:LANES] = k_win
    win_o[:, LANES:2 * LANES] = unit(zkv, 5)
    if prompt:
        qt_o[...] = jnp.transpose(q).astype(BF16)
        gt_o[...] = jnp.transpose(gates)
        row = pl.program_id(0) * tm + lax.broadcasted_iota(jnp.int32, (tm, 1), 0)
        blk = (row & (seq - 1)) >> 6
        onehot = lax.broadcasted_iota(jnp.int32, (tm, LANES), 1) == blk
        kaug_o[:, 0:LANES] = k_slc.astype(BF16)
        kaug_o[:, LANES:2 * LANES] = jnp.where(onehot, 1.0, 0.0).astype(BF16)
        vst_o[...] = jnp.transpose(unit(zkv, 3)).astype(BF16)
        kw_o[...] = k_win.astype(BF16)
        vwt = jnp.transpose(unit(zkv, 5)).astype(BF16)
        for j in range(tm // LANES):
            vwt_o[j] = vwt[:, j * LANES:(j + 1) * LANES]
    else:
        q_o[...] = q.astype(BF16).astype(F32)
        g_o[...] = gates

    zrq = seg(U_RQ, 4)
    zrk = seg(U_RK, 4)
    for c in range(4):
        sl = slice(c * LANES, (c + 1) * LANES)
        rq_o[:, sl] = _rope(unit(zrq, c), cr, srf, srb, ret_half)
        rk_o[:, sl] = _rope(unit(zrk, c), cr, srf, srb, ret_half) * (RET_DK ** -0.5)
    rv_o[...] = seg(U_RV, 4)
    rg_o[...] = seg(U_RG, 4)


def _rope_tables(pos, theta, n_rot):
    half = n_rot // 2
    inv = theta ** (-jnp.arange(half, dtype=F32) / half)
    ang = pos.astype(F32)[:, None] * inv[None, :]
    cos, sin = jnp.cos(ang), jnp.sin(ang)
    n = pos.shape[0]
    rest = HEAD_DIM - n_rot
    c = jnp.concatenate([cos, cos, jnp.ones((n, rest), F32)], 1)
    s_fwd = jnp.concatenate([-sin, jnp.zeros((n, half + rest), F32)], 1)
    s_bwd = jnp.concatenate([jnp.zeros((n, half), F32), sin, jnp.zeros((n, rest), F32)], 1)
    rep = LANES // HEAD_DIM
    return tuple(jnp.tile(t, (1, rep)) for t in (c, s_fwd, s_bwd))


def _project(x2d, w_perm, tables, *, tm, seq, prompt):
    rows = x2d.shape[0]
    n_tab_tiles = tables[0].shape[0] // tm
    row_spec = lambda w: pl.BlockSpec((tm, w), lambda i: (i, 0))
    tab_spec = pl.BlockSpec((tm, LANES), lambda i: (i % n_tab_tiles, 0))
    outs = [(row_spec(4 * LANES), (rows, 4 * LANES), F32), (row_spec(2 * LANES), (rows, 2 * LANES), F32)]
    outs += [(row_spec(4 * LANES), (rows, 4 * LANES), F32)] * 4
    if prompt:
        assert tm == KEY_TILE and seq % tm == 0
        batch, tpb = rows // seq, seq // tm
        feat_spec = lambda f: pl.BlockSpec((None, f, tm), lambda i: (i // tpb, 0, i % tpb))
        outs += [(feat_spec(4 * LANES), (batch, 4 * LANES, seq), BF16),
                 (feat_spec(LANES), (batch, LANES, seq), F32),
                 (row_spec(2 * LANES), (rows, 2 * LANES), BF16),
                 (pl.BlockSpec((None, LANES, tm), lambda i: (i, 0, 0)), (rows // tm, LANES, tm), BF16),
                 (row_spec(LANES), (rows, LANES), BF16),
                 (pl.BlockSpec((tm // LANES, LANES, LANES), lambda i: (i, 0, 0)), (rows // LANES, LANES, LANES), BF16)]
    else:
        outs += [(row_spec(4 * LANES), (rows, 4 * LANES), F32), (row_spec(LANES), (rows, LANES), F32)]
    return pl.pallas_call(
        functools.partial(_proj_kernel, tm=tm, seq=seq, prompt=prompt),
        grid=(rows // tm,),
        in_specs=[row_spec(D_MODEL), _const_spec(w_perm.shape)] + [tab_spec] * 6,
        out_specs=[s for s, _, _ in outs],
        out_shape=[jax.ShapeDtypeStruct(shape, dt) for _, shape, dt in outs],
        compiler_params=_cparams("parallel"),
        name="proj_prompt" if prompt else "proj_decode",
    )(x2d, w_perm, *tables)


def _cmp_kernel(pt_ref, cache_ref, wlo_ref, whi_ref, plo_ref, phi_ref, out_ref, slab, sem, *, n_pages):
    b = pl.program_id(0)
    nb = pl.num_programs(0)
    slot = b % 2

    def page_copy(bb, sl, p, kv):
        return pltpu.make_async_copy(
            cache_ref.at[pt_ref[bb, p], :, pl.ds(kv * LANES, LANES)],
            slab.at[sl, kv, pl.ds(p * PAGE_SIZE, PAGE_SIZE), :],
            sem.at[sl])

    def start_all(bb, sl):
        for p in range(n_pages):
            for kv in range(2):
                page_copy(bb, sl, p, kv).start()

    @pl.when(b == 0)
    def _():
        start_all(0, 0)

    @pl.when(b + 1 < nb)
    def _():
        start_all(b + 1, 1 - slot)

    for p in range(n_pages):
        for kv in range(2):
            page_copy(b, slot, p, kv).wait()

    n_chunks = n_pages * PAGE_SIZE // CMP_STRIDE
    acc_lo = jnp.zeros((n_chunks, 2 * LANES), F32)
    acc_hi = jnp.zeros((n_chunks, 2 * LANES), F32)
    for r in range(CMP_STRIDE):
        xr = jnp.concatenate([slab[slot, kv, pl.ds(r, n_chunks, stride=CMP_STRIDE), :] for kv in range(2)], axis=1)
        acc_lo = acc_lo + _dot((xr + plo_ref[r:r + 1, :]).astype(BF16), wlo_ref[r])
        acc_hi = acc_hi + _dot((xr + phi_ref[r:r + 1, :]).astype(BF16), whi_ref[r])
    nxt = pltpu.roll(acc_hi, n_chunks - 1, 0)
    last = lax.broadcasted_iota(jnp.int32, (n_chunks, 1), 0) == n_chunks - 1
    out_ref[...] = jnp.where(last, 0.0, acc_lo + nxt)


def _compress(page_table, cache4, wlo, whi, plo, phi):
    nb, n_pages = page_table.shape
    n_chunks = n_pages * PAGE_SIZE // CMP_STRIDE
    grid_spec = pltpu.PrefetchScalarGridSpec(
        num_scalar_prefetch=1,
        grid=(nb,),
        in_specs=[pl.BlockSpec(memory_space=pl.ANY),
                  pl.BlockSpec(wlo.shape, lambda b, pt: (0, 0, 0)),
                  pl.BlockSpec(whi.shape, lambda b, pt: (0, 0, 0)),
                  pl.BlockSpec(plo.shape, lambda b, pt: (0, 0)),
                  pl.BlockSpec(phi.shape, lambda b, pt: (0, 0))],
        out_specs=pl.BlockSpec((None, n_chunks, 2 * LANES), lambda b, pt: (b, 0, 0)),
        scratch_shapes=[pltpu.VMEM((2, 2, n_pages * PAGE_SIZE, LANES), F32), pltpu.SemaphoreType.DMA((2,))],
    )
    return pl.pallas_call(
        functools.partial(_cmp_kernel, n_pages=n_pages),
        grid_spec=grid_spec,
        out_shape=jax.ShapeDtypeStruct((nb, n_chunks, 2 * LANES), F32),
        compiler_params=_cparams("arbitrary"),
        name="compress",
    )(page_table, cache4, wlo, whi, plo, phi)


def _compress_weights(w_k, w_v, pos_k, pos_v):
    def bd(wk_r, wv_r):
        z = jnp.zeros((HEAD_DIM, HEAD_DIM), F32)
        rows = [[wk_r, z, z, z], [z, wk_r, z, z], [z, z, wv_r, z], [z, z, z, wv_r]]
        return jnp.block(rows)

    w_all = jax.vmap(bd)(w_k, w_v).astype(BF16)
    p_all = jnp.concatenate([pos_k, pos_k, pos_v, pos_v], axis=1)
    return w_all[:CMP_STRIDE], w_all[CMP_STRIDE:], p_all[:CMP_STRIDE], p_all[CMP_STRIDE:]


def _masked_softmax(s, mask):
    s = jnp.where(mask, s, NEG)
    m = jnp.max(s, -1, keepdims=True)
    p = jnp.exp(s - m) * mask.astype(F32)
    return p / jnp.maximum(jnp.sum(p, -1, keepdims=True), 1e-30)


def _split3(x):
    hi = x.astype(BF16)
    r1 = x - hi.astype(F32)
    mid = r1.astype(BF16)
    lo = (r1 - mid.astype(F32)).astype(BF16)
    return hi, mid, lo


def _importance_matrix(n_cmp):
    n = jnp.arange(n_cmp)[:, None]
    j = jnp.arange(N_SEL_BLOCKS)[None, :]
    inner = (n >= 4 * j) & (n <= 4 * j + 2)
    edge = (n == 4 * j - 1) | (n == 4 * j + 3)
    return jnp.where(inner, 2.0, jnp.where(edge, 1.0, 0.0)).astype(F32)


def _softmax_over_keys(s, mask):
    s = jnp.where(mask, s, NEG)
    m = jnp.max(s, 0, keepdims=True)
    p = jnp.exp(s - m)
    inv = jnp.where(m > 0.5 * NEG, 1.0 / jnp.maximum(jnp.sum(p, 0, keepdims=True), 1e-30), 0.0)
    return p * inv


def _nsa_kernel(qt_ref, gt_ref, kcvc_ref, kaug_ref, vst_ref, kw_ref, vwt_ref, mt_ref, out_ref,
                kc_s, vct_s, m_s, l_s, acc_s):
    i = pl.program_id(1)
    t0 = i * Q_TILE
    n_cmp = kcvc_ref.shape[0]

    @pl.when(i == 0)
    def _():
        kcvc = kcvc_ref[...]
        kc_s[...] = kcvc[:, 0:LANES].astype(BF16)
        vct_s[...] = jnp.transpose(kcvc[:, LANES:2 * LANES]).astype(BF16)

    lane = lax.broadcasted_iota(jnp.int32, (1, Q_TILE), 1)
    tq = jnp.concatenate([t0 + lane] * N_NSA_HEADS, axis=1)
    low_rows = lax.broadcasted_iota(jnp.int32, (LANES, 1), 0) < HEAD_DIM
    qt = qt_ref[...]
    zero = jnp.zeros((), BF16)
    q_all = jnp.concatenate([jnp.where(low_rows if g == 0 else ~low_rows, qt[c * LANES:(c + 1) * LANES], zero)
                             for g in range(N_KV) for c in range(HPG)], axis=1)

    cend = lax.broadcasted_iota(jnp.int32, (n_cmp, 1), 0) * CMP_STRIDE + (CMP_BLOCK - 1)
    p_c = _softmax_over_keys(_dot(kc_s[...], q_all), cend <= tq)
    o_c = _dot(vct_s[...], p_c.astype(BF16))

    psum = jnp.concatenate([sum(p_c[:, (g * HPG + c) * Q_TILE:(g * HPG + c + 1) * Q_TILE] for c in range(HPG))
                            for g in range(N_KV)], axis=1)
    mt = mt_ref[...]
    imp_t = sum(_dot(mt, part) for part in _split3(psum))
    blk = lax.broadcasted_iota(jnp.int32, (N_SEL_BLOCKS, 1), 0)
    blk_f = blk.astype(F32)
    cur = jnp.concatenate([(t0 + lane) >> 6] * N_KV, axis=1)
    forced = (blk == 0) | (blk == cur) | (blk == cur - 1)
    score = jnp.where(blk <= cur, imp_t + FORCE_BONUS * forced.astype(F32), NEG)
    sel = jnp.zeros(score.shape, jnp.bool_)
    for _ in range(TOP_N):
        best = jnp.max(score, axis=0, keepdims=True)
        first = jnp.min(jnp.where(score == best, blk_f, float(N_SEL_BLOCKS)), axis=0, keepdims=True)
        pick = blk_f == first
        sel = sel | pick
        score = jnp.where(pick, BELOW_NEG, score)
    bias = jnp.where(sel, 0.0, NEG).astype(BF16)
    bias_all = jnp.concatenate([bias[:, g * Q_TILE:(g + 1) * Q_TILE] for g in range(N_KV) for _ in range(HPG)], axis=1)
    q_aug = jnp.concatenate([q_all, bias_all], axis=0)

    m_s[...] = jnp.full(m_s.shape, NEG, F32)
    l_s[...] = jnp.zeros(l_s.shape, F32)
    acc_s[...] = jnp.zeros(acc_s.shape, F32)

    def sweep(kt, causal):
        k0 = pl.multiple_of(kt * KEY_TILE, KEY_TILE)
        s = _dot(kaug_ref[pl.ds(k0, KEY_TILE), :], q_aug)
        if causal:
            kpos = k0 + lax.broadcasted_iota(jnp.int32, (KEY_TILE, 1), 0)
            s = jnp.where(kpos <= tq, s, NEG)
        m_old = m_s[...]
        m_new = jnp.maximum(m_old, jnp.max(s, 0, keepdims=True))
        alpha = jnp.exp(m_old - m_new)
        p = jnp.exp(s - m_new)
        l_s[...] = alpha * l_s[...] + jnp.sum(p, 0, keepdims=True)
        acc_s[...] = alpha * acc_s[...] + _dot(vst_ref[kt], p.astype(BF16))
        m_s[...] = m_new

    last = (t0 + Q_TILE - 1) // KEY_TILE

    def body(kt, carry):
        sweep(kt, False)
        return carry

    lax.fori_loop(0, last, body, 0)
    sweep(last, True)
    o_s = acc_s[...] / l_s[...]

    w0 = pl.multiple_of(jnp.maximum(t0 - WINDOW, 0), Q_TILE)
    n_win = WINDOW + Q_TILE
    dist = tq - (w0 + lax.broadcasted_iota(jnp.int32, (n_win, 1), 0))
    p_w = _softmax_over_keys(_dot(kw_ref[pl.ds(w0, n_win), :], q_all), (dist >= 0) & (dist < WINDOW))
    w_blk = w0 // LANES
    vwt = jnp.concatenate([vwt_ref[w_blk + j] for j in range(n_win // LANES)], axis=1)
    o_w = _dot(vwt, p_w.astype(BF16))

    gt = gt_ref[...]
    gk = [jnp.concatenate([gt[3 * h + k:3 * h + k + 1] for h in range(N_NSA_HEADS)], axis=1) for k in range(3)]
    o = gk[0] * o_c + gk[1] * o_s + gk[2] * o_w
    half = HPG * Q_TILE
    comb = jnp.where(low_rows, o[:, 0:half], o[:, half:2 * half])
    for c in range(HPG):
        out_ref[:, c * LANES:(c + 1) * LANES] = jnp.transpose(comb[:, c * Q_TILE:(c + 1) * Q_TILE]).astype(BF16)


def _nsa_prompt(qt, gt, kcvc, kaug, vst, kw, vwt, mt, *, batch, seq):
    nq = seq // Q_TILE
    n_cols = N_NSA_HEADS * Q_TILE
    n_cmp = kcvc.shape[1]
    col_spec = lambda f: pl.BlockSpec((None, f, Q_TILE), lambda b, i: (b, 0, i))
    seq_spec = lambda w: pl.BlockSpec((seq, w), lambda b, i: (b, 0))
    return pl.pallas_call(
        _nsa_kernel,
        grid=(batch, nq),
        in_specs=[col_spec(4 * LANES), col_spec(LANES),
                  pl.BlockSpec((None, n_cmp, 2 * LANES), lambda b, i: (b, 0, 0)),
                  seq_spec(2 * LANES),
                  pl.BlockSpec((seq // KEY_TILE, LANES, KEY_TILE), lambda b, i: (b, 0, 0)),
                  seq_spec(LANES),
                  pl.BlockSpec((seq // LANES, LANES, LANES), lambda b, i: (b, 0, 0)),
                  _const_spec(mt.shape)],
        out_specs=pl.BlockSpec((Q_TILE, 4 * LANES), lambda b, i: (b * nq + i, 0)),
        out_shape=jax.ShapeDtypeStruct((batch * seq, 4 * LANES), BF16),
        scratch_shapes=[pltpu.VMEM((n_cmp, LANES), BF16), pltpu.VMEM((LANES, n_cmp), BF16),
                        pltpu.VMEM((1, n_cols), F32), pltpu.VMEM((1, n_cols), F32), pltpu.VMEM((LANES, n_cols), F32)],
        compiler_params=_cparams("parallel", "arbitrary"),
        name="nsa_prompt",
    )(qt, gt, kcvc, kaug, vst, kw, vwt, mt)


def _head_sum(a, low):
    s0 = jnp.sum(jnp.where(low, a, 0.0), -1, keepdims=True)
    s1 = jnp.sum(jnp.where(low, 0.0, a), -1, keepdims=True)
    return jnp.where(low, s0, s1)


def _gated_group_norm(o, rg, norm_g, low):
    mu = _head_sum(o, low) * (1.0 / RET_DV)
    d = o - mu
    var = _head_sum(d * d, low) * (1.0 / RET_DV)
    gn = d * lax.rsqrt(var + LN_EPS) * norm_g
    return (rg * jax.nn.sigmoid(rg)) * gn


def _ret_kernel(rq_ref, rk_ref, rv_ref, rg_ref, decay_ref, qdec_ref, kdec_ref, gcl_ref, bd_ref, ng_ref,
                ret_o, st_o, st_s):
    c = pl.program_id(1)

    @pl.when(c == 0)
    def _():
        st_s[...] = jnp.zeros(st_s.shape, F32)

    low = lax.broadcasted_iota(jnp.int32, (1, LANES), 1) < RET_DK
    zero = jnp.zeros((), BF16)
    for p in range(N_PAIR):
        sl = slice(p * LANES, (p + 1) * LANES)
        q, k, v = rq_ref[:, sl], rk_ref[:, sl], rv_ref[:, sl]
        qb, kb, vb = q.astype(BF16), k.astype(BF16), v.astype(BF16)
        in0 = _dot_nt(jnp.where(low, qb, zero), kb) * decay_ref[2 * p]
        in1 = _dot_nt(jnp.where(low, zero, qb), kb) * decay_ref[2 * p + 1]
        inner = jnp.concatenate([in0.astype(BF16), in1.astype(BF16)], axis=1)
        v_split = jnp.concatenate([jnp.where(low, vb, zero), jnp.where(low, zero, vb)], axis=0)
        state = st_s[p]
        o = _dot(inner, v_split) + _dot((q * qdec_ref[:, sl]).astype(BF16), state.astype(BF16))
        kd_t = jnp.transpose(k * kdec_ref[:, sl]).astype(BF16)
        st_s[p] = state * gcl_ref[:, sl] + _dot(kd_t, vb) * bd_ref[...]
        ret_o[:, sl] = _gated_group_norm(o, rg_ref[:, sl], ng_ref[:, sl], low).astype(BF16)

    @pl.when(c == pl.num_programs(1) - 1)
    def _():
        st_o[...] = st_s[...]


def _retention_tables(log_gamma, chunk):
    i = jnp.arange(chunk, dtype=F32)
    diff = i[:, None] - i[None, :]
    decay = jnp.where(diff >= 0, jnp.exp(log_gamma[:, None, None] * jnp.maximum(diff, 0.0)), 0.0)
    per_lane = lambda t: jnp.repeat(t, RET_DK, axis=-1)
    qdec = per_lane(jnp.exp((i + 1.0)[:, None] * log_gamma[None, :]))
    kdec = per_lane(jnp.exp((chunk - 1.0 - i)[:, None] * log_gamma[None, :]))
    gcl = per_lane(jnp.exp(chunk * log_gamma)[None, :])
    r = jnp.arange(LANES)
    bd = ((r[:, None] // RET_DK) == (r[None, :] // RET_DV)).astype(F32)
    return decay, qdec, kdec, gcl, bd


def _retention_prompt(rq, rk, rv, rg, tables, norm_g, *, batch, seq):
    decay, qdec, kdec, gcl, bd = tables
    nc = seq // RET_CHUNK
    row_spec = pl.BlockSpec((RET_CHUNK, RET_WIDTH), lambda b, c: (b * nc + c, 0))
    return pl.pallas_call(
        _ret_kernel,
        grid=(batch, nc),
        in_specs=[row_spec] * 4 + [_const_spec(decay.shape), _const_spec(qdec.shape), _const_spec(kdec.shape),
                                   _const_spec(gcl.shape), _const_spec(bd.shape), _const_spec(norm_g.shape)],
        out_specs=[row_spec, pl.BlockSpec((None, N_PAIR, LANES, LANES), lambda b, c: (b, 0, 0, 0))],
        out_shape=[jax.ShapeDtypeStruct((batch * seq, RET_WIDTH), BF16),
                   jax.ShapeDtypeStruct((batch, N_PAIR, LANES, LANES), F32)],
        scratch_shapes=[pltpu.VMEM((N_PAIR, LANES, LANES), F32)],
        compiler_params=_cparams("parallel", "arbitrary"),
        name="retention",
    )(rq, rk, rv, rg, decay, qdec, kdec, gcl, bd, norm_g)


def _layer_norm(v, g, b):
    mu = jnp.mean(v, -1, keepdims=True)
    d = v - mu
    var = jnp.mean(d * d, -1, keepdims=True)
    return d * lax.rsqrt(var + LN_EPS) * g + b


def _tail_kernel(x_ref, nsa_ref, ret_ref, wo_ref, g1_ref, b1_ref, wu_ref, wd_ref, g2_ref, b2_ref, y_ref, *, alpha):
    mix = _dot(nsa_ref[...], wo_ref[0:NSA_WIDTH, :]) + _dot(ret_ref[...], wo_ref[NSA_WIDTH:, :])
    x1 = _layer_norm(alpha * x_ref[...] + mix, g1_ref[...], b1_ref[...])
    x1b = x1.astype(BF16)
    f = jnp.zeros(x1.shape, F32)
    for j in range(D_FF // D_MODEL):
        sl = slice(j * D_MODEL, (j + 1) * D_MODEL)
        h = jnp.maximum(_dot(x1b, wu_ref[:, sl]), 0.0)
        f = f + _dot((h * h).astype(BF16), wd_ref[sl, :])
    y_ref[...] = _layer_norm(alpha * x1 + f, g2_ref[...], b2_ref[...])


def _tail(x2d, nsa, ret, wo, g1, b1, wu, wd, g2, b2, *, tm, alpha):
    rows = x2d.shape[0]
    row_spec = lambda w: pl.BlockSpec((tm, w), lambda i: (i, 0))
    consts = (wo, g1, b1, wu, wd, g2, b2)
    return pl.pallas_call(
        functools.partial(_tail_kernel, alpha=alpha),
        grid=(rows // tm,),
        in_specs=[row_spec(D_MODEL), row_spec(NSA_WIDTH), row_spec(RET_WIDTH)] + [_const_spec(c.shape) for c in consts],
        out_specs=row_spec(D_MODEL),
        out_shape=jax.ShapeDtypeStruct((rows, D_MODEL), F32),
        compiler_params=_cparams("parallel"),
        name="tail",
    )(x2d, nsa, ret, *consts)


DEC_ROWS = 16
N_DEC_SEL = TOP_N - 1


def _dec_stack_q(q, low):
    rows = lax.broadcasted_iota(jnp.int32, (DEC_ROWS, 1), 0)
    qs = jnp.zeros((DEC_ROWS, LANES), F32)
    for g in range(N_KV):
        for c in range(HPG):
            piece = jnp.where(low if g == 0 else ~low, q[:, c * LANES:(c + 1) * LANES], 0.0)
            qs = jnp.where(rows == g * HPG + c, piece, qs)
    return qs, rows


def _dec_score_kernel(q_ref, kcvc_ref, m_ref, oc_o, idx_o, *, past):
    lane = lax.broadcasted_iota(jnp.int32, (1, LANES), 1)
    low = lane < HEAD_DIM
    qs, rows = _dec_stack_q(q_ref[...], low)
    qsb = qs.astype(BF16)
    kcvc = kcvc_ref[...]
    kc = kcvc[:, 0:LANES].astype(BF16)
    vc = kcvc[:, LANES:2 * LANES].astype(BF16)
    n_cmp = kcvc.shape[0]
    cend = lax.broadcasted_iota(jnp.int32, (1, n_cmp), 1) * CMP_STRIDE + (CMP_BLOCK - 1)
    p = _masked_softmax(_dot_nt(qsb, kc), jnp.broadcast_to(cend <= past, (DEC_ROWS, n_cmp)))
    oc_o[...] = _dot(p.astype(BF16), vc)

    psum = jnp.zeros((DEC_ROWS, n_cmp), F32)
    for g in range(N_KV):
        acc = p[g * HPG:g * HPG + 1]
        for c in range(1, HPG):
            acc = acc + p[g * HPG + c:g * HPG + c + 1]
        psum = jnp.where(rows == g, acc, psum)
    m = m_ref[...]
    imp = sum(_dot(part, m) for part in _split3(psum))
    lane_f = lane.astype(F32)
    forced = (lane == 0) | (lane == N_SEL_BLOCKS - 1)
    score = imp + FORCE_BONUS * forced.astype(F32)
    picked = jnp.zeros((DEC_ROWS, LANES), F32)
    for r in range(N_DEC_SEL):
        best = jnp.max(score, -1, keepdims=True)
        first = jnp.min(jnp.where(score == best, lane_f, float(LANES)), -1, keepdims=True)
        picked = jnp.where(lane == r, first, picked)
        score = jnp.where(lane_f == first, BELOW_NEG, score)
    idx_o[...] = picked.astype(jnp.int32)


def _dec_score(q3, kcvc, m, *, past):
    nb = q3.shape[0]
    n_cmp = kcvc.shape[1]
    return pl.pallas_call(
        functools.partial(_dec_score_kernel, past=past),
        grid=(nb,),
        in_specs=[pl.BlockSpec((None, 1, 4 * LANES), lambda b: (b, 0, 0)),
                  pl.BlockSpec((None, n_cmp, 2 * LANES), lambda b: (b, 0, 0)),
                  _const_spec(m.shape)],
        out_specs=[pl.BlockSpec((None, DEC_ROWS, LANES), lambda b: (b, 0, 0))] * 2,
        out_shape=[jax.ShapeDtypeStruct((nb, DEC_ROWS, LANES), F32), jax.ShapeDtypeStruct((nb, DEC_ROWS, LANES), jnp.int32)],
        compiler_params=_cparams("parallel"),
        name="dec_score",
    )(q3, kcvc, m)


def _round_bf16(x):
    return x.astype(BF16).astype(F32)


def _dec_attn_kernel(pt_ref, idx_ref, q_ref, g_ref, oc_ref, kvn_ref, wn_ref, cache_ref, cw_ref,
                     rq_ref, rk_ref, rv_ref, rg_ref, st_ref, gl_ref, ng_ref,
                     nsa_o, ret_o, win_o, st_o, kbuf, sem):
    b = pl.program_id(0)
    nb = pl.num_programs(0)
    slot = b % 2

    def blk_copy(bb, sl, g, k):
        j = idx_ref[bb, g, k]
        off = pl.multiple_of((j & 1) * SEL_BLOCK, SEL_BLOCK)
        return pltpu.make_async_copy(
            cache_ref.at[pt_ref[bb, j >> 1], pl.ds(off, SEL_BLOCK), pl.ds(2 * LANES, 2 * LANES)],
            kbuf.at[sl, g, pl.ds(k * SEL_BLOCK, SEL_BLOCK), :],
            sem.at[sl])

    def start_all(bb, sl):
        for g in range(N_KV):
            for k in range(N_DEC_SEL):
                blk_copy(bb, sl, g, k).start()

    @pl.when(b == 0)
    def _():
        start_all(0, 0)

    @pl.when(b + 1 < nb)
    def _():
        start_all(b + 1, 1 - slot)

    lane = lax.broadcasted_iota(jnp.int32, (1, LANES), 1)
    low = lane < HEAD_DIM
    qs, rows = _dec_stack_q(q_ref[...], low)
    qsb = qs.astype(BF16)
    g0_rows = rows < HPG

    cw = cw_ref[...]
    wb = cw.shape[0]
    wn = wn_ref[...]
    col = lax.broadcasted_iota(jnp.int32, (1, wb), 1)
    valid = jnp.broadcast_to((wb - col) < WINDOW, (DEC_ROWS, wb))
    s_w = jnp.where(valid, _dot_nt(qsb, cw[:, 0:LANES].astype(BF16)), NEG)
    s_wn = jnp.sum(qs * _round_bf16(wn[:, 0:LANES]), -1, keepdims=True)
    m_w = jnp.maximum(jnp.max(s_w, -1, keepdims=True), s_wn)
    p_w = jnp.exp(s_w - m_w) * valid.astype(F32)
    p_wn = jnp.exp(s_wn - m_w)
    l_w = jnp.maximum(jnp.sum(p_w, -1, keepdims=True) + p_wn, 1e-30)
    o_w = _dot((p_w / l_w).astype(BF16), cw[:, LANES:2 * LANES].astype(BF16)) + _round_bf16(p_wn / l_w) * _round_bf16(wn[:, LANES:2 * LANES])
    shifted = pltpu.roll(cw, wb - 1, 0)
    win_o[...] = jnp.where(lax.broadcasted_iota(jnp.int32, (wb, 1), 0) == wb - 1, wn, shifted)

    for g in range(N_KV):
        for k in range(N_DEC_SEL):
            blk_copy(b, slot, g, k).wait()
    kvn = kvn_ref[...]
    s_g = [_dot_nt(qsb, kbuf[slot, g, :, 0:LANES].astype(BF16)) for g in range(N_KV)]
    s_s = jnp.where(g0_rows, s_g[0], s_g[1])
    s_sn = jnp.sum(qs * _round_bf16(kvn[:, 2 * LANES:3 * LANES]), -1, keepdims=True)
    m_sel = jnp.maximum(jnp.max(s_s, -1, keepdims=True), s_sn)
    p_s = jnp.exp(s_s - m_sel)
    p_sn = jnp.exp(s_sn - m_sel)
    l_sel = jnp.maximum(jnp.sum(p_s, -1, keepdims=True) + p_sn, 1e-30)
    pn = (p_s / l_sel).astype(BF16)
    o_g = [_dot(pn, kbuf[slot, g, :, LANES:2 * LANES].astype(BF16)) for g in range(N_KV)]
    o_s = jnp.where(g0_rows, o_g[0], o_g[1]) + _round_bf16(p_sn / l_sel) * _round_bf16(kvn[:, 3 * LANES:4 * LANES])

    gates = g_ref[...]
    gcol = [jnp.zeros((DEC_ROWS, 1), F32) for _ in range(3)]
    for h in range(N_NSA_HEADS):
        for k in range(3):
            gcol[k] = jnp.where(rows == h, gates[:, 3 * h + k:3 * h + k + 1], gcol[k])
    o = gcol[0] * oc_ref[...] + gcol[1] * o_s + gcol[2] * o_w
    for c in range(HPG):
        nsa_o[:, c * LANES:(c + 1) * LANES] = jnp.where(low, o[c:c + 1], o[HPG + c:HPG + c + 1])

    for p in range(N_PAIR):
        sl = slice(p * LANES, (p + 1) * LANES)
        q, k, v, gl = rq_ref[:, sl], rk_ref[:, sl], rv_ref[:, sl], gl_ref[:, sl]
        state = st_ref[p]
        inner = _head_sum(q * k, low)

        def per_dk(row):
            t = jnp.transpose(jnp.broadcast_to(row, (LANES, LANES)))
            return jnp.where(low, t[0:RET_DK], t[RET_DK:2 * RET_DK])

        o_r = inner * v + jnp.sum(per_dk(q * gl) * state, axis=0, keepdims=True)
        st_o[p] = gl * state + per_dk(k) * v
        ret_o[:, sl] = _gated_group_norm(o_r, rg_ref[:, sl], ng_ref[:, sl], low)


def _dec_attn(page_table, idx, q3, gates3, oc, kvn3, wn3, cache4, cwin, rq3, rk3, rv3, rg3, state, gl, ng):
    nb = q3.shape[0]
    wb = cwin.shape[1]
    row3 = lambda w: pl.BlockSpec((None, 1, w), lambda b, *_: (b, 0, 0))
    st_spec = pl.BlockSpec((None, N_PAIR, RET_DK, LANES), lambda b, *_: (b, 0, 0, 0))
    win_spec = pl.BlockSpec((None, wb, 2 * LANES), lambda b, *_: (b, 0, 0))
    grid_spec = pltpu.PrefetchScalarGridSpec(
        num_scalar_prefetch=2,
        grid=(nb,),
        in_specs=[row3(4 * LANES), row3(LANES), pl.BlockSpec((None, DEC_ROWS, LANES), lambda b, *_: (b, 0, 0)),
                  row3(4 * LANES), row3(2 * LANES), pl.BlockSpec(memory_space=pl.ANY), win_spec,
                  row3(4 * LANES), row3(4 * LANES), row3(4 * LANES), row3(4 * LANES), st_spec,
                  pl.BlockSpec(gl.shape, lambda b, *_: (0, 0)), pl.BlockSpec(ng.shape, lambda b, *_: (0, 0))],
        out_specs=[row3(4 * LANES), row3(4 * LANES), win_spec, st_spec],
        scratch_shapes=[pltpu.VMEM((2, N_KV, N_DEC_SEL * SEL_BLOCK, 2 * LANES), F32), pltpu.SemaphoreType.DMA((2,))],
    )
    return pl.pallas_call(
        _dec_attn_kernel,
        grid_spec=grid_spec,
        out_shape=[jax.ShapeDtypeStruct((nb, 1, 4 * LANES), F32), jax.ShapeDtypeStruct((nb, 1, 4 * LANES), F32),
                   jax.ShapeDtypeStruct((nb, wb, 2 * LANES), F32), jax.ShapeDtypeStruct(state.shape, F32)],
        compiler_params=_cparams("arbitrary"),
        name="dec_attn",
    )(page_table, idx, q3, gates3, oc, kvn3, wn3, cache4, cwin, rq3, rk3, rv3, rg3, state, gl, ng)


def _permute_w_in(w):
    p0 = NSA_WIDTH
    p1 = p0 + 6 * KV_WIDTH
    p2 = p1 + 3 * N_NSA_HEADS
    wq = w[:, :p0].reshape(D_MODEL, N_NSA_HEADS, HEAD_DIM)[:, jnp.array(HEAD_PERM)].reshape(D_MODEL, p0)
    w_gates = jnp.pad(w[:, p1:p2], ((0, 0), (0, LANES - 3 * N_NSA_HEADS)))
    return jnp.concatenate([wq, w[:, p0:p1], w[:, p2:], w_gates], axis=1).astype(BF16)


def _permute_w_o(w):
    w_nsa = w[:NSA_WIDTH].reshape(N_NSA_HEADS, HEAD_DIM, D_MODEL)[jnp.array(HEAD_PERM)].reshape(NSA_WIDTH, D_MODEL)
    return jnp.concatenate([w_nsa, w[NSA_WIDTH:]], axis=0).astype(BF16)


def kernel(x_prompt, x_sample, cache_kv, cache_win, state_ret, page_table, w_in, w_cmp_k, w_cmp_v, pos_cmp_k, pos_cmp_v, ret_norm_g, w_o, ln1_g, ln1_b, w_up, w_down, ln2_g, ln2_b):
    depth = w_in.shape[0]
    assert depth == 1 and x_sample.shape[1] == 1
    batch, seq, _ = x_prompt.shape
    nb = x_sample.shape[0]
    n_pages = page_table.shape[1]
    past = n_pages * PAGE_SIZE
    assert past // SEL_BLOCK == N_SEL_BLOCKS and cache_win.shape[2] == WINDOW
    alpha = (2 * depth) ** 0.25
    log_gamma = jnp.log1p(-jnp.exp2(-5.0 - jnp.arange(N_RET_HEADS, dtype=F32)))

    w_perm = _permute_w_in(w_in[0])
    wo = _permute_w_o(w_o[0])
    wu, wd = w_up[0].astype(BF16), w_down[0].astype(BF16)
    vec = lambda a: a[0].reshape(1, -1)
    cmp_w = _compress_weights(w_cmp_k[0], w_cmp_v[0], pos_cmp_k[0], pos_cmp_v[0])
    tail_consts = (wo, vec(ln1_g), vec(ln1_b), wu, wd, vec(ln2_g), vec(ln2_b))
    norm_g = vec(ret_norm_g)

    tm = 512
    pos_p = jnp.arange(seq, dtype=jnp.int32)
    tabs_p = _rope_tables(pos_p, ROPE_THETA, ROPE_DIMS) + _rope_tables(pos_p, RET_THETA, RET_DK)
    x2d = x_prompt.reshape(batch * seq, D_MODEL)
    kv4, win, rq, rk, rv, rg, qt, gt, kaug, vst, kw, vwt = _project(x2d, w_perm, tabs_p, tm=tm, seq=seq, prompt=True)
    seq_pages = seq // PAGE_SIZE
    pt_p = jnp.arange(batch * seq_pages, dtype=jnp.int32).reshape(batch, seq_pages)
    kcvc_p = _compress(pt_p, kv4.reshape(batch * seq_pages, PAGE_SIZE, 4 * KV_WIDTH), *cmp_w)
    o_nsa = _nsa_prompt(qt, gt, kcvc_p, kaug, vst, kw, vwt, _importance_matrix(seq // CMP_STRIDE).T.astype(BF16), batch=batch, seq=seq)
    ret, st_p = _retention_prompt(rq, rk, rv, rg, _retention_tables(log_gamma, RET_CHUNK), norm_g, batch=batch, seq=seq)
    y_p = _tail(x2d, o_nsa, ret, *tail_consts, tm=tm, alpha=alpha)

    y_prompt = y_p.reshape(batch, seq, D_MODEL)
    kv_prompt = kv4.reshape(1, batch, seq, 4, N_KV, HEAD_DIM)
    wlen = min(WINDOW, seq)
    win_prompt = win.reshape(batch, seq, 2, N_KV, HEAD_DIM)[None, :, seq - wlen:]
    st6 = st_p.reshape(batch, N_PAIR, 2, RET_DK, 2, RET_DV)
    ret_prompt = jnp.stack([st6[:, :, 0, :, 0, :], st6[:, :, 1, :, 1, :]], axis=2).reshape(1, batch, N_RET_HEADS, RET_DK, RET_DV)

    pos_s = jnp.full((nb,), past, dtype=jnp.int32)
    tabs_s = _rope_tables(pos_s, ROPE_THETA, ROPE_DIMS) + _rope_tables(pos_s, RET_THETA, RET_DK)
    xs2d = x_sample.reshape(nb, D_MODEL)
    kv4_d, win_d, rq_d, rk_d, rv_d, rg_d, q_d, gates_d = _project(xs2d, w_perm, tabs_s, tm=nb, seq=seq, prompt=False)
    cache4 = cache_kv[0].reshape(cache_kv.shape[1], PAGE_SIZE, 4 * KV_WIDTH)
    kcvc_d = _compress(page_table, cache4, *cmp_w)
    row3 = lambda a: a.reshape(nb, 1, a.shape[-1])
    q3 = row3(q_d)
    oc_d, idx_d = _dec_score(q3, kcvc_d, _importance_matrix(past // CMP_STRIDE).astype(BF16), past=past)
    idx = idx_d[:, :N_KV, :TOP_N]
    state = state_ret[0].astype(F32).reshape(nb, N_PAIR, 2, RET_DK, RET_DV).transpose(0, 1, 3, 2, 4).reshape(nb, N_PAIR, RET_DK, LANES)
    gl = jnp.repeat(jnp.exp(log_gamma), RET_DK)[None, :]
    cwin = cache_win[0].reshape(nb, WINDOW, 2 * KV_WIDTH)
    nsa_d, ret_d, win_s, st_d = _dec_attn(page_table, idx, q3, row3(gates_d), oc_d, row3(kv4_d), row3(win_d), cache4, cwin,
                                          row3(rq_d), row3(rk_d), row3(rv_d), row3(rg_d), state, gl, norm_g)
    y_s = _tail(xs2d, nsa_d.reshape(nb, NSA_WIDTH).astype(BF16), ret_d.reshape(nb, RET_WIDTH).astype(BF16), *tail_consts, tm=nb, alpha=alpha)

    y_sample = y_s.reshape(nb, 1, D_MODEL)
    kv_sample = kv4_d.reshape(1, nb, 1, 4, N_KV, HEAD_DIM)
    win_sample = win_s.reshape(1, nb, WINDOW, 2, N_KV, HEAD_DIM)
    ret_sample = st_d.reshape(nb, N_PAIR, RET_DK, 2, RET_DV).transpose(0, 1, 3, 2, 4).reshape(1, nb, N_RET_HEADS, RET_DK, RET_DV)
    return (y_prompt, y_sample, kv_prompt, kv_sample, win_prompt, win_sample, ret_prompt, ret_sample)
```

```python
import functools

import jax
import jax.numpy as jnp
from jax import lax
from jax.experimental import pallas as pl
from jax.experimental.pallas import tpu as pltpu

F32 = jnp.float32
BF16 = jnp.bfloat16

D_MODEL = 1024
HEAD_DIM = 64
N_NSA_HEADS = 8
N_KV = 2
HPG = N_NSA_HEADS // N_KV
N_RET_HEADS = 8
RET_DK = 64
RET_DV = 64
NSA_WIDTH = N_NSA_HEADS * HEAD_DIM
RET_WIDTH = N_RET_HEADS * RET_DV
KV_WIDTH = N_KV * HEAD_DIM
D_FF = 4 * D_MODEL
CMP_STRIDE = 16
CMP_BLOCK = 32
SEL_BLOCK = 64
TOP_N = 16
WINDOW = 512
RET_CHUNK = 128
PAGE_SIZE = 128
ROPE_THETA = 500000.0
ROPE_DIMS = HEAD_DIM // 4
RET_THETA = 10000.0
LN_EPS = 1e-5
FORCE_BONUS = 1e4
NEG = -1e30
BELOW_NEG = -3e38

LANES = 128
SUBLANES = 8
VMEM_LIMIT = 56 * 1024 * 1024

LOG2E = 1.4426950408889634
BF16_SUBLANES = 16
ONES_ROWS = BF16_SUBLANES
Q_TILE = 128
KEY_TILE = 512
N_PAIR = N_RET_HEADS // 2
N_SEL_BLOCKS = 128

U_Q, U_KV, U_RQ, U_RK, U_RV, U_RG, U_G, U_TOTAL = 0, 4, 10, 14, 18, 22, 26, 27
HEAD_PERM = (0, 4, 1, 5, 2, 6, 3, 7)


def _dot(a, b):
    return jnp.dot(a, b, preferred_element_type=F32)


def _dot_nt(a, b):
    return lax.dot_general(a, b, (((1,), (1,)), ((), ())), preferred_element_type=F32)


def _cparams(*sem):
    return pltpu.CompilerParams(dimension_semantics=sem, vmem_limit_bytes=VMEM_LIMIT)


def _const_spec(shape):
    nd = len(shape)
    return pl.BlockSpec(shape, lambda *_: (0,) * nd)


def _rope(u, c, s_fwd, s_bwd, half):
    return u * c + pltpu.roll(u, LANES - half, 1) * s_fwd + pltpu.roll(u, half, 1) * s_bwd


def _proj_kernel(x_ref, w_ref, cq_ref, sqf_ref, sqb_ref, cr_ref, srf_ref, srb_ref, *outs, tm, seq, prompt):
    if prompt:
        kv4_o, win_o, rq_o, rk_o, rv_o, rg_o, qt_o, gt_o, kaug_o, vst_o, kw_o, vwt_o = outs
    else:
        kv4_o, win_o, rq_o, rk_o, rv_o, rg_o, q_o, g_o = outs
    xb = x_ref[...].astype(BF16)

    def seg(u0, n):
        return _dot(xb, w_ref[:, u0 * LANES:(u0 + n) * LANES])

    def unit(z, u):
        return z[:, u * LANES:(u + 1) * LANES]

    cq, sqf, sqb = cq_ref[...], sqf_ref[...], sqb_ref[...]
    cr, srf, srb = cr_ref[...], srf_ref[...], srb_ref[...]
    nsa_half, ret_half = ROPE_DIMS // 2, RET_DK // 2

    zq = seg(U_Q, 4)
    q = jnp.concatenate([_rope(unit(zq, c), cq, sqf, sqb, nsa_half) for c in range(4)], axis=1) * (HEAD_DIM ** -0.5)
    gates = jax.nn.sigmoid(seg(U_G, 1))
    zkv = seg(U_KV, 6)
    k_slc = _rope(unit(zkv, 2), cq, sqf, sqb, nsa_half)
    k_win = _rope(unit(zkv, 4), cq, sqf, sqb, nsa_half)
    kv4_o[:, 0:2 * LANES] = zkv[:, 0:2 * LANES]
    kv4_o[:, 2 * LANES:3 * LANES] = k_slc
    kv4_o[:, 3 * LANES:4 * LANES] = unit(zkv, 3)
    win_o[:, 0:LANES] = k_win
    win_o[:, LANES:2 * LANES] = unit(zkv, 5)
    if prompt:
        qt_o[...] = jnp.transpose(q * LOG2E).astype(BF16)
        gt_o[...] = jnp.transpose(gates)
        row = pl.program_id(0) * tm + lax.broadcasted_iota(jnp.int32, (tm, 1), 0)
        blk = (row & (seq - 1)) >> 6
        onehot = lax.broadcasted_iota(jnp.int32, (tm, LANES), 1) == blk
        kaug_o[:, 0:LANES] = k_slc.astype(BF16)
        kaug_o[:, LANES:2 * LANES] = jnp.where(onehot, 1.0, 0.0).astype(BF16)
        vst_o[0:LANES, :] = jnp.transpose(unit(zkv, 3)).astype(BF16)
        vst_o[LANES:, :] = jnp.ones((ONES_ROWS, tm), BF16)
        kw_o[...] = k_win.astype(BF16)
        vwt = jnp.transpose(unit(zkv, 5)).astype(BF16)
        for j in range(tm // LANES):
            vwt_o[j] = vwt[:, j * LANES:(j + 1) * LANES]
    else:
        q_o[...] = q.astype(BF16).astype(F32)
        g_o[...] = gates

    zrq = seg(U_RQ, 4)
    zrk = seg(U_RK, 4)
    for c in range(4):
        sl = slice(c * LANES, (c + 1) * LANES)
        rq_o[:, sl] = _rope(unit(zrq, c), cr, srf, srb, ret_half)
        rk_o[:, sl] = _rope(unit(zrk, c), cr, srf, srb, ret_half) * (RET_DK ** -0.5)
    rv_o[...] = seg(U_RV, 4)
    rg_o[...] = seg(U_RG, 4)


def _rope_tables(pos, theta, n_rot):
    half = n_rot // 2
    inv = theta ** (-jnp.arange(half, dtype=F32) / half)
    ang = pos.astype(F32)[:, None] * inv[None, :]
    cos, sin = jnp.cos(ang), jnp.sin(ang)
    n = pos.shape[0]
    rest = HEAD_DIM - n_rot
    c = jnp.concatenate([cos, cos, jnp.ones((n, rest), F32)], 1)
    s_fwd = jnp.concatenate([-sin, jnp.zeros((n, half + rest), F32)], 1)
    s_bwd = jnp.concatenate([jnp.zeros((n, half), F32), sin, jnp.zeros((n, rest), F32)], 1)
    rep = LANES // HEAD_DIM
    return tuple(jnp.tile(t, (1, rep)) for t in (c, s_fwd, s_bwd))


def _project(x2d, w_perm, tables, *, tm, seq, prompt):
    rows = x2d.shape[0]
    n_tab_tiles = tables[0].shape[0] // tm
    row_spec = lambda w: pl.BlockSpec((tm, w), lambda i: (i, 0))
    tab_spec = pl.BlockSpec((tm, LANES), lambda i: (i % n_tab_tiles, 0))
    outs = [(row_spec(4 * LANES), (rows, 4 * LANES), F32), (row_spec(2 * LANES), (rows, 2 * LANES), F32)]
    outs += [(row_spec(4 * LANES), (rows, 4 * LANES), F32)] * 4
    if prompt:
        assert tm == KEY_TILE and seq % tm == 0
        batch, tpb = rows // seq, seq // tm
        feat_spec = lambda f: pl.BlockSpec((None, f, tm), lambda i: (i // tpb, 0, i % tpb))
        outs += [(feat_spec(4 * LANES), (batch, 4 * LANES, seq), BF16),
                 (feat_spec(LANES), (batch, LANES, seq), F32),
                 (row_spec(2 * LANES), (rows, 2 * LANES), BF16),
                 (pl.BlockSpec((None, LANES + ONES_ROWS, tm), lambda i: (i, 0, 0)), (rows // tm, LANES + ONES_ROWS, tm), BF16),
                 (row_spec(LANES), (rows, LANES), BF16),
                 (pl.BlockSpec((tm // LANES, LANES, LANES), lambda i: (i, 0, 0)), (rows // LANES, LANES, LANES), BF16)]
    else:
        outs += [(row_spec(4 * LANES), (rows, 4 * LANES), F32), (row_spec(LANES), (rows, LANES), F32)]
    return pl.pallas_call(
        functools.partial(_proj_kernel, tm=tm, seq=seq, prompt=prompt),
        grid=(rows // tm,),
        in_specs=[row_spec(D_MODEL), _const_spec(w_perm.shape)] + [tab_spec] * 6,
        out_specs=[s for s, _, _ in outs],
        out_shape=[jax.ShapeDtypeStruct(shape, dt) for _, shape, dt in outs],
        compiler_params=_cparams("parallel"),
        name="proj_prompt" if prompt else "proj_decode",
    )(x2d, w_perm, *tables)


def _cmp_kernel(pt_ref, cache_ref, wlo_ref, whi_ref, plo_ref, phi_ref, out_ref, slab, sem, *, n_pages):
    b = pl.program_id(0)
    nb = pl.num_programs(0)
    slot = b % 2

    def page_copy(bb, sl, p, kv):
        return pltpu.make_async_copy(
            cache_ref.at[pt_ref[bb, p], :, pl.ds(kv * LANES, LANES)],
            slab.at[sl, kv, pl.ds(p * PAGE_SIZE, PAGE_SIZE), :],
            sem.at[sl])

    def start_all(bb, sl):
        for p in range(n_pages):
            for kv in range(2):
                page_copy(bb, sl, p, kv).start()

    @pl.when(b == 0)
    def _():
        start_all(0, 0)

    @pl.when(b + 1 < nb)
    def _():
        start_all(b + 1, 1 - slot)

    for p in range(n_pages):
        for kv in range(2):
            page_copy(b, slot, p, kv).wait()

    n_chunks = n_pages * PAGE_SIZE // CMP_STRIDE
    acc_lo = jnp.zeros((n_chunks, 2 * LANES), F32)
    acc_hi = jnp.zeros((n_chunks, 2 * LANES), F32)
    for r in range(CMP_STRIDE):
        xr = jnp.concatenate([slab[slot, kv, pl.ds(r, n_chunks, stride=CMP_STRIDE), :] for kv in range(2)], axis=1)
        acc_lo = acc_lo + _dot((xr + plo_ref[r:r + 1, :]).astype(BF16), wlo_ref[r])
        acc_hi = acc_hi + _dot((xr + phi_ref[r:r + 1, :]).astype(BF16), whi_ref[r])
    nxt = pltpu.roll(acc_hi, n_chunks - 1, 0)
    last = lax.broadcasted_iota(jnp.int32, (n_chunks, 1), 0) == n_chunks - 1
    out_ref[...] = jnp.where(last, 0.0, acc_lo + nxt)


def _compress(page_table, cache4, wlo, whi, plo, phi):
    nb, n_pages = page_table.shape
    n_chunks = n_pages * PAGE_SIZE // CMP_STRIDE
    grid_spec = pltpu.PrefetchScalarGridSpec(
        num_scalar_prefetch=1,
        grid=(nb,),
        in_specs=[pl.BlockSpec(memory_space=pl.ANY),
                  pl.BlockSpec(wlo.shape, lambda b, pt: (0, 0, 0)),
                  pl.BlockSpec(whi.shape, lambda b, pt: (0, 0, 0)),
                  pl.BlockSpec(plo.shape, lambda b, pt: (0, 0)),
                  pl.BlockSpec(phi.shape, lambda b, pt: (0, 0))],
        out_specs=pl.BlockSpec((None, n_chunks, 2 * LANES), lambda b, pt: (b, 0, 0)),
        scratch_shapes=[pltpu.VMEM((2, 2, n_pages * PAGE_SIZE, LANES), F32), pltpu.SemaphoreType.DMA((2,))],
    )
    return pl.pallas_call(
        functools.partial(_cmp_kernel, n_pages=n_pages),
        grid_spec=grid_spec,
        out_shape=jax.ShapeDtypeStruct((nb, n_chunks, 2 * LANES), F32),
        compiler_params=_cparams("arbitrary"),
        name="compress",
    )(page_table, cache4, wlo, whi, plo, phi)


def _compress_weights(w_k, w_v, pos_k, pos_v):
    def bd(wk_r, wv_r):
        z = jnp.zeros((HEAD_DIM, HEAD_DIM), F32)
        rows = [[wk_r, z, z, z], [z, wk_r, z, z], [z, z, wv_r, z], [z, z, z, wv_r]]
        return jnp.block(rows)

    w_all = jax.vmap(bd)(w_k, w_v).astype(BF16)
    p_all = jnp.concatenate([pos_k, pos_k, pos_v, pos_v], axis=1)
    return w_all[:CMP_STRIDE], w_all[CMP_STRIDE:], p_all[:CMP_STRIDE], p_all[CMP_STRIDE:]


def _masked_softmax(s, mask):
    s = jnp.where(mask, s, NEG)
    m = jnp.max(s, -1, keepdims=True)
    p = jnp.exp(s - m) * mask.astype(F32)
    return p / jnp.maximum(jnp.sum(p, -1, keepdims=True), 1e-30)


def _split3(x):
    hi = x.astype(BF16)
    r1 = x - hi.astype(F32)
    mid = r1.astype(BF16)
    lo = (r1 - mid.astype(F32)).astype(BF16)
    return hi, mid, lo


def _importance_matrix(n_cmp):
    n = jnp.arange(n_cmp)[:, None]
    j = jnp.arange(N_SEL_BLOCKS)[None, :]
    inner = (n >= 4 * j) & (n <= 4 * j + 2)
    edge = (n == 4 * j - 1) | (n == 4 * j + 3)
    return jnp.where(inner, 2.0, jnp.where(edge, 1.0, 0.0)).astype(F32)


def _softmax_over_keys(s, mask):
    s = jnp.where(mask, s, NEG)
    m = jnp.max(s, 0, keepdims=True)
    p = jnp.exp2(s - m)
    inv = jnp.where(m > 0.5 * NEG, 1.0 / jnp.maximum(jnp.sum(p, 0, keepdims=True), 1e-30), 0.0)
    return p * inv


def _nsa_kernel(qt_ref, gt_ref, kcvc_ref, kaug_ref, vst_ref, kw_ref, vwt_ref, mt_ref, out_ref,
                kc_s, vct_s, m_s, acc_s, s_buf, tmax_s):
    i = pl.program_id(1)
    t0 = i * Q_TILE
    n_cmp = kcvc_ref.shape[0]

    @pl.when(i == 0)
    def _():
        kcvc = kcvc_ref[...]
        kc_s[...] = kcvc[:, 0:LANES].astype(BF16)
        vct_s[...] = jnp.transpose(kcvc[:, LANES:2 * LANES]).astype(BF16)

    lane = lax.broadcasted_iota(jnp.int32, (1, Q_TILE), 1)
    tq = jnp.concatenate([t0 + lane] * N_NSA_HEADS, axis=1)
    low_rows = lax.broadcasted_iota(jnp.int32, (LANES, 1), 0) < HEAD_DIM
    qt = qt_ref[...]
    zero = jnp.zeros((), BF16)
    q_all = jnp.concatenate([jnp.where(low_rows if g == 0 else ~low_rows, qt[c * LANES:(c + 1) * LANES], zero)
                             for g in range(N_KV) for c in range(HPG)], axis=1)

    cend = lax.broadcasted_iota(jnp.int32, (n_cmp, 1), 0) * CMP_STRIDE + (CMP_BLOCK - 1)
    p_c = _softmax_over_keys(_dot(kc_s[...], q_all), cend <= tq)
    o_c = _dot(vct_s[...], p_c.astype(BF16))

    w0 = pl.multiple_of(jnp.maximum(t0 - WINDOW, 0), Q_TILE)
    n_win = WINDOW + Q_TILE
    dist = tq - (w0 + lax.broadcasted_iota(jnp.int32, (n_win, 1), 0))
    p_w = _softmax_over_keys(_dot(kw_ref[pl.ds(w0, n_win), :], q_all), (dist >= 0) & (dist < WINDOW))
    w_blk = w0 // LANES
    vwt = jnp.concatenate([vwt_ref[w_blk + j] for j in range(n_win // LANES)], axis=1)
    o_w = _dot(vwt, p_w.astype(BF16))
    gt = gt_ref[...]
    gk = [jnp.concatenate([gt[3 * h + k:3 * h + k + 1] for h in range(N_NSA_HEADS)], axis=1) for k in range(3)]
    o_cw = gk[0] * o_c + gk[2] * o_w

    psum = jnp.concatenate([sum(p_c[:, (g * HPG + c) * Q_TILE:(g * HPG + c + 1) * Q_TILE] for c in range(HPG))
                            for g in range(N_KV)], axis=1)
    mt = mt_ref[...]
    imp_t = sum(_dot(mt, part) for part in _split3(psum))
    blk = lax.broadcasted_iota(jnp.int32, (N_SEL_BLOCKS, 1), 0)
    blk_f = blk.astype(F32)
    cur = jnp.concatenate([(t0 + lane) >> 6] * N_KV, axis=1)
    forced = (blk == 0) | (blk == cur) | (blk == cur - 1)
    score = jnp.where(blk <= cur, imp_t + FORCE_BONUS * forced.astype(F32), NEG)
    bias = jnp.full(score.shape, NEG, F32)
    for _ in range(TOP_N):
        best = jnp.max(score, axis=0, keepdims=True)
        first = jnp.min(jnp.where(score == best, blk_f, float(N_SEL_BLOCKS)), axis=0, keepdims=True)
        pick = blk_f == first
        bias = jnp.where(pick, 0.0, bias)
        score = jnp.where(pick, BELOW_NEG, score)
    bias = bias.astype(BF16)
    bias_all = jnp.concatenate([bias[:, g * Q_TILE:(g + 1) * Q_TILE] for g in range(N_KV) for _ in range(HPG)], axis=1)
    q_aug = jnp.concatenate([q_all, bias_all], axis=0)

    m_s[...] = jnp.full(m_s.shape, NEG, F32)
    acc_s[...] = jnp.zeros(acc_s.shape, F32)

    def scores(kt, causal):
        k0 = pl.multiple_of(kt * KEY_TILE, KEY_TILE)
        s = _dot(kaug_ref[pl.ds(k0, KEY_TILE), :], q_aug)
        if causal:
            kpos = k0 + lax.broadcasted_iota(jnp.int32, (KEY_TILE, 1), 0)
            s = jnp.where(kpos <= tq, s, NEG)
        s_buf[...] = s
        tmax_s[...] = jnp.max(s, 0, keepdims=True)

    def accumulate(kt):
        m_old = m_s[...]
        m_new = jnp.maximum(m_old, tmax_s[...])
        alpha = jnp.exp2(m_old - m_new)
        p = jnp.exp2(s_buf[...] - m_new)
        acc_s[...] = alpha * acc_s[...] + _dot(vst_ref[kt], p.astype(BF16))
        m_s[...] = m_new

    last = (t0 + Q_TILE - 1) // KEY_TILE

    @pl.when(last == 0)
    def _():
        scores(0, True)

    @pl.when(last > 0)
    def _():
        scores(0, False)

        def body(kt, carry):
            accumulate(kt)
            scores(kt + 1, False)
            return carry

        lax.fori_loop(0, last - 1, body, 0)
        accumulate(last - 1)
        scores(last, True)

    accumulate(last)
    o_s = acc_s[0:LANES, :] * (1.0 / acc_s[LANES:LANES + 1, :])

    o = o_cw + gk[1] * o_s
    half = HPG * Q_TILE
    comb = jnp.where(low_rows, o[:, 0:half], o[:, half:2 * half])
    for c in range(HPG):
        out_ref[:, c * LANES:(c + 1) * LANES] = jnp.transpose(comb[:, c * Q_TILE:(c + 1) * Q_TILE]).astype(BF16)


def _nsa_prompt(qt, gt, kcvc, kaug, vst, kw, vwt, mt, *, batch, seq):
    nq = seq // Q_TILE
    n_cols = N_NSA_HEADS * Q_TILE
    n_cmp = kcvc.shape[1]
    col_spec = lambda f: pl.BlockSpec((None, f, Q_TILE), lambda b, i: (b, 0, i))
    seq_spec = lambda w: pl.BlockSpec((seq, w), lambda b, i: (b, 0))
    return pl.pallas_call(
        _nsa_kernel,
        grid=(batch, nq),
        in_specs=[col_spec(4 * LANES), col_spec(LANES),
                  pl.BlockSpec((None, n_cmp, 2 * LANES), lambda b, i: (b, 0, 0)),
                  seq_spec(2 * LANES),
                  pl.BlockSpec((seq // KEY_TILE, LANES + ONES_ROWS, KEY_TILE), lambda b, i: (b, 0, 0)),
                  seq_spec(LANES),
                  pl.BlockSpec((seq // LANES, LANES, LANES), lambda b, i: (b, 0, 0)),
                  _const_spec(mt.shape)],
        out_specs=pl.BlockSpec((Q_TILE, 4 * LANES), lambda b, i: (b * nq + i, 0)),
        out_shape=jax.ShapeDtypeStruct((batch * seq, 4 * LANES), BF16),
        scratch_shapes=[pltpu.VMEM((n_cmp, LANES), BF16), pltpu.VMEM((LANES, n_cmp), BF16),
                        pltpu.VMEM((1, n_cols), F32), pltpu.VMEM((LANES + ONES_ROWS, n_cols), F32),
                        pltpu.VMEM((KEY_TILE, n_cols), F32), pltpu.VMEM((1, n_cols), F32)],
        compiler_params=_cparams("parallel", "arbitrary"),
        name="nsa_prompt",
    )(qt, gt, kcvc, kaug, vst, kw, vwt, mt)


def _head_sum(a, low):
    s0 = jnp.sum(jnp.where(low, a, 0.0), -1, keepdims=True)
    s1 = jnp.sum(jnp.where(low, 0.0, a), -1, keepdims=True)
    return jnp.where(low, s0, s1)


def _gated_group_norm(o, rg, norm_g, low):
    mu = _head_sum(o, low) * (1.0 / RET_DV)
    d = o - mu
    var = _head_sum(d * d, low) * (1.0 / RET_DV)
    gn = d * lax.rsqrt(var + LN_EPS) * norm_g
    return (rg * jax.nn.sigmoid(rg)) * gn


def _ret_kernel(rq_ref, rk_ref, rv_ref, rg_ref, decay_ref, qdec_ref, kdec_ref, gcl_ref, bd_ref, ng_ref,
                ret_o, st_o, st_s):
    c = pl.program_id(1)

    @pl.when(c == 0)
    def _():
        st_s[...] = jnp.zeros(st_s.shape, F32)

    low = lax.broadcasted_iota(jnp.int32, (1, LANES), 1) < RET_DK
    zero = jnp.zeros((), BF16)
    for p in range(N_PAIR):
        sl = slice(p * LANES, (p + 1) * LANES)
        q, k, v = rq_ref[:, sl], rk_ref[:, sl], rv_ref[:, sl]
        qb, kb, vb = q.astype(BF16), k.astype(BF16), v.astype(BF16)
        in0 = _dot_nt(jnp.where(low, qb, zero), kb) * decay_ref[2 * p]
        in1 = _dot_nt(jnp.where(low, zero, qb), kb) * decay_ref[2 * p + 1]
        inner = jnp.concatenate([in0.astype(BF16), in1.astype(BF16)], axis=1)
        v_split = jnp.concatenate([jnp.where(low, vb, zero), jnp.where(low, zero, vb)], axis=0)
        state = st_s[p]
        o = _dot(inner, v_split) + _dot((q * qdec_ref[:, sl]).astype(BF16), state.astype(BF16))
        kd_t = jnp.transpose(k * kdec_ref[:, sl]).astype(BF16)
        st_s[p] = state * gcl_ref[:, sl] + _dot(kd_t, vb) * bd_ref[...]
        ret_o[:, sl] = _gated_group_norm(o, rg_ref[:, sl], ng_ref[:, sl], low).astype(BF16)

    @pl.when(c == pl.num_programs(1) - 1)
    def _():
        st_o[...] = st_s[...]


def _retention_tables(log_gamma, chunk):
    i = jnp.arange(chunk, dtype=F32)
    diff = i[:, None] - i[None, :]
    decay = jnp.where(diff >= 0, jnp.exp(log_gamma[:, None, None] * jnp.maximum(diff, 0.0)), 0.0)
    per_lane = lambda t: jnp.repeat(t, RET_DK, axis=-1)
    qdec = per_lane(jnp.exp((i + 1.0)[:, None] * log_gamma[None, :]))
    kdec = per_lane(jnp.exp((chunk - 1.0 - i)[:, None] * log_gamma[None, :]))
    gcl = per_lane(jnp.exp(chunk * log_gamma)[None, :])
    r = jnp.arange(LANES)
    bd = ((r[:, None] // RET_DK) == (r[None, :] // RET_DV)).astype(F32)
    return decay, qdec, kdec, gcl, bd


def _retention_prompt(rq, rk, rv, rg, tables, norm_g, *, batch, seq):
    decay, qdec, kdec, gcl, bd = tables
    nc = seq // RET_CHUNK
    row_spec = pl.BlockSpec((RET_CHUNK, RET_WIDTH), lambda b, c: (b * nc + c, 0))
    return pl.pallas_call(
        _ret_kernel,
        grid=(batch, nc),
        in_specs=[row_spec] * 4 + [_const_spec(decay.shape), _const_spec(qdec.shape), _const_spec(kdec.shape),
                                   _const_spec(gcl.shape), _const_spec(bd.shape), _const_spec(norm_g.shape)],
        out_specs=[row_spec, pl.BlockSpec((None, N_PAIR, LANES, LANES), lambda b, c: (b, 0, 0, 0))],
        out_shape=[jax.ShapeDtypeStruct((batch * seq, RET_WIDTH), BF16),
                   jax.ShapeDtypeStruct((batch, N_PAIR, LANES, LANES), F32)],
        scratch_shapes=[pltpu.VMEM((N_PAIR, LANES, LANES), F32)],
        compiler_params=_cparams("parallel", "arbitrary"),
        name="retention",
    )(rq, rk, rv, rg, decay, qdec, kdec, gcl, bd, norm_g)


def _layer_norm(v, g, b):
    mu = jnp.mean(v, -1, keepdims=True)
    d = v - mu
    var = jnp.mean(d * d, -1, keepdims=True)
    return d * lax.rsqrt(var + LN_EPS) * g + b


def _tail_kernel(x_ref, nsa_ref, ret_ref, wo_ref, g1_ref, b1_ref, wu_ref, wd_ref, g2_ref, b2_ref, y_ref, *, alpha):
    mix = _dot(nsa_ref[...], wo_ref[0:NSA_WIDTH, :]) + _dot(ret_ref[...], wo_ref[NSA_WIDTH:, :])
    x1 = _layer_norm(alpha * x_ref[...] + mix, g1_ref[...], b1_ref[...])
    x1b = x1.astype(BF16)
    f = jnp.zeros(x1.shape, F32)
    for j in range(D_FF // D_MODEL):
        sl = slice(j * D_MODEL, (j + 1) * D_MODEL)
        h = jnp.maximum(_dot(x1b, wu_ref[:, sl]), 0.0)
        f = f + _dot((h * h).astype(BF16), wd_ref[sl, :])
    y_ref[...] = _layer_norm(alpha * x1 + f, g2_ref[...], b2_ref[...])


def _tail(x2d, nsa, ret, wo, g1, b1, wu, wd, g2, b2, *, tm, alpha):
    rows = x2d.shape[0]
    row_spec = lambda w: pl.BlockSpec((tm, w), lambda i: (i, 0))
    consts = (wo, g1, b1, wu, wd, g2, b2)
    return pl.pallas_call(
        functools.partial(_tail_kernel, alpha=alpha),
        grid=(rows // tm,),
        in_specs=[row_spec(D_MODEL), row_spec(NSA_WIDTH), row_spec(RET_WIDTH)] + [_const_spec(c.shape) for c in consts],
        out_specs=row_spec(D_MODEL),
        out_shape=jax.ShapeDtypeStruct((rows, D_MODEL), F32),
        compiler_params=_cparams("parallel"),
        name="tail",
    )(x2d, nsa, ret, *consts)


DEC_ROWS = 16
N_DEC_SEL = TOP_N - 1


def _dec_stack_q(q, low):
    rows = lax.broadcasted_iota(jnp.int32, (DEC_ROWS, 1), 0)
    qs = jnp.zeros((DEC_ROWS, LANES), F32)
    for g in range(N_KV):
        for c in range(HPG):
            piece = jnp.where(low if g == 0 else ~low, q[:, c * LANES:(c + 1) * LANES], 0.0)
            qs = jnp.where(rows == g * HPG + c, piece, qs)
    return qs, rows


def _dec_score_kernel(q_ref, kcvc_ref, m_ref, oc_o, idx_o, *, past):
    lane = lax.broadcasted_iota(jnp.int32, (1, LANES), 1)
    low = lane < HEAD_DIM
    qs, rows = _dec_stack_q(q_ref[...], low)
    qsb = qs.astype(BF16)
    kcvc = kcvc_ref[...]
    kc = kcvc[:, 0:LANES].astype(BF16)
    vc = kcvc[:, LANES:2 * LANES].astype(BF16)
    n_cmp = kcvc.shape[0]
    cend = lax.broadcasted_iota(jnp.int32, (1, n_cmp), 1) * CMP_STRIDE + (CMP_BLOCK - 1)
    p = _masked_softmax(_dot_nt(qsb, kc), jnp.broadcast_to(cend <= past, (DEC_ROWS, n_cmp)))
    oc_o[...] = _dot(p.astype(BF16), vc)

    psum = jnp.zeros((DEC_ROWS, n_cmp), F32)
    for g in range(N_KV):
        acc = p[g * HPG:g * HPG + 1]
        for c in range(1, HPG):
            acc = acc + p[g * HPG + c:g * HPG + c + 1]
        psum = jnp.where(rows == g, acc, psum)
    m = m_ref[...]
    imp = sum(_dot(part, m) for part in _split3(psum))
    lane_f = lane.astype(F32)
    forced = (lane == 0) | (lane == N_SEL_BLOCKS - 1)
    score = imp + FORCE_BONUS * forced.astype(F32)
    picked = jnp.zeros((DEC_ROWS, LANES), F32)
    for r in range(N_DEC_SEL):
        best = jnp.max(score, -1, keepdims=True)
        first = jnp.min(jnp.where(score == best, lane_f, float(LANES)), -1, keepdims=True)
        picked = jnp.where(lane == r, first, picked)
        score = jnp.where(lane_f == first, BELOW_NEG, score)
    idx_o[...] = picked.astype(jnp.int32)


def _dec_score(q3, kcvc, m, *, past):
    nb = q3.shape[0]
    n_cmp = kcvc.shape[1]
    return pl.pallas_call(
        functools.partial(_dec_score_kernel, past=past),
        grid=(nb,),
        in_specs=[pl.BlockSpec((None, 1, 4 * LANES), lambda b: (b, 0, 0)),
                  pl.BlockSpec((None, n_cmp, 2 * LANES), lambda b: (b, 0, 0)),
                  _const_spec(m.shape)],
        out_specs=[pl.BlockSpec((None, DEC_ROWS, LANES), lambda b: (b, 0, 0))] * 2,
        out_shape=[jax.ShapeDtypeStruct((nb, DEC_ROWS, LANES), F32), jax.ShapeDtypeStruct((nb, DEC_ROWS, LANES), jnp.int32)],
        compiler_params=_cparams("parallel"),
        name="dec_score",
    )(q3, kcvc, m)


def _round_bf16(x):
    return x.astype(BF16).astype(F32)


def _dec_attn_kernel(pt_ref, idx_ref, q_ref, g_ref, oc_ref, kvn_ref, wn_ref, cache_ref, cw_ref,
                     rq_ref, rk_ref, rv_ref, rg_ref, st_ref, gl_ref, ng_ref,
                     nsa_o, ret_o, win_o, st_o, kbuf, sem):
    b = pl.program_id(0)
    nb = pl.num_programs(0)
    slot = b % 2

    def blk_copy(bb, sl, g, k):
        j = idx_ref[bb, g, k]
        off = pl.multiple_of((j & 1) * SEL_BLOCK, SEL_BLOCK)
        return pltpu.make_async_copy(
            cache_ref.at[pt_ref[bb, j >> 1], pl.ds(off, SEL_BLOCK), pl.ds(2 * LANES, 2 * LANES)],
            kbuf.at[sl, g, pl.ds(k * SEL_BLOCK, SEL_BLOCK), :],
            sem.at[sl])

    def start_all(bb, sl):
        for g in range(N_KV):
            for k in range(N_DEC_SEL):
                blk_copy(bb, sl, g, k).start()

    @pl.when(b == 0)
    def _():
        start_all(0, 0)

    @pl.when(b + 1 < nb)
    def _():
        start_all(b + 1, 1 - slot)

    lane = lax.broadcasted_iota(jnp.int32, (1, LANES), 1)
    low = lane < HEAD_DIM
    qs, rows = _dec_stack_q(q_ref[...], low)
    qsb = qs.astype(BF16)
    g0_rows = rows < HPG

    cw = cw_ref[...]
    wb = cw.shape[0]
    wn = wn_ref[...]
    col = lax.broadcasted_iota(jnp.int32, (1, wb), 1)
    valid = jnp.broadcast_to((wb - col) < WINDOW, (DEC_ROWS, wb))
    s_w = jnp.where(valid, _dot_nt(qsb, cw[:, 0:LANES].astype(BF16)), NEG)
    s_wn = jnp.sum(qs * _round_bf16(wn[:, 0:LANES]), -1, keepdims=True)
    m_w = jnp.maximum(jnp.max(s_w, -1, keepdims=True), s_wn)
    p_w = jnp.exp(s_w - m_w) * valid.astype(F32)
    p_wn = jnp.exp(s_wn - m_w)
    l_w = jnp.maximum(jnp.sum(p_w, -1, keepdims=True) + p_wn, 1e-30)
    o_w = _dot((p_w / l_w).astype(BF16), cw[:, LANES:2 * LANES].astype(BF16)) + _round_bf16(p_wn / l_w) * _round_bf16(wn[:, LANES:2 * LANES])
    shifted = pltpu.roll(cw, wb - 1, 0)
    win_o[...] = jnp.where(lax.broadcasted_iota(jnp.int32, (wb, 1), 0) == wb - 1, wn, shifted)

    for g in range(N_KV):
        for k in range(N_DEC_SEL):
            blk_copy(b, slot, g, k).wait()
    kvn = kvn_ref[...]
    s_g = [_dot_nt(qsb, kbuf[slot, g, :, 0:LANES].astype(BF16)) for g in range(N_KV)]
    s_s = jnp.where(g0_rows, s_g[0], s_g[1])
    s_sn = jnp.sum(qs * _round_bf16(kvn[:, 2 * LANES:3 * LANES]), -1, keepdims=True)
    m_sel = jnp.maximum(jnp.max(s_s, -1, keepdims=True), s_sn)
    p_s = jnp.exp(s_s - m_sel)
    p_sn = jnp.exp(s_sn - m_sel)
    l_sel = jnp.maximum(jnp.sum(p_s, -1, keepdims=True) + p_sn, 1e-30)
    pn = (p_s / l_sel).astype(BF16)
    o_g = [_dot(pn, kbuf[slot, g, :, LANES:2 * LANES].astype(BF16)) for g in range(N_KV)]
    o_s = jnp.where(g0_rows, o_g[0], o_g[1]) + _round_bf16(p_sn / l_sel) * _round_bf16(kvn[:, 3 * LANES:4 * LANES])

    gates = g_ref[...]
    gcol = [jnp.zeros((DEC_ROWS, 1), F32) for _ in range(3)]
    for h in range(N_NSA_HEADS):
        for k in range(3):
            gcol[k] = jnp.where(rows == h, gates[:, 3 * h + k:3 * h + k + 1], gcol[k])
    o = gcol[0] * oc_ref[...] + gcol[1] * o_s + gcol[2] * o_w
    for c in range(HPG):
        nsa_o[:, c * LANES:(c + 1) * LANES] = jnp.where(low, o[c:c + 1], o[HPG + c:HPG + c + 1])

    for p in range(N_PAIR):
        sl = slice(p * LANES, (p + 1) * LANES)
        q, k, v, gl = rq_ref[:, sl], rk_ref[:, sl], rv_ref[:, sl], gl_ref[:, sl]
        state = st_ref[p]
        inner = _head_sum(q * k, low)

        def per_dk(row):
            t = jnp.transpose(jnp.broadcast_to(row, (LANES, LANES)))
            return jnp.where(low, t[0:RET_DK], t[RET_DK:2 * RET_DK])

        o_r = inner * v + jnp.sum(per_dk(q * gl) * state, axis=0, keepdims=True)
        st_o[p] = gl * state + per_dk(k) * v
        ret_o[:, sl] = _gated_group_norm(o_r, rg_ref[:, sl], ng_ref[:, sl], low)


def _dec_attn(page_table, idx, q3, gates3, oc, kvn3, wn3, cache4, cwin, rq3, rk3, rv3, rg3, state, gl, ng):
    nb = q3.shape[0]
    wb = cwin.shape[1]
    row3 = lambda w: pl.BlockSpec((None, 1, w), lambda b, *_: (b, 0, 0))
    st_spec = pl.BlockSpec((None, N_PAIR, RET_DK, LANES), lambda b, *_: (b, 0, 0, 0))
    win_spec = pl.BlockSpec((None, wb, 2 * LANES), lambda b, *_: (b, 0, 0))
    grid_spec = pltpu.PrefetchScalarGridSpec(
        num_scalar_prefetch=2,
        grid=(nb,),
        in_specs=[row3(4 * LANES), row3(LANES), pl.BlockSpec((None, DEC_ROWS, LANES), lambda b, *_: (b, 0, 0)),
                  row3(4 * LANES), row3(2 * LANES), pl.BlockSpec(memory_space=pl.ANY), win_spec,
                  row3(4 * LANES), row3(4 * LANES), row3(4 * LANES), row3(4 * LANES), st_spec,
                  pl.BlockSpec(gl.shape, lambda b, *_: (0, 0)), pl.BlockSpec(ng.shape, lambda b, *_: (0, 0))],
        out_specs=[row3(4 * LANES), row3(4 * LANES), win_spec, st_spec],
        scratch_shapes=[pltpu.VMEM((2, N_KV, N_DEC_SEL * SEL_BLOCK, 2 * LANES), F32), pltpu.SemaphoreType.DMA((2,))],
    )
    return pl.pallas_call(
        _dec_attn_kernel,
        grid_spec=grid_spec,
        out_shape=[jax.ShapeDtypeStruct((nb, 1, 4 * LANES), F32), jax.ShapeDtypeStruct((nb, 1, 4 * LANES), F32),
                   jax.ShapeDtypeStruct((nb, wb, 2 * LANES), F32), jax.ShapeDtypeStruct(state.shape, F32)],
        compiler_params=_cparams("arbitrary"),
        name="dec_attn",
    )(page_table, idx, q3, gates3, oc, kvn3, wn3, cache4, cwin, rq3, rk3, rv3, rg3, state, gl, ng)


def _permute_w_in(w):
    p0 = NSA_WIDTH
    p1 = p0 + 6 * KV_WIDTH
    p2 = p1 + 3 * N_NSA_HEADS
    wq = w[:, :p0].reshape(D_MODEL, N_NSA_HEADS, HEAD_DIM)[:, jnp.array(HEAD_PERM)].reshape(D_MODEL, p0)
    w_gates = jnp.pad(w[:, p1:p2], ((0, 0), (0, LANES - 3 * N_NSA_HEADS)))
    return jnp.concatenate([wq, w[:, p0:p1], w[:, p2:], w_gates], axis=1).astype(BF16)


def _permute_w_o(w):
    w_nsa = w[:NSA_WIDTH].reshape(N_NSA_HEADS, HEAD_DIM, D_MODEL)[jnp.array(HEAD_PERM)].reshape(NSA_WIDTH, D_MODEL)
    return jnp.concatenate([w_nsa, w[NSA_WIDTH:]], axis=0).astype(BF16)


def kernel(x_prompt, x_sample, cache_kv, cache_win, state_ret, page_table, w_in, w_cmp_k, w_cmp_v, pos_cmp_k, pos_cmp_v, ret_norm_g, w_o, ln1_g, ln1_b, w_up, w_down, ln2_g, ln2_b):
    depth = w_in.shape[0]
    assert depth == 1 and x_sample.shape[1] == 1
    batch, seq, _ = x_prompt.shape
    nb = x_sample.shape[0]
    n_pages = page_table.shape[1]
    past = n_pages * PAGE_SIZE
    assert past // SEL_BLOCK == N_SEL_BLOCKS and cache_win.shape[2] == WINDOW
    alpha = (2 * depth) ** 0.25
    log_gamma = jnp.log1p(-jnp.exp2(-5.0 - jnp.arange(N_RET_HEADS, dtype=F32)))

    w_perm = _permute_w_in(w_in[0])
    wo = _permute_w_o(w_o[0])
    wu, wd = w_up[0].astype(BF16), w_down[0].astype(BF16)
    vec = lambda a: a[0].reshape(1, -1)
    cmp_w = _compress_weights(w_cmp_k[0], w_cmp_v[0], pos_cmp_k[0], pos_cmp_v[0])
    tail_consts = (wo, vec(ln1_g), vec(ln1_b), wu, wd, vec(ln2_g), vec(ln2_b))
    norm_g = vec(ret_norm_g)

    tm = 512
    pos_p = jnp.arange(seq, dtype=jnp.int32)
    tabs_p = _rope_tables(pos_p, ROPE_THETA, ROPE_DIMS) + _rope_tables(pos_p, RET_THETA, RET_DK)
    x2d = x_prompt.reshape(batch * seq, D_MODEL)
    kv4, win, rq, rk, rv, rg, qt, gt, kaug, vst, kw, vwt = _project(x2d, w_perm, tabs_p, tm=tm, seq=seq, prompt=True)
    seq_pages = seq // PAGE_SIZE
    pt_p = jnp.arange(batch * seq_pages, dtype=jnp.int32).reshape(batch, seq_pages)
    kcvc_p = _compress(pt_p, kv4.reshape(batch * seq_pages, PAGE_SIZE, 4 * KV_WIDTH), *cmp_w)
    o_nsa = _nsa_prompt(qt, gt, kcvc_p, kaug, vst, kw, vwt, _importance_matrix(seq // CMP_STRIDE).T.astype(BF16), batch=batch, seq=seq)
    ret, st_p = _retention_prompt(rq, rk, rv, rg, _retention_tables(log_gamma, RET_CHUNK), norm_g, batch=batch, seq=seq)
    y_p = _tail(x2d, o_nsa, ret, *tail_consts, tm=tm, alpha=alpha)

    y_prompt = y_p.reshape(batch, seq, D_MODEL)
    kv_prompt = kv4.reshape(1, batch, seq, 4, N_KV, HEAD_DIM)
    wlen = min(WINDOW, seq)
    win_prompt = win.reshape(batch, seq, 2, N_KV, HEAD_DIM)[None, :, seq - wlen:]
    st6 = st_p.reshape(batch, N_PAIR, 2, RET_DK, 2, RET_DV)
    ret_prompt = jnp.stack([st6[:, :, 0, :, 0, :], st6[:, :, 1, :, 1, :]], axis=2).reshape(1, batch, N_RET_HEADS, RET_DK, RET_DV)

    pos_s = jnp.full((nb,), past, dtype=jnp.int32)
    tabs_s = _rope_tables(pos_s, ROPE_THETA, ROPE_DIMS) + _rope_tables(pos_s, RET_THETA, RET_DK)
    xs2d = x_sample.reshape(nb, D_MODEL)
    kv4_d, win_d, rq_d, rk_d, rv_d, rg_d, q_d, gates_d = _project(xs2d, w_perm, tabs_s, tm=nb, seq=seq, prompt=False)
    cache4 = cache_kv[0].reshape(cache_kv.shape[1], PAGE_SIZE, 4 * KV_WIDTH)
    kcvc_d = _compress(page_table, cache4, *cmp_w)
    row3 = lambda a: a.reshape(nb, 1, a.shape[-1])
    q3 = row3(q_d)
    oc_d, idx_d = _dec_score(q3, kcvc_d, _importance_matrix(past // CMP_STRIDE).astype(BF16), past=past)
    idx = idx_d[:, :N_KV, :TOP_N]
    state = state_ret[0].astype(F32).reshape(nb, N_PAIR, 2, RET_DK, RET_DV).transpose(0, 1, 3, 2, 4).reshape(nb, N_PAIR, RET_DK, LANES)
    gl = jnp.repeat(jnp.exp(log_gamma), RET_DK)[None, :]
    cwin = cache_win[0].reshape(nb, WINDOW, 2 * KV_WIDTH)
    nsa_d, ret_d, win_s, st_d = _dec_attn(page_table, idx, q3, row3(gates_d), oc_d, row3(kv4_d), row3(win_d), cache4, cwin,
                                          row3(rq_d), row3(rk_d), row3(rv_d), row3(rg_d), state, gl, norm_g)
    y_s = _tail(xs2d, nsa_d.reshape(nb, NSA_WIDTH).astype(BF16), ret_d.reshape(nb, RET_WIDTH).astype(BF16), *tail_consts, tm=nb, alpha=alpha)

    y_sample = y_s.reshape(nb, 1, D_MODEL)
    kv_sample = kv4_d.reshape(1, nb, 1, 4, N_KV, HEAD_DIM)
    win_sample = win_s.reshape(1, nb, WINDOW, 2, N_KV, HEAD_DIM)
    ret_sample = st_d.reshape(nb, N_PAIR, RET_DK, 2, RET_DV).transpose(0, 1, 3, 2, 4).reshape(1, nb, N_RET_HEADS, RET_DK, RET_DV)
    return (y_prompt, y_sample, kv_prompt, kv_sample, win_prompt, win_sample, ret_prompt, ret_sample)
```

```python
import functools

import jax
import jax.numpy as jnp
from jax import lax
from jax.experimental import pallas as pl
from jax.experimental.pallas import tpu as pltpu

F32 = jnp.float32
BF16 = jnp.bfloat16

D_MODEL = 1024
HEAD_DIM = 64
N_NSA_HEADS = 8
N_KV = 2
HPG = N_NSA_HEADS // N_KV
N_RET_HEADS = 8
RET_DK = 64
RET_DV = 64
NSA_WIDTH = N_NSA_HEADS * HEAD_DIM
RET_WIDTH = N_RET_HEADS * RET_DV
KV_WIDTH = N_KV * HEAD_DIM
D_FF = 4 * D_MODEL
CMP_STRIDE = 16
CMP_BLOCK = 32
SEL_BLOCK = 64
TOP_N = 16
WINDOW = 512
RET_CHUNK = 128
PAGE_SIZE = 128
ROPE_THETA = 500000.0
ROPE_DIMS = HEAD_DIM // 4
RET_THETA = 10000.0
LN_EPS = 1e-5
FORCE_BONUS = 1e4
NEG = -1e30
BELOW_NEG = -3e38

LANES = 128
SUBLANES = 8
VMEM_LIMIT = 56 * 1024 * 1024

LOG2E = 1.4426950408889634
BF16_SUBLANES = 16
ONES_ROWS = BF16_SUBLANES
Q_TILE = 128
KEY_TILE = 512
N_PAIR = N_RET_HEADS // 2
N_SEL_BLOCKS = 128

U_Q, U_KV, U_RQ, U_RK, U_RV, U_RG, U_G, U_TOTAL = 0, 4, 10, 14, 18, 22, 26, 27
HEAD_PERM = (0, 4, 1, 5, 2, 6, 3, 7)


def _dot(a, b):
    return jnp.dot(a, b, preferred_element_type=F32)


def _dot_nt(a, b):
    return lax.dot_general(a, b, (((1,), (1,)), ((), ())), preferred_element_type=F32)


def _cparams(*sem):
    return pltpu.CompilerParams(dimension_semantics=sem, vmem_limit_bytes=VMEM_LIMIT)


def _const_spec(shape):
    nd = len(shape)
    return pl.BlockSpec(shape, lambda *_: (0,) * nd)


def _rope(u, c, s_fwd, s_bwd, half):
    return u * c + pltpu.roll(u, LANES - half, 1) * s_fwd + pltpu.roll(u, half, 1) * s_bwd


def _proj_kernel(x_ref, w_ref, cq_ref, sqf_ref, sqb_ref, cr_ref, srf_ref, srb_ref, *outs, tm, seq, prompt):
    if prompt:
        kv4t_o, wint_o, rq_o, rk_o, rv_o, rg_o, qt_o, gt_o, kaug_o, vst_o, kw_o, vwt_o = outs
    else:
        kv4_o, win_o, rq_o, rk_o, rv_o, rg_o, q_o, g_o = outs
    xb = x_ref[...].astype(BF16)

    def seg(u0, n):
        return _dot(xb, w_ref[:, u0 * LANES:(u0 + n) * LANES])

    def unit(z, u):
        return z[:, u * LANES:(u + 1) * LANES]

    cq, sqf, sqb = cq_ref[...], sqf_ref[...], sqb_ref[...]
    cr, srf, srb = cr_ref[...], srf_ref[...], srb_ref[...]
    nsa_half, ret_half = ROPE_DIMS // 2, RET_DK // 2

    zq = seg(U_Q, 4)
    q = jnp.concatenate([_rope(unit(zq, c), cq, sqf, sqb, nsa_half) for c in range(4)], axis=1) * (HEAD_DIM ** -0.5)
    gates = jax.nn.sigmoid(seg(U_G, 1))
    zkv = seg(U_KV, 6)
    k_slc = _rope(unit(zkv, 2), cq, sqf, sqb, nsa_half)
    k_win = _rope(unit(zkv, 4), cq, sqf, sqb, nsa_half)
    kv4 = jnp.concatenate([zkv[:, 0:2 * LANES], k_slc, unit(zkv, 3)], axis=1)
    win = jnp.concatenate([k_win, unit(zkv, 5)], axis=1)
    if prompt:
        kv4t = jnp.transpose(kv4)
        wint = jnp.transpose(win)
        kv4t_o[...] = kv4t
        wint_o[...] = wint
        qt_o[...] = jnp.transpose(q * LOG2E).astype(BF16)
        gt_o[...] = jnp.transpose(gates)
        row = pl.program_id(0) * tm + lax.broadcasted_iota(jnp.int32, (tm, 1), 0)
        blk = (row & (seq - 1)) >> 6
        onehot = lax.broadcasted_iota(jnp.int32, (tm, LANES), 1) == blk
        kaug_o[:, 0:LANES] = k_slc.astype(BF16)
        kaug_o[:, LANES:2 * LANES] = jnp.where(onehot, 1.0, 0.0).astype(BF16)
        vst_o[0:LANES, :] = kv4t[3 * LANES:4 * LANES].astype(BF16)
        vst_o[LANES:, :] = jnp.ones((ONES_ROWS, tm), BF16)
        kw_o[...] = k_win.astype(BF16)
        vwt = wint[LANES:2 * LANES].astype(BF16)
        for j in range(tm // LANES):
            vwt_o[j] = vwt[:, j * LANES:(j + 1) * LANES]
    else:
        kv4_o[...] = kv4
        win_o[...] = win
        q_o[...] = q.astype(BF16).astype(F32)
        g_o[...] = gates

    zrq = seg(U_RQ, 4)
    zrk = seg(U_RK, 4)
    for c in range(4):
        sl = slice(c * LANES, (c + 1) * LANES)
        rq_o[:, sl] = _rope(unit(zrq, c), cr, srf, srb, ret_half)
        rk_o[:, sl] = _rope(unit(zrk, c), cr, srf, srb, ret_half) * (RET_DK ** -0.5)
    rv_o[...] = seg(U_RV, 4)
    rg_o[...] = seg(U_RG, 4)


def _rope_tables(pos, theta, n_rot):
    half = n_rot // 2
    inv = theta ** (-jnp.arange(half, dtype=F32) / half)
    ang = pos.astype(F32)[:, None] * inv[None, :]
    cos, sin = jnp.cos(ang), jnp.sin(ang)
    n = pos.shape[0]
    rest = HEAD_DIM - n_rot
    c = jnp.concatenate([cos, cos, jnp.ones((n, rest), F32)], 1)
    s_fwd = jnp.concatenate([-sin, jnp.zeros((n, half + rest), F32)], 1)
    s_bwd = jnp.concatenate([jnp.zeros((n, half), F32), sin, jnp.zeros((n, rest), F32)], 1)
    rep = LANES // HEAD_DIM
    return tuple(jnp.tile(t, (1, rep)) for t in (c, s_fwd, s_bwd))


def _project(x2d, w_perm, tables, *, tm, seq, prompt):
    rows = x2d.shape[0]
    n_tab_tiles = tables[0].shape[0] // tm
    row_spec = lambda w: pl.BlockSpec((tm, w), lambda i: (i, 0))
    tab_spec = pl.BlockSpec((tm, LANES), lambda i: (i % n_tab_tiles, 0))
    if prompt:
        assert tm == KEY_TILE and seq % tm == 0
        batch, tpb = rows // seq, seq // tm
        feat_spec = lambda f: pl.BlockSpec((None, f, tm), lambda i: (i // tpb, 0, i % tpb))
        outs = [(feat_spec(4 * LANES), (batch, 4 * LANES, seq), F32), (feat_spec(2 * LANES), (batch, 2 * LANES, seq), F32)]
    else:
        outs = [(row_spec(4 * LANES), (rows, 4 * LANES), F32), (row_spec(2 * LANES), (rows, 2 * LANES), F32)]
    outs += [(row_spec(4 * LANES), (rows, 4 * LANES), F32)] * 4
    if prompt:
        outs += [(feat_spec(4 * LANES), (batch, 4 * LANES, seq), BF16),
                 (feat_spec(LANES), (batch, LANES, seq), F32),
                 (row_spec(2 * LANES), (rows, 2 * LANES), BF16),
                 (pl.BlockSpec((None, LANES + ONES_ROWS, tm), lambda i: (i, 0, 0)), (rows // tm, LANES + ONES_ROWS, tm), BF16),
                 (row_spec(LANES), (rows, LANES), BF16),
                 (pl.BlockSpec((tm // LANES, LANES, LANES), lambda i: (i, 0, 0)), (rows // LANES, LANES, LANES), BF16)]
    else:
        outs += [(row_spec(4 * LANES), (rows, 4 * LANES), F32), (row_spec(LANES), (rows, LANES), F32)]
    return pl.pallas_call(
        functools.partial(_proj_kernel, tm=tm, seq=seq, prompt=prompt),
        grid=(rows // tm,),
        in_specs=[row_spec(D_MODEL), _const_spec(w_perm.shape)] + [tab_spec] * 6,
        out_specs=[s for s, _, _ in outs],
        out_shape=[jax.ShapeDtypeStruct(shape, dt) for _, shape, dt in outs],
        compiler_params=_cparams("parallel"),
        name="proj_prompt" if prompt else "proj_decode",
    )(x2d, w_perm, *tables)


def _cmp_kernel(pn_ref, po_ref, src_ref, wlo_ref, whi_ref, plo_ref, phi_ref, out_ref, stage, slab, sem, *, n_pages):
    b = pl.program_id(0)
    nb = pl.num_programs(0)
    slot = b % 2

    def page_copy(bb, sl, p, kv):
        return pltpu.make_async_copy(
            src_ref.at[pn_ref[bb, p], pl.ds(kv * LANES, LANES), pl.ds(pl.multiple_of(po_ref[bb, p], PAGE_SIZE), PAGE_SIZE)],
            stage.at[sl, kv, p],
            sem.at[sl])

    def start_all(bb, sl):
        for p in range(n_pages):
            for kv in range(2):
                page_copy(bb, sl, p, kv).start()

    @pl.when(b == 0)
    def _():
        start_all(0, 0)

    @pl.when(b + 1 < nb)
    def _():
        start_all(b + 1, 1 - slot)

    for p in range(n_pages):
        for kv in range(2):
            page_copy(b, slot, p, kv).wait()

    def to_rows(p, carry):
        r0 = pl.multiple_of(p * PAGE_SIZE, PAGE_SIZE)
        for kv in range(2):
            slab[kv, pl.ds(r0, PAGE_SIZE), :] = jnp.transpose(stage[slot, kv, p])
        return carry

    lax.fori_loop(0, n_pages, to_rows, 0)

    n_chunks = n_pages * PAGE_SIZE // CMP_STRIDE
    acc_lo = jnp.zeros((n_chunks, 2 * LANES), F32)
    acc_hi = jnp.zeros((n_chunks, 2 * LANES), F32)
    for r in range(CMP_STRIDE):
        xr = jnp.concatenate([slab[kv, pl.ds(r, n_chunks, stride=CMP_STRIDE), :] for kv in range(2)], axis=1)
        acc_lo = acc_lo + _dot((xr + plo_ref[r:r + 1, :]).astype(BF16), wlo_ref[r])
        acc_hi = acc_hi + _dot((xr + phi_ref[r:r + 1, :]).astype(BF16), whi_ref[r])
    nxt = pltpu.roll(acc_hi, n_chunks - 1, 0)
    last = lax.broadcasted_iota(jnp.int32, (n_chunks, 1), 0) == n_chunks - 1
    out_ref[...] = jnp.where(last, 0.0, acc_lo + nxt)


def _compress(page_index, page_offset, src, wlo, whi, plo, phi):
    nb, n_pages = page_index.shape
    n_chunks = n_pages * PAGE_SIZE // CMP_STRIDE
    grid_spec = pltpu.PrefetchScalarGridSpec(
        num_scalar_prefetch=2,
        grid=(nb,),
        in_specs=[pl.BlockSpec(memory_space=pl.ANY),
                  pl.BlockSpec(wlo.shape, lambda b, *_: (0, 0, 0)),
                  pl.BlockSpec(whi.shape, lambda b, *_: (0, 0, 0)),
                  pl.BlockSpec(plo.shape, lambda b, *_: (0, 0)),
                  pl.BlockSpec(phi.shape, lambda b, *_: (0, 0))],
        out_specs=pl.BlockSpec((None, n_chunks, 2 * LANES), lambda b, *_: (b, 0, 0)),
        scratch_shapes=[pltpu.VMEM((2, 2, n_pages, LANES, PAGE_SIZE), F32),
                        pltpu.VMEM((2, n_pages * PAGE_SIZE, LANES), F32),
                        pltpu.SemaphoreType.DMA((2,))],
    )
    return pl.pallas_call(
        functools.partial(_cmp_kernel, n_pages=n_pages),
        grid_spec=grid_spec,
        out_shape=jax.ShapeDtypeStruct((nb, n_chunks, 2 * LANES), F32),
        compiler_params=_cparams("arbitrary"),
        name="compress",
    )(page_index, page_offset, src, wlo, whi, plo, phi)


def _compress_weights(w_k, w_v, pos_k, pos_v):
    def bd(wk_r, wv_r):
        z = jnp.zeros((HEAD_DIM, HEAD_DIM), F32)
        rows = [[wk_r, z, z, z], [z, wk_r, z, z], [z, z, wv_r, z], [z, z, z, wv_r]]
        return jnp.block(rows)

    w_all = jax.vmap(bd)(w_k, w_v).astype(BF16)
    p_all = jnp.concatenate([pos_k, pos_k, pos_v, pos_v], axis=1)
    return w_all[:CMP_STRIDE], w_all[CMP_STRIDE:], p_all[:CMP_STRIDE], p_all[CMP_STRIDE:]


def _masked_softmax(s, mask):
    s = jnp.where(mask, s, NEG)
    m = jnp.max(s, -1, keepdims=True)
    p = jnp.exp(s - m) * mask.astype(F32)
    return p / jnp.maximum(jnp.sum(p, -1, keepdims=True), 1e-30)


def _split3(x):
    hi = x.astype(BF16)
    r1 = x - hi.astype(F32)
    mid = r1.astype(BF16)
    lo = (r1 - mid.astype(F32)).astype(BF16)
    return hi, mid, lo


def _importance_matrix(n_cmp):
    n = jnp.arange(n_cmp)[:, None]
    j = jnp.arange(N_SEL_BLOCKS)[None, :]
    inner = (n >= 4 * j) & (n <= 4 * j + 2)
    edge = (n == 4 * j - 1) | (n == 4 * j + 3)
    return jnp.where(inner, 2.0, jnp.where(edge, 1.0, 0.0)).astype(F32)


def _softmax_over_keys(s, mask):
    s = jnp.where(mask, s, NEG)
    m = jnp.max(s, 0, keepdims=True)
    p = jnp.exp2(s - m)
    inv = jnp.where(m > 0.5 * NEG, 1.0 / jnp.maximum(jnp.sum(p, 0, keepdims=True), 1e-30), 0.0)
    return p * inv


def _nsa_kernel(qt_ref, gt_ref, kcvc_ref, kaug_ref, vst_ref, kw_ref, vwt_ref, mt_ref, out_ref,
                kc_s, vct_s, m_s, acc_s, s_buf, tmax_s):
    i = pl.program_id(1)
    t0 = i * Q_TILE
    n_cmp = kcvc_ref.shape[0]

    @pl.when(i == 0)
    def _():
        kcvc = kcvc_ref[...]
        kc_s[...] = kcvc[:, 0:LANES].astype(BF16)
        vct_s[...] = jnp.transpose(kcvc[:, LANES:2 * LANES]).astype(BF16)

    lane = lax.broadcasted_iota(jnp.int32, (1, Q_TILE), 1)
    tq = jnp.concatenate([t0 + lane] * N_NSA_HEADS, axis=1)
    low_rows = lax.broadcasted_iota(jnp.int32, (LANES, 1), 0) < HEAD_DIM
    qt = qt_ref[...]
    zero = jnp.zeros((), BF16)
    q_all = jnp.concatenate([jnp.where(low_rows if g == 0 else ~low_rows, qt[c * LANES:(c + 1) * LANES], zero)
                             for g in range(N_KV) for c in range(HPG)], axis=1)

    cend = lax.broadcasted_iota(jnp.int32, (n_cmp, 1), 0) * CMP_STRIDE + (CMP_BLOCK - 1)
    p_c = _softmax_over_keys(_dot(kc_s[...], q_all), cend <= tq)
    o_c = _dot(vct_s[...], p_c.astype(BF16))

    w0 = pl.multiple_of(jnp.maximum(t0 - WINDOW, 0), Q_TILE)
    n_win = WINDOW + Q_TILE
    dist = tq - (w0 + lax.broadcasted_iota(jnp.int32, (n_win, 1), 0))
    p_w = _softmax_over_keys(_dot(kw_ref[pl.ds(w0, n_win), :], q_all), (dist >= 0) & (dist < WINDOW))
    w_blk = w0 // LANES
    vwt = jnp.concatenate([vwt_ref[w_blk + j] for j in range(n_win // LANES)], axis=1)
    o_w = _dot(vwt, p_w.astype(BF16))
    gt = gt_ref[...]
    gk = [jnp.concatenate([gt[3 * h + k:3 * h + k + 1] for h in range(N_NSA_HEADS)], axis=1) for k in range(3)]
    o_cw = gk[0] * o_c + gk[2] * o_w

    psum = jnp.concatenate([sum(p_c[:, (g * HPG + c) * Q_TILE:(g * HPG + c + 1) * Q_TILE] for c in range(HPG))
                            for g in range(N_KV)], axis=1)
    mt = mt_ref[...]
    imp_t = sum(_dot(mt, part) for part in _split3(psum))
    blk = lax.broadcasted_iota(jnp.int32, (N_SEL_BLOCKS, 1), 0)
    blk_f = blk.astype(F32)
    cur = jnp.concatenate([(t0 + lane) >> 6] * N_KV, axis=1)
    forced = (blk == 0) | (blk == cur) | (blk == cur - 1)
    score = jnp.where(blk <= cur, imp_t + FORCE_BONUS * forced.astype(F32), NEG)
    bias = jnp.full(score.shape, NEG, F32)
    for _ in range(TOP_N):
        best = jnp.max(score, axis=0, keepdims=True)
        first = jnp.min(jnp.where(score == best, blk_f, float(N_SEL_BLOCKS)), axis=0, keepdims=True)
        pick = blk_f == first
        bias = jnp.where(pick, 0.0, bias)
        score = jnp.where(pick, BELOW_NEG, score)
    bias = bias.astype(BF16)
    bias_all = jnp.concatenate([bias[:, g * Q_TILE:(g + 1) * Q_TILE] for g in range(N_KV) for _ in range(HPG)], axis=1)
    q_aug = jnp.concatenate([q_all, bias_all], axis=0)

    m_s[...] = jnp.full(m_s.shape, NEG, F32)
    acc_s[...] = jnp.zeros(acc_s.shape, F32)

    def scores(kt, causal):
        k0 = pl.multiple_of(kt * KEY_TILE, KEY_TILE)
        s = _dot(kaug_ref[pl.ds(k0, KEY_TILE), :], q_aug)
        if causal:
            kpos = k0 + lax.broadcasted_iota(jnp.int32, (KEY_TILE, 1), 0)
            s = jnp.where(kpos <= tq, s, NEG)
        s_buf[...] = s
        tmax_s[...] = jnp.max(s, 0, keepdims=True)

    def accumulate(kt):
        m_old = m_s[...]
        m_new = jnp.maximum(m_old, tmax_s[...])
        alpha = jnp.exp2(m_old - m_new)
        p = jnp.exp2(s_buf[...] - m_new)
        acc_s[...] = alpha * acc_s[...] + _dot(vst_ref[kt], p.astype(BF16))
        m_s[...] = m_new

    last = (t0 + Q_TILE - 1) // KEY_TILE

    @pl.when(last == 0)
    def _():
        scores(0, True)

    @pl.when(last > 0)
    def _():
        scores(0, False)

        def body(kt, carry):
            accumulate(kt)
            scores(kt + 1, False)
            return carry

        lax.fori_loop(0, last - 1, body, 0)
        accumulate(last - 1)
        scores(last, True)

    accumulate(last)
    o_s = acc_s[0:LANES, :] * (1.0 / acc_s[LANES:LANES + 1, :])

    o = o_cw + gk[1] * o_s
    half = HPG * Q_TILE
    comb = jnp.where(low_rows, o[:, 0:half], o[:, half:2 * half])
    for c in range(HPG):
        out_ref[:, c * LANES:(c + 1) * LANES] = jnp.transpose(comb[:, c * Q_TILE:(c + 1) * Q_TILE]).astype(BF16)


def _nsa_prompt(qt, gt, kcvc, kaug, vst, kw, vwt, mt, *, batch, seq):
    nq = seq // Q_TILE
    n_cols = N_NSA_HEADS * Q_TILE
    n_cmp = kcvc.shape[1]
    col_spec = lambda f: pl.BlockSpec((None, f, Q_TILE), lambda b, i: (b, 0, i))
    seq_spec = lambda w: pl.BlockSpec((seq, w), lambda b, i: (b, 0))
    return pl.pallas_call(
        _nsa_kernel,
        grid=(batch, nq),
        in_specs=[col_spec(4 * LANES), col_spec(LANES),
                  pl.BlockSpec((None, n_cmp, 2 * LANES), lambda b, i: (b, 0, 0)),
                  seq_spec(2 * LANES),
                  pl.BlockSpec((seq // KEY_TILE, LANES + ONES_ROWS, KEY_TILE), lambda b, i: (b, 0, 0)),
                  seq_spec(LANES),
                  pl.BlockSpec((seq // LANES, LANES, LANES), lambda b, i: (b, 0, 0)),
                  _const_spec(mt.shape)],
        out_specs=pl.BlockSpec((Q_TILE, 4 * LANES), lambda b, i: (b * nq + i, 0)),
        out_shape=jax.ShapeDtypeStruct((batch * seq, 4 * LANES), BF16),
        scratch_shapes=[pltpu.VMEM((n_cmp, LANES), BF16), pltpu.VMEM((LANES, n_cmp), BF16),
                        pltpu.VMEM((1, n_cols), F32), pltpu.VMEM((LANES + ONES_ROWS, n_cols), F32),
                        pltpu.VMEM((KEY_TILE, n_cols), F32), pltpu.VMEM((1, n_cols), F32)],
        compiler_params=_cparams("parallel", "arbitrary"),
        name="nsa_prompt",
    )(qt, gt, kcvc, kaug, vst, kw, vwt, mt)


def _head_sum(a, low):
    s0 = jnp.sum(jnp.where(low, a, 0.0), -1, keepdims=True)
    s1 = jnp.sum(jnp.where(low, 0.0, a), -1, keepdims=True)
    return jnp.where(low, s0, s1)


def _gated_group_norm(o, rg, norm_g, low):
    mu = _head_sum(o, low) * (1.0 / RET_DV)
    d = o - mu
    var = _head_sum(d * d, low) * (1.0 / RET_DV)
    gn = d * lax.rsqrt(var + LN_EPS) * norm_g
    return (rg * jax.nn.sigmoid(rg)) * gn


def _ret_kernel(rq_ref, rk_ref, rv_ref, rg_ref, decay_ref, qdec_ref, kdec_ref, gcl_ref, bd_ref, ng_ref,
                ret_o, st_o, st_s):
    c = pl.program_id(1)

    @pl.when(c == 0)
    def _():
        st_s[...] = jnp.zeros(st_s.shape, F32)

    low = lax.broadcasted_iota(jnp.int32, (1, LANES), 1) < RET_DK
    zero = jnp.zeros((), BF16)
    for p in range(N_PAIR):
        sl = slice(p * LANES, (p + 1) * LANES)
        q, k, v = rq_ref[:, sl], rk_ref[:, sl], rv_ref[:, sl]
        qb, kb, vb = q.astype(BF16), k.astype(BF16), v.astype(BF16)
        in0 = _dot_nt(jnp.where(low, qb, zero), kb) * decay_ref[2 * p]
        in1 = _dot_nt(jnp.where(low, zero, qb), kb) * decay_ref[2 * p + 1]
        inner = jnp.concatenate([in0.astype(BF16), in1.astype(BF16)], axis=1)
        v_split = jnp.concatenate([jnp.where(low, vb, zero), jnp.where(low, zero, vb)], axis=0)
        state = st_s[p]
        o = _dot(inner, v_split) + _dot((q * qdec_ref[:, sl]).astype(BF16), state.astype(BF16))
        kd_t = jnp.transpose(k * kdec_ref[:, sl]).astype(BF16)
        st_s[p] = state * gcl_ref[:, sl] + _dot(kd_t, vb) * bd_ref[...]
        ret_o[:, sl] = _gated_group_norm(o, rg_ref[:, sl], ng_ref[:, sl], low).astype(BF16)

    @pl.when(c == pl.num_programs(1) - 1)
    def _():
        st_o[...] = st_s[...]


def _retention_tables(log_gamma, chunk):
    i = jnp.arange(chunk, dtype=F32)
    diff = i[:, None] - i[None, :]
    decay = jnp.where(diff >= 0, jnp.exp(log_gamma[:, None, None] * jnp.maximum(diff, 0.0)), 0.0)
    per_lane = lambda t: jnp.repeat(t, RET_DK, axis=-1)
    qdec = per_lane(jnp.exp((i + 1.0)[:, None] * log_gamma[None, :]))
    kdec = per_lane(jnp.exp((chunk - 1.0 - i)[:, None] * log_gamma[None, :]))
    gcl = per_lane(jnp.exp(chunk * log_gamma)[None, :])
    r = jnp.arange(LANES)
    bd = ((r[:, None] // RET_DK) == (r[None, :] // RET_DV)).astype(F32)
    return decay, qdec, kdec, gcl, bd


def _retention_prompt(rq, rk, rv, rg, tables, norm_g, *, batch, seq):
    decay, qdec, kdec, gcl, bd = tables
    nc = seq // RET_CHUNK
    row_spec = pl.BlockSpec((RET_CHUNK, RET_WIDTH), lambda b, c: (b * nc + c, 0))
    return pl.pallas_call(
        _ret_kernel,
        grid=(batch, nc),
        in_specs=[row_spec] * 4 + [_const_spec(decay.shape), _const_spec(qdec.shape), _const_spec(kdec.shape),
                                   _const_spec(gcl.shape), _const_spec(bd.shape), _const_spec(norm_g.shape)],
        out_specs=[row_spec, pl.BlockSpec((None, N_PAIR, LANES, LANES), lambda b, c: (b, 0, 0, 0))],
        out_shape=[jax.ShapeDtypeStruct((batch * seq, RET_WIDTH), BF16),
                   jax.ShapeDtypeStruct((batch, N_PAIR, LANES, LANES), F32)],
        scratch_shapes=[pltpu.VMEM((N_PAIR, LANES, LANES), F32)],
        compiler_params=_cparams("parallel", "arbitrary"),
        name="retention",
    )(rq, rk, rv, rg, decay, qdec, kdec, gcl, bd, norm_g)


def _layer_norm(v, g, b):
    mu = jnp.mean(v, -1, keepdims=True)
    d = v - mu
    var = jnp.mean(d * d, -1, keepdims=True)
    return d * lax.rsqrt(var + LN_EPS) * g + b


def _tail_kernel(x_ref, nsa_ref, ret_ref, wo_ref, g1_ref, b1_ref, wu_ref, wd_ref, g2_ref, b2_ref, y_ref, *, alpha):
    mix = _dot(nsa_ref[...], wo_ref[0:NSA_WIDTH, :]) + _dot(ret_ref[...], wo_ref[NSA_WIDTH:, :])
    x1 = _layer_norm(alpha * x_ref[...] + mix, g1_ref[...], b1_ref[...])
    x1b = x1.astype(BF16)
    f = jnp.zeros(x1.shape, F32)
    for j in range(D_FF // D_MODEL):
        sl = slice(j * D_MODEL, (j + 1) * D_MODEL)
        h = jnp.maximum(_dot(x1b, wu_ref[:, sl]), 0.0)
        f = f + _dot((h * h).astype(BF16), wd_ref[sl, :])
    y_ref[...] = _layer_norm(alpha * x1 + f, g2_ref[...], b2_ref[...])


def _tail(x2d, nsa, ret, wo, g1, b1, wu, wd, g2, b2, *, tm, alpha):
    rows = x2d.shape[0]
    row_spec = lambda w: pl.BlockSpec((tm, w), lambda i: (i, 0))
    consts = (wo, g1, b1, wu, wd, g2, b2)
    return pl.pallas_call(
        functools.partial(_tail_kernel, alpha=alpha),
        grid=(rows // tm,),
        in_specs=[row_spec(D_MODEL), row_spec(NSA_WIDTH), row_spec(RET_WIDTH)] + [_const_spec(c.shape) for c in consts],
        out_specs=row_spec(D_MODEL),
        out_shape=jax.ShapeDtypeStruct((rows, D_MODEL), F32),
        compiler_params=_cparams("parallel"),
        name="tail",
    )(x2d, nsa, ret, *consts)


DEC_ROWS = 16
N_DEC_SEL = TOP_N - 1


def _dec_stack_q(q, low):
    rows = lax.broadcasted_iota(jnp.int32, (DEC_ROWS, 1), 0)
    qs = jnp.zeros((DEC_ROWS, LANES), F32)
    for g in range(N_KV):
        for c in range(HPG):
            piece = jnp.where(low if g == 0 else ~low, q[:, c * LANES:(c + 1) * LANES], 0.0)
            qs = jnp.where(rows == g * HPG + c, piece, qs)
    return qs, rows


def _dec_score_kernel(q_ref, kcvc_ref, m_ref, oc_o, idx_o, *, past):
    lane = lax.broadcasted_iota(jnp.int32, (1, LANES), 1)
    low = lane < HEAD_DIM
    qs, rows = _dec_stack_q(q_ref[...], low)
    qsb = qs.astype(BF16)
    kcvc = kcvc_ref[...]
    kc = kcvc[:, 0:LANES].astype(BF16)
    vc = kcvc[:, LANES:2 * LANES].astype(BF16)
    n_cmp = kcvc.shape[0]
    cend = lax.broadcasted_iota(jnp.int32, (1, n_cmp), 1) * CMP_STRIDE + (CMP_BLOCK - 1)
    p = _masked_softmax(_dot_nt(qsb, kc), jnp.broadcast_to(cend <= past, (DEC_ROWS, n_cmp)))
    oc_o[...] = _dot(p.astype(BF16), vc)

    psum = jnp.zeros((DEC_ROWS, n_cmp), F32)
    for g in range(N_KV):
        acc = p[g * HPG:g * HPG + 1]
        for c in range(1, HPG):
            acc = acc + p[g * HPG + c:g * HPG + c + 1]
        psum = jnp.where(rows == g, acc, psum)
    m = m_ref[...]
    imp = sum(_dot(part, m) for part in _split3(psum))
    lane_f = lane.astype(F32)
    forced = (lane == 0) | (lane == N_SEL_BLOCKS - 1)
    score = imp + FORCE_BONUS * forced.astype(F32)
    picked = jnp.zeros((DEC_ROWS, LANES), F32)
    for r in range(N_DEC_SEL):
        best = jnp.max(score, -1, keepdims=True)
        first = jnp.min(jnp.where(score == best, lane_f, float(LANES)), -1, keepdims=True)
        picked = jnp.where(lane == r, first, picked)
        score = jnp.where(lane_f == first, BELOW_NEG, score)
    idx_o[...] = picked.astype(jnp.int32)


def _dec_score(q3, kcvc, m, *, past):
    nb = q3.shape[0]
    n_cmp = kcvc.shape[1]
    return pl.pallas_call(
        functools.partial(_dec_score_kernel, past=past),
        grid=(nb,),
        in_specs=[pl.BlockSpec((None, 1, 4 * LANES), lambda b: (b, 0, 0)),
                  pl.BlockSpec((None, n_cmp, 2 * LANES), lambda b: (b, 0, 0)),
                  _const_spec(m.shape)],
        out_specs=[pl.BlockSpec((None, DEC_ROWS, LANES), lambda b: (b, 0, 0))] * 2,
        out_shape=[jax.ShapeDtypeStruct((nb, DEC_ROWS, LANES), F32), jax.ShapeDtypeStruct((nb, DEC_ROWS, LANES), jnp.int32)],
        compiler_params=_cparams("parallel"),
        name="dec_score",
    )(q3, kcvc, m)


def _round_bf16(x):
    return x.astype(BF16).astype(F32)


def _dec_attn_kernel(pt_ref, idx_ref, q_ref, g_ref, oc_ref, kvn_ref, wn_ref, cache_ref, cw_ref,
                     rq_ref, rk_ref, rv_ref, rg_ref, st_ref, gl_ref, ng_ref,
                     nsa_o, ret_o, win_o, st_o, kbuf, sem):
    b = pl.program_id(0)
    nb = pl.num_programs(0)
    slot = b % 2

    def blk_copy(bb, sl, g, k):
        j = idx_ref[bb, g, k]
        return pltpu.make_async_copy(
            cache_ref.at[pt_ref[bb, j >> 1], pl.ds(2 * LANES, 2 * LANES), :],
            kbuf.at[sl, g, :, pl.ds(k * PAGE_SIZE, PAGE_SIZE)],
            sem.at[sl])

    def start_all(bb, sl):
        for g in range(N_KV):
            for k in range(N_DEC_SEL):
                blk_copy(bb, sl, g, k).start()

    @pl.when(b == 0)
    def _():
        start_all(0, 0)

    @pl.when(b + 1 < nb)
    def _():
        start_all(b + 1, 1 - slot)

    lane = lax.broadcasted_iota(jnp.int32, (1, LANES), 1)
    low = lane < HEAD_DIM
    qs, rows = _dec_stack_q(q_ref[...], low)
    qsb = qs.astype(BF16)
    g0_rows = rows < HPG

    cw = cw_ref[...]
    wb = cw.shape[1]
    wn = wn_ref[...]
    col = lax.broadcasted_iota(jnp.int32, (1, wb), 1)
    valid = jnp.broadcast_to((wb - col) < WINDOW, (DEC_ROWS, wb))
    s_w = jnp.where(valid, _dot(qsb, cw[0:LANES].astype(BF16)), NEG)
    s_wn = jnp.sum(qs * _round_bf16(wn[:, 0:LANES]), -1, keepdims=True)
    m_w = jnp.maximum(jnp.max(s_w, -1, keepdims=True), s_wn)
    p_w = jnp.exp(s_w - m_w) * valid.astype(F32)
    p_wn = jnp.exp(s_wn - m_w)
    l_w = jnp.maximum(jnp.sum(p_w, -1, keepdims=True) + p_wn, 1e-30)
    o_w = _dot_nt((p_w / l_w).astype(BF16), cw[LANES:2 * LANES].astype(BF16)) + _round_bf16(p_wn / l_w) * _round_bf16(wn[:, LANES:2 * LANES])
    new_cols = jnp.concatenate(
        [jnp.concatenate([jnp.transpose(jnp.broadcast_to(wn[:, u * LANES:(u + 1) * LANES], (LANES, LANES)))] * (wb // LANES), axis=1)
         for u in range(2)], axis=0)
    win_o[...] = jnp.where(col == wb - 1, new_cols, pltpu.roll(cw, wb - 1, 1))

    for g in range(N_KV):
        for k in range(N_DEC_SEL):
            blk_copy(b, slot, g, k).wait()
    kvn = kvn_ref[...]
    s_g = [_dot(qsb, kbuf[slot, g, 0:LANES, :].astype(BF16)) for g in range(N_KV)]
    pos = lax.broadcasted_iota(jnp.int32, (1, N_DEC_SEL * PAGE_SIZE), 1)
    picked_half = []
    for g in range(N_KV):
        half = jnp.zeros(pos.shape, jnp.int32)
        for k in range(N_DEC_SEL):
            half = jnp.where(pos >> 7 == k, idx_ref[b, g, k] & 1, half)
        picked_half.append(half)
    in_block = ((pos >> 6) & 1) == jnp.where(g0_rows, picked_half[0], picked_half[1])
    s_s = jnp.where(in_block, jnp.where(g0_rows, s_g[0], s_g[1]), NEG)
    s_sn = jnp.sum(qs * _round_bf16(kvn[:, 2 * LANES:3 * LANES]), -1, keepdims=True)
    m_sel = jnp.maximum(jnp.max(s_s, -1, keepdims=True), s_sn)
    p_s = jnp.exp(s_s - m_sel)
    p_sn = jnp.exp(s_sn - m_sel)
    l_sel = jnp.maximum(jnp.sum(p_s, -1, keepdims=True) + p_sn, 1e-30)
    pn = (p_s / l_sel).astype(BF16)
    o_g = [_dot_nt(pn, kbuf[slot, g, LANES:2 * LANES, :].astype(BF16)) for g in range(N_KV)]
    o_s = jnp.where(g0_rows, o_g[0], o_g[1]) + _round_bf16(p_sn / l_sel) * _round_bf16(kvn[:, 3 * LANES:4 * LANES])

    gates = g_ref[...]
    gcol = [jnp.zeros((DEC_ROWS, 1), F32) for _ in range(3)]
    for h in range(N_NSA_HEADS):
        for k in range(3):
            gcol[k] = jnp.where(rows == h, gates[:, 3 * h + k:3 * h + k + 1], gcol[k])
    o = gcol[0] * oc_ref[...] + gcol[1] * o_s + gcol[2] * o_w
    for c in range(HPG):
        nsa_o[:, c * LANES:(c + 1) * LANES] = jnp.where(low, o[c:c + 1], o[HPG + c:HPG + c + 1])

    for p in range(N_PAIR):
        sl = slice(p * LANES, (p + 1) * LANES)
        q, k, v, gl = rq_ref[:, sl], rk_ref[:, sl], rv_ref[:, sl], gl_ref[:, sl]
        state = st_ref[p]
        inner = _head_sum(q * k, low)

        def per_dk(row):
            t = jnp.transpose(jnp.broadcast_to(row, (LANES, LANES)))
            return jnp.where(low, t[0:RET_DK], t[RET_DK:2 * RET_DK])

        o_r = inner * v + jnp.sum(per_dk(q * gl) * state, axis=0, keepdims=True)
        st_o[p] = gl * state + per_dk(k) * v
        ret_o[:, sl] = _gated_group_norm(o_r, rg_ref[:, sl], ng_ref[:, sl], low)


def _dec_attn(page_table, idx, q3, gates3, oc, kvn3, wn3, cache4, cwin, rq3, rk3, rv3, rg3, state, gl, ng):
    nb = q3.shape[0]
    wb = cwin.shape[2]
    row3 = lambda w: pl.BlockSpec((None, 1, w), lambda b, *_: (b, 0, 0))
    st_spec = pl.BlockSpec((None, N_PAIR, RET_DK, LANES), lambda b, *_: (b, 0, 0, 0))
    win_spec = pl.BlockSpec((None, 2 * LANES, wb), lambda b, *_: (b, 0, 0))
    grid_spec = pltpu.PrefetchScalarGridSpec(
        num_scalar_prefetch=2,
        grid=(nb,),
        in_specs=[row3(4 * LANES), row3(LANES), pl.BlockSpec((None, DEC_ROWS, LANES), lambda b, *_: (b, 0, 0)),
                  row3(4 * LANES), row3(2 * LANES), pl.BlockSpec(memory_space=pl.ANY), win_spec,
                  row3(4 * LANES), row3(4 * LANES), row3(4 * LANES), row3(4 * LANES), st_spec,
                  pl.BlockSpec(gl.shape, lambda b, *_: (0, 0)), pl.BlockSpec(ng.shape, lambda b, *_: (0, 0))],
        out_specs=[row3(4 * LANES), row3(4 * LANES), win_spec, st_spec],
        scratch_shapes=[pltpu.VMEM((2, N_KV, 2 * LANES, N_DEC_SEL * PAGE_SIZE), F32), pltpu.SemaphoreType.DMA((2,))],
    )
    return pl.pallas_call(
        _dec_attn_kernel,
        grid_spec=grid_spec,
        out_shape=[jax.ShapeDtypeStruct((nb, 1, 4 * LANES), F32), jax.ShapeDtypeStruct((nb, 1, 4 * LANES), F32),
                   jax.ShapeDtypeStruct((nb, 2 * LANES, wb), F32), jax.ShapeDtypeStruct(state.shape, F32)],
        compiler_params=_cparams("arbitrary"),
        name="dec_attn",
    )(page_table, idx, q3, gates3, oc, kvn3, wn3, cache4, cwin, rq3, rk3, rv3, rg3, state, gl, ng)


def _permute_w_in(w):
    p0 = NSA_WIDTH
    p1 = p0 + 6 * KV_WIDTH
    p2 = p1 + 3 * N_NSA_HEADS
    wq = w[:, :p0].reshape(D_MODEL, N_NSA_HEADS, HEAD_DIM)[:, jnp.array(HEAD_PERM)].reshape(D_MODEL, p0)
    w_gates = jnp.pad(w[:, p1:p2], ((0, 0), (0, LANES - 3 * N_NSA_HEADS)))
    return jnp.concatenate([wq, w[:, p0:p1], w[:, p2:], w_gates], axis=1).astype(BF16)


def _permute_w_o(w):
    w_nsa = w[:NSA_WIDTH].reshape(N_NSA_HEADS, HEAD_DIM, D_MODEL)[jnp.array(HEAD_PERM)].reshape(NSA_WIDTH, D_MODEL)
    return jnp.concatenate([w_nsa, w[NSA_WIDTH:]], axis=0).astype(BF16)


def kernel(x_prompt, x_sample, cache_kv, cache_win, state_ret, page_table, w_in, w_cmp_k, w_cmp_v, pos_cmp_k, pos_cmp_v, ret_norm_g, w_o, ln1_g, ln1_b, w_up, w_down, ln2_g, ln2_b):
    depth = w_in.shape[0]
    assert depth == 1 and x_sample.shape[1] == 1
    batch, seq, _ = x_prompt.shape
    nb = x_sample.shape[0]
    n_pages = page_table.shape[1]
    past = n_pages * PAGE_SIZE
    assert past // SEL_BLOCK == N_SEL_BLOCKS and cache_win.shape[2] == WINDOW
    alpha = (2 * depth) ** 0.25
    log_gamma = jnp.log1p(-jnp.exp2(-5.0 - jnp.arange(N_RET_HEADS, dtype=F32)))

    w_perm = _permute_w_in(w_in[0])
    wo = _permute_w_o(w_o[0])
    wu, wd = w_up[0].astype(BF16), w_down[0].astype(BF16)
    vec = lambda a: a[0].reshape(1, -1)
    cmp_w = _compress_weights(w_cmp_k[0], w_cmp_v[0], pos_cmp_k[0], pos_cmp_v[0])
    tail_consts = (wo, vec(ln1_g), vec(ln1_b), wu, wd, vec(ln2_g), vec(ln2_b))
    norm_g = vec(ret_norm_g)

    tm = 512
    pos_p = jnp.arange(seq, dtype=jnp.int32)
    tabs_p = _rope_tables(pos_p, ROPE_THETA, ROPE_DIMS) + _rope_tables(pos_p, RET_THETA, RET_DK)
    x2d = x_prompt.reshape(batch * seq, D_MODEL)
    kv4t, wint, rq, rk, rv, rg, qt, gt, kaug, vst, kw, vwt = _project(x2d, w_perm, tabs_p, tm=tm, seq=seq, prompt=True)
    seq_pages = seq // PAGE_SIZE
    pidx_p = jnp.broadcast_to(jnp.arange(batch, dtype=jnp.int32)[:, None], (batch, seq_pages))
    poff_p = jnp.broadcast_to(jnp.arange(seq_pages, dtype=jnp.int32)[None, :] * PAGE_SIZE, (batch, seq_pages))
    kcvc_p = _compress(pidx_p, poff_p, kv4t, *cmp_w)
    o_nsa = _nsa_prompt(qt, gt, kcvc_p, kaug, vst, kw, vwt, _importance_matrix(seq // CMP_STRIDE).T.astype(BF16), batch=batch, seq=seq)
    ret, st_p = _retention_prompt(rq, rk, rv, rg, _retention_tables(log_gamma, RET_CHUNK), norm_g, batch=batch, seq=seq)
    y_p = _tail(x2d, o_nsa, ret, *tail_consts, tm=tm, alpha=alpha)

    y_prompt = y_p.reshape(batch, seq, D_MODEL)
    to_rows = lambda a, n_slot: a.reshape(a.shape[0], n_slot, N_KV, HEAD_DIM, a.shape[-1]).transpose(0, 4, 1, 2, 3)[None]
    kv_prompt = to_rows(kv4t, 4)
    wlen = min(WINDOW, seq)
    win_prompt = to_rows(wint[:, :, seq - wlen:], 2)
    st6 = st_p.reshape(batch, N_PAIR, 2, RET_DK, 2, RET_DV)
    ret_prompt = jnp.stack([st6[:, :, 0, :, 0, :], st6[:, :, 1, :, 1, :]], axis=2).reshape(1, batch, N_RET_HEADS, RET_DK, RET_DV)

    pos_s = jnp.full((nb,), past, dtype=jnp.int32)
    tabs_s = _rope_tables(pos_s, ROPE_THETA, ROPE_DIMS) + _rope_tables(pos_s, RET_THETA, RET_DK)
    xs2d = x_sample.reshape(nb, D_MODEL)
    kv4_d, win_d, rq_d, rk_d, rv_d, rg_d, q_d, gates_d = _project(xs2d, w_perm, tabs_s, tm=nb, seq=seq, prompt=False)
    cache_t = cache_kv[0].transpose(0, 2, 3, 4, 1).reshape(cache_kv.shape[1], 4 * KV_WIDTH, PAGE_SIZE)
    cwin_t = cache_win[0].transpose(0, 2, 3, 4, 1).reshape(nb, 2 * KV_WIDTH, WINDOW)
    kcvc_d = _compress(page_table, jnp.zeros_like(page_table), cache_t, *cmp_w)
    row3 = lambda a: a.reshape(nb, 1, a.shape[-1])
    q3 = row3(q_d)
    oc_d, idx_d = _dec_score(q3, kcvc_d, _importance_matrix(past // CMP_STRIDE).astype(BF16), past=past)
    idx = idx_d[:, :N_KV, :TOP_N]
    state = state_ret[0].astype(F32).reshape(nb, N_PAIR, 2, RET_DK, RET_DV).transpose(0, 1, 3, 2, 4).reshape(nb, N_PAIR, RET_DK, LANES)
    gl = jnp.repeat(jnp.exp(log_gamma), RET_DK)[None, :]
    nsa_d, ret_d, win_st, st_d = _dec_attn(page_table, idx, q3, row3(gates_d), oc_d, row3(kv4_d), row3(win_d), cache_t, cwin_t,
                                          row3(rq_d), row3(rk_d), row3(rv_d), row3(rg_d), state, gl, norm_g)
    y_s = _tail(xs2d, nsa_d.reshape(nb, NSA_WIDTH).astype(BF16), ret_d.reshape(nb, RET_WIDTH).astype(BF16), *tail_consts, tm=nb, alpha=alpha)

    y_sample = y_s.reshape(nb, 1, D_MODEL)
    kv_sample = kv4_d.reshape(1, nb, 1, 4, N_KV, HEAD_DIM)
    win_sample = to_rows(win_st, 2)
    ret_sample = st_d.reshape(nb, N_PAIR, RET_DK, 2, RET_DV).transpose(0, 1, 3, 2, 4).reshape(1, nb, N_RET_HEADS, RET_DK, RET_DV)
    return (y_prompt, y_sample, kv_prompt, kv_sample, win_prompt, win_sample, ret_prompt, ret_sample)
```

```python
import functools

import jax
import jax.numpy as jnp
from jax import lax
from jax.experimental import pallas as pl
from jax.experimental.pallas import tpu as pltpu

F32 = jnp.float32
BF16 = jnp.bfloat16

D_MODEL = 1024
HEAD_DIM = 64
N_NSA_HEADS = 8
N_KV = 2
HPG = N_NSA_HEADS // N_KV
N_RET_HEADS = 8
RET_DK = 64
RET_DV = 64
NSA_WIDTH = N_NSA_HEADS * HEAD_DIM
RET_WIDTH = N_RET_HEADS * RET_DV
KV_WIDTH = N_KV * HEAD_DIM
D_FF = 4 * D_MODEL
CMP_STRIDE = 16
CMP_BLOCK = 32
SEL_BLOCK = 64
TOP_N = 16
N_FORCED = 3
WINDOW = 512
RET_CHUNK = 128
PAGE_SIZE = 128
ROPE_THETA = 500000.0
ROPE_DIMS = HEAD_DIM // 4
RET_THETA = 10000.0
LN_EPS = 1e-5
FORCE_BONUS = 1e4
NEG = -1e30
BELOW_NEG = -3e38

LANES = 128
SUBLANES = 8
VMEM_LIMIT = 56 * 1024 * 1024

LOG2E = 1.4426950408889634
BF16_SUBLANES = 16
ONES_ROWS = BF16_SUBLANES
CHUNK_PITCH = 20
PAGES_PER_STEP = 4
Q_TILE = 128
KEY_TILE = 512
N_PAIR = N_RET_HEADS // 2
N_SEL_BLOCKS = 128

U_Q, U_KV, U_RQ, U_RK, U_RV, U_RG, U_G, U_TOTAL = 0, 4, 10, 14, 18, 22, 26, 27
HEAD_PERM = (0, 4, 1, 5, 2, 6, 3, 7)


def _dot(a, b):
    return jnp.dot(a, b, preferred_element_type=F32)


def _dot_nt(a, b):
    return lax.dot_general(a, b, (((1,), (1,)), ((), ())), preferred_element_type=F32)


def _cparams(*sem):
    return pltpu.CompilerParams(dimension_semantics=sem, vmem_limit_bytes=VMEM_LIMIT)


def _const_spec(shape):
    nd = len(shape)
    return pl.BlockSpec(shape, lambda *_: (0,) * nd)


def _rope(u, c, s_fwd, s_bwd, half):
    return u * c + pltpu.roll(u, LANES - half, 1) * s_fwd + pltpu.roll(u, half, 1) * s_bwd


def _proj_kernel(x_ref, w_ref, cq_ref, sqf_ref, sqb_ref, cr_ref, srf_ref, srb_ref, *outs, tm, seq, prompt):
    if prompt:
        kv4t_o, wint_o, rq_o, rk_o, rv_o, rg_o, qt_o, gt_o, kaug_o, vst_o, kw_o, vwt_o = outs
    else:
        kv4_o, win_o, rq_o, rk_o, rv_o, rg_o, q_o, g_o = outs
    xb = x_ref[...].astype(BF16)

    def seg(u0, n):
        return _dot(xb, w_ref[:, u0 * LANES:(u0 + n) * LANES])

    def unit(z, u):
        return z[:, u * LANES:(u + 1) * LANES]

    cq, sqf, sqb = cq_ref[...], sqf_ref[...], sqb_ref[...]
    cr, srf, srb = cr_ref[...], srf_ref[...], srb_ref[...]
    nsa_half, ret_half = ROPE_DIMS // 2, RET_DK // 2

    zq = seg(U_Q, 4)
    q = jnp.concatenate([_rope(unit(zq, c), cq, sqf, sqb, nsa_half) for c in range(4)], axis=1) * (HEAD_DIM ** -0.5)
    gates = jax.nn.sigmoid(seg(U_G, 1))
    zkv = seg(U_KV, 6)
    k_slc = _rope(unit(zkv, 2), cq, sqf, sqb, nsa_half)
    k_win = _rope(unit(zkv, 4), cq, sqf, sqb, nsa_half)
    kv4 = jnp.concatenate([zkv[:, 0:2 * LANES], k_slc, unit(zkv, 3)], axis=1)
    win = jnp.concatenate([k_win, unit(zkv, 5)], axis=1)
    if prompt:
        kv4t = jnp.transpose(kv4)
        wint = jnp.transpose(win)
        kv4t_o[...] = kv4t
        wint_o[...] = wint
        qt_o[...] = jnp.transpose(q * LOG2E).astype(BF16)
        gt_o[...] = jnp.transpose(gates)
        row = pl.program_id(0) * tm + lax.broadcasted_iota(jnp.int32, (tm, 1), 0)
        blk = (row & (seq - 1)) >> 6
        onehot = lax.broadcasted_iota(jnp.int32, (tm, LANES), 1) == blk
        kaug_o[:, 0:LANES] = k_slc.astype(BF16)
        kaug_o[:, LANES:2 * LANES] = jnp.where(onehot, 1.0, 0.0).astype(BF16)
        vst_o[0:LANES, :] = kv4t[3 * LANES:4 * LANES].astype(BF16)
        vst_o[LANES:, :] = jnp.ones((ONES_ROWS, tm), BF16)
        kw_o[...] = k_win.astype(BF16)
        vwt = wint[LANES:2 * LANES].astype(BF16)
        for j in range(tm // LANES):
            vwt_o[j] = vwt[:, j * LANES:(j + 1) * LANES]
    else:
        kv4_o[...] = kv4
        win_o[...] = win
        q_o[...] = q.astype(BF16).astype(F32)
        g_o[...] = gates

    zrq = seg(U_RQ, 4)
    zrk = seg(U_RK, 4)
    for c in range(4):
        sl = slice(c * LANES, (c + 1) * LANES)
        rq_o[:, sl] = _rope(unit(zrq, c), cr, srf, srb, ret_half)
        rk_o[:, sl] = _rope(unit(zrk, c), cr, srf, srb, ret_half) * (RET_DK ** -0.5)
    rv_o[...] = seg(U_RV, 4)
    rg_o[...] = seg(U_RG, 4)


def _rope_tables(pos, theta, n_rot):
    half = n_rot // 2
    inv = theta ** (-jnp.arange(half, dtype=F32) / half)
    ang = pos.astype(F32)[:, None] * inv[None, :]
    cos, sin = jnp.cos(ang), jnp.sin(ang)
    n = pos.shape[0]
    rest = HEAD_DIM - n_rot
    c = jnp.concatenate([cos, cos, jnp.ones((n, rest), F32)], 1)
    s_fwd = jnp.concatenate([-sin, jnp.zeros((n, half + rest), F32)], 1)
    s_bwd = jnp.concatenate([jnp.zeros((n, half), F32), sin, jnp.zeros((n, rest), F32)], 1)
    rep = LANES // HEAD_DIM
    return tuple(jnp.tile(t, (1, rep)) for t in (c, s_fwd, s_bwd))


def _project(x2d, w_perm, tables, *, tm, seq, prompt):
    rows = x2d.shape[0]
    n_tab_tiles = tables[0].shape[0] // tm
    row_spec = lambda w: pl.BlockSpec((tm, w), lambda i: (i, 0))
    tab_spec = pl.BlockSpec((tm, LANES), lambda i: (i % n_tab_tiles, 0))
    if prompt:
        assert tm == KEY_TILE and seq % tm == 0
        batch, tpb = rows // seq, seq // tm
        feat_spec = lambda f: pl.BlockSpec((None, f, tm), lambda i: (i // tpb, 0, i % tpb))
        outs = [(feat_spec(4 * LANES), (batch, 4 * LANES, seq), F32), (feat_spec(2 * LANES), (batch, 2 * LANES, seq), F32)]
    else:
        outs = [(row_spec(4 * LANES), (rows, 4 * LANES), F32), (row_spec(2 * LANES), (rows, 2 * LANES), F32)]
    outs += [(row_spec(4 * LANES), (rows, 4 * LANES), F32)] * 4
    if prompt:
        outs += [(feat_spec(4 * LANES), (batch, 4 * LANES, seq), BF16),
                 (feat_spec(LANES), (batch, LANES, seq), F32),
                 (row_spec(2 * LANES), (rows, 2 * LANES), BF16),
                 (pl.BlockSpec((None, LANES + ONES_ROWS, tm), lambda i: (i, 0, 0)), (rows // tm, LANES + ONES_ROWS, tm), BF16),
                 (row_spec(LANES), (rows, LANES), BF16),
                 (pl.BlockSpec((tm // LANES, LANES, LANES), lambda i: (i, 0, 0)), (rows // LANES, LANES, LANES), BF16)]
    else:
        outs += [(row_spec(4 * LANES), (rows, 4 * LANES), F32), (row_spec(LANES), (rows, LANES), F32)]
    return pl.pallas_call(
        functools.partial(_proj_kernel, tm=tm, seq=seq, prompt=prompt),
        grid=(rows // tm,),
        in_specs=[row_spec(D_MODEL), _const_spec(w_perm.shape)] + [tab_spec] * 6,
        out_specs=[s for s, _, _ in outs],
        out_shape=[jax.ShapeDtypeStruct(shape, dt) for _, shape, dt in outs],
        compiler_params=_cparams("parallel"),
        name="proj_prompt" if prompt else "proj_decode",
    )(x2d, w_perm, *tables)


def _cmp_kernel(pn_ref, po_ref, src_ref, wlo_ref, whi_ref, plo_ref, phi_ref, out_ref, stage, slab, sem, *, n_pages):
    b = pl.program_id(0)
    nb = pl.num_programs(0)
    slot = b % 2

    def page_copy(bb, sl, p, kv):
        return pltpu.make_async_copy(
            src_ref.at[pn_ref[bb, p], pl.ds(kv * LANES, LANES), pl.ds(pl.multiple_of(po_ref[bb, p], PAGE_SIZE), PAGE_SIZE)],
            stage.at[sl, kv, p],
            sem.at[sl])

    def start_all(bb, sl):
        for p in range(n_pages):
            for kv in range(2):
                page_copy(bb, sl, p, kv).start()

    @pl.when(b == 0)
    def _():
        start_all(0, 0)

    @pl.when(b + 1 < nb)
    def _():
        start_all(b + 1, 1 - slot)

    for p in range(n_pages):
        for kv in range(2):
            page_copy(b, slot, p, kv).wait()

    chunks_per_page = PAGE_SIZE // CMP_STRIDE
    page_rows = chunks_per_page * CHUNK_PITCH

    def to_rows(pp, carry):
        for u in range(PAGES_PER_STEP):
            p = pp * PAGES_PER_STEP + u
            r0 = pl.multiple_of(p * page_rows, SUBLANES)
            for kv in range(2):
                x = jnp.transpose(stage[slot, kv, p])
                for c in range(chunks_per_page):
                    slab[kv, pl.ds(r0 + c * CHUNK_PITCH, CMP_STRIDE), :] = x[c * CMP_STRIDE:(c + 1) * CMP_STRIDE]
        return carry

    lax.fori_loop(0, n_pages // PAGES_PER_STEP, to_rows, 0)

    n_chunks = n_pages * chunks_per_page
    acc_lo = jnp.zeros((n_chunks, 2 * LANES), F32)
    acc_hi = jnp.zeros((n_chunks, 2 * LANES), F32)
    for r in range(CMP_STRIDE):
        xr = jnp.concatenate([slab[kv, pl.ds(r, n_chunks, stride=CHUNK_PITCH), :] for kv in range(2)], axis=1)
        acc_lo = acc_lo + _dot((xr + plo_ref[r:r + 1, :]).astype(BF16), wlo_ref[r])
        acc_hi = acc_hi + _dot((xr + phi_ref[r:r + 1, :]).astype(BF16), whi_ref[r])
    nxt = pltpu.roll(acc_hi, n_chunks - 1, 0)
    last = lax.broadcasted_iota(jnp.int32, (n_chunks, 1), 0) == n_chunks - 1
    out_ref[...] = jnp.where(last, 0.0, acc_lo + nxt)


def _compress(page_index, page_offset, src, wlo, whi, plo, phi):
    nb, n_pages = page_index.shape
    n_chunks = n_pages * PAGE_SIZE // CMP_STRIDE
    grid_spec = pltpu.PrefetchScalarGridSpec(
        num_scalar_prefetch=2,
        grid=(nb,),
        in_specs=[pl.BlockSpec(memory_space=pl.ANY),
                  pl.BlockSpec(wlo.shape, lambda b, *_: (0, 0, 0)),
                  pl.BlockSpec(whi.shape, lambda b, *_: (0, 0, 0)),
                  pl.BlockSpec(plo.shape, lambda b, *_: (0, 0)),
                  pl.BlockSpec(phi.shape, lambda b, *_: (0, 0))],
        out_specs=pl.BlockSpec((None, n_chunks, 2 * LANES), lambda b, *_: (b, 0, 0)),
        scratch_shapes=[pltpu.VMEM((2, 2, n_pages, LANES, PAGE_SIZE), F32),
                        pltpu.VMEM((2, n_chunks * CHUNK_PITCH, LANES), F32),
                        pltpu.SemaphoreType.DMA((2,))],
    )
    return pl.pallas_call(
        functools.partial(_cmp_kernel, n_pages=n_pages),
        grid_spec=grid_spec,
        out_shape=jax.ShapeDtypeStruct((nb, n_chunks, 2 * LANES), F32),
        compiler_params=_cparams("arbitrary"),
        name="compress",
    )(page_index, page_offset, src, wlo, whi, plo, phi)


def _compress_weights(w_k, w_v, pos_k, pos_v):
    def bd(wk_r, wv_r):
        z = jnp.zeros((HEAD_DIM, HEAD_DIM), F32)
        rows = [[wk_r, z, z, z], [z, wk_r, z, z], [z, z, wv_r, z], [z, z, z, wv_r]]
        return jnp.block(rows)

    w_all = jax.vmap(bd)(w_k, w_v).astype(BF16)
    p_all = jnp.concatenate([pos_k, pos_k, pos_v, pos_v], axis=1)
    return w_all[:CMP_STRIDE], w_all[CMP_STRIDE:], p_all[:CMP_STRIDE], p_all[CMP_STRIDE:]


def _masked_softmax(s, mask):
    s = jnp.where(mask, s, NEG)
    m = jnp.max(s, -1, keepdims=True)
    p = jnp.exp(s - m) * mask.astype(F32)
    return p / jnp.maximum(jnp.sum(p, -1, keepdims=True), 1e-30)


def _split3(x):
    hi = x.astype(BF16)
    r1 = x - hi.astype(F32)
    mid = r1.astype(BF16)
    lo = (r1 - mid.astype(F32)).astype(BF16)
    return hi, mid, lo


def _importance_matrix(n_cmp):
    n = jnp.arange(n_cmp)[:, None]
    j = jnp.arange(N_SEL_BLOCKS)[None, :]
    inner = (n >= 4 * j) & (n <= 4 * j + 2)
    edge = (n == 4 * j - 1) | (n == 4 * j + 3)
    return jnp.where(inner, 2.0, jnp.where(edge, 1.0, 0.0)).astype(F32)


def _softmax_over_keys(s, mask):
    s = jnp.where(mask, s, NEG)
    m = jnp.max(s, 0, keepdims=True)
    p = jnp.exp2(s - m)
    inv = jnp.where(m > 0.5 * NEG, 1.0 / jnp.maximum(jnp.sum(p, 0, keepdims=True), 1e-30), 0.0)
    return p * inv


def _nsa_kernel(qt_ref, gt_ref, kcvc_ref, kaug_ref, vst_ref, kw_ref, vwt_ref, mt_ref, out_ref,
                kc_s, vct_s, m_s, acc_s, s_buf, tmax_s):
    i = pl.program_id(1)
    t0 = i * Q_TILE
    n_cmp = kcvc_ref.shape[0]

    @pl.when(i == 0)
    def _():
        kcvc = kcvc_ref[...]
        kc_s[...] = kcvc[:, 0:LANES].astype(BF16)
        vct_s[...] = jnp.transpose(kcvc[:, LANES:2 * LANES]).astype(BF16)

    lane = lax.broadcasted_iota(jnp.int32, (1, Q_TILE), 1)
    tq = jnp.concatenate([t0 + lane] * N_NSA_HEADS, axis=1)
    low_rows = lax.broadcasted_iota(jnp.int32, (LANES, 1), 0) < HEAD_DIM
    qt = qt_ref[...]
    zero = jnp.zeros((), BF16)
    q_all = jnp.concatenate([jnp.where(low_rows if g == 0 else ~low_rows, qt[c * LANES:(c + 1) * LANES], zero)
                             for g in range(N_KV) for c in range(HPG)], axis=1)

    cend = lax.broadcasted_iota(jnp.int32, (n_cmp, 1), 0) * CMP_STRIDE + (CMP_BLOCK - 1)
    p_c = _softmax_over_keys(_dot(kc_s[...], q_all), cend <= tq)
    o_c = _dot(vct_s[...], p_c.astype(BF16))

    w0 = pl.multiple_of(jnp.maximum(t0 - WINDOW, 0), Q_TILE)
    n_win = WINDOW + Q_TILE
    dist = tq - (w0 + lax.broadcasted_iota(jnp.int32, (n_win, 1), 0))
    p_w = _softmax_over_keys(_dot(kw_ref[pl.ds(w0, n_win), :], q_all), (dist >= 0) & (dist < WINDOW))
    w_blk = w0 // LANES
    vwt = jnp.concatenate([vwt_ref[w_blk + j] for j in range(n_win // LANES)], axis=1)
    o_w = _dot(vwt, p_w.astype(BF16))
    gt = gt_ref[...]
    gk = [jnp.concatenate([gt[3 * h + k:3 * h + k + 1] for h in range(N_NSA_HEADS)], axis=1) for k in range(3)]
    o_cw = gk[0] * o_c + gk[2] * o_w

    psum = jnp.concatenate([sum(p_c[:, (g * HPG + c) * Q_TILE:(g * HPG + c + 1) * Q_TILE] for c in range(HPG))
                            for g in range(N_KV)], axis=1)
    mt = mt_ref[...]
    imp_t = sum(_dot(mt, part) for part in _split3(psum))
    blk = lax.broadcasted_iota(jnp.int32, (N_SEL_BLOCKS, 1), 0)
    blk_f = blk.astype(F32)
    cur = jnp.concatenate([(t0 + lane) >> 6] * N_KV, axis=1)
    forced = (blk == 0) | (blk == cur) | (blk == cur - 1)
    score = jnp.where(forced, BELOW_NEG, jnp.where(blk <= cur, imp_t, NEG))
    bias = jnp.where(forced, 0.0, NEG)
    for _ in range(TOP_N - N_FORCED):
        best = jnp.max(score, axis=0, keepdims=True)
        first = jnp.min(jnp.where(score == best, blk_f, float(N_SEL_BLOCKS)), axis=0, keepdims=True)
        pick = blk_f == first
        bias = jnp.where(pick, 0.0, bias)
        score = jnp.where(pick, BELOW_NEG, score)
    bias = bias.astype(BF16)
    bias_all = jnp.concatenate([bias[:, g * Q_TILE:(g + 1) * Q_TILE] for g in range(N_KV) for _ in range(HPG)], axis=1)
    q_aug = jnp.concatenate([q_all, bias_all], axis=0)

    m_s[...] = jnp.full(m_s.shape, NEG, F32)
    acc_s[...] = jnp.zeros(acc_s.shape, F32)

    def scores(kt, causal):
        k0 = pl.multiple_of(kt * KEY_TILE, KEY_TILE)
        s = _dot(kaug_ref[pl.ds(k0, KEY_TILE), :], q_aug)
        if causal:
            kpos = k0 + lax.broadcasted_iota(jnp.int32, (KEY_TILE, 1), 0)
            s = jnp.where(kpos <= tq, s, NEG)
        s_buf[...] = s
        tmax_s[...] = jnp.max(s, 0, keepdims=True)

    def accumulate(kt):
        m_old = m_s[...]
        m_new = jnp.maximum(m_old, tmax_s[...])
        alpha = jnp.exp2(m_old - m_new)
        p = jnp.exp2(s_buf[...] - m_new)
        acc_s[...] = alpha * acc_s[...] + _dot(vst_ref[kt], p.astype(BF16))
        m_s[...] = m_new

    last = (t0 + Q_TILE - 1) // KEY_TILE

    @pl.when(last == 0)
    def _():
        scores(0, True)

    @pl.when(last > 0)
    def _():
        scores(0, False)

        def body(kt, carry):
            accumulate(kt)
            scores(kt + 1, False)
            return carry

        lax.fori_loop(0, last - 1, body, 0)
        accumulate(last - 1)
        scores(last, True)

    accumulate(last)
    o_s = acc_s[0:LANES, :] * (1.0 / acc_s[LANES:LANES + 1, :])

    o = o_cw + gk[1] * o_s
    half = HPG * Q_TILE
    comb = jnp.where(low_rows, o[:, 0:half], o[:, half:2 * half])
    for c in range(HPG):
        out_ref[:, c * LANES:(c + 1) * LANES] = jnp.transpose(comb[:, c * Q_TILE:(c + 1) * Q_TILE]).astype(BF16)


def _nsa_prompt(qt, gt, kcvc, kaug, vst, kw, vwt, mt, *, batch, seq):
    nq = seq // Q_TILE
    n_cols = N_NSA_HEADS * Q_TILE
    n_cmp = kcvc.shape[1]
    col_spec = lambda f: pl.BlockSpec((None, f, Q_TILE), lambda b, i: (b, 0, i))
    seq_spec = lambda w: pl.BlockSpec((seq, w), lambda b, i: (b, 0))
    return pl.pallas_call(
        _nsa_kernel,
        grid=(batch, nq),
        in_specs=[col_spec(4 * LANES), col_spec(LANES),
                  pl.BlockSpec((None, n_cmp, 2 * LANES), lambda b, i: (b, 0, 0)),
                  seq_spec(2 * LANES),
                  pl.BlockSpec((seq // KEY_TILE, LANES + ONES_ROWS, KEY_TILE), lambda b, i: (b, 0, 0)),
                  seq_spec(LANES),
                  pl.BlockSpec((seq // LANES, LANES, LANES), lambda b, i: (b, 0, 0)),
                  _const_spec(mt.shape)],
        out_specs=pl.BlockSpec((Q_TILE, 4 * LANES), lambda b, i: (b * nq + i, 0)),
        out_shape=jax.ShapeDtypeStruct((batch * seq, 4 * LANES), BF16),
        scratch_shapes=[pltpu.VMEM((n_cmp, LANES), BF16), pltpu.VMEM((LANES, n_cmp), BF16),
                        pltpu.VMEM((1, n_cols), F32), pltpu.VMEM((LANES + ONES_ROWS, n_cols), F32),
                        pltpu.VMEM((KEY_TILE, n_cols), F32), pltpu.VMEM((1, n_cols), F32)],
        compiler_params=_cparams("parallel", "arbitrary"),
        name="nsa_prompt",
    )(qt, gt, kcvc, kaug, vst, kw, vwt, mt)


def _head_sum(a, low):
    s0 = jnp.sum(jnp.where(low, a, 0.0), -1, keepdims=True)
    s1 = jnp.sum(jnp.where(low, 0.0, a), -1, keepdims=True)
    return jnp.where(low, s0, s1)


def _gated_group_norm(o, rg, norm_g, low):
    mu = _head_sum(o, low) * (1.0 / RET_DV)
    d = o - mu
    var = _head_sum(d * d, low) * (1.0 / RET_DV)
    gn = d * lax.rsqrt(var + LN_EPS) * norm_g
    return (rg * jax.nn.sigmoid(rg)) * gn


def _ret_kernel(rq_ref, rk_ref, rv_ref, rg_ref, decay_ref, qdec_ref, kdec_ref, gcl_ref, bd_ref, ng_ref,
                ret_o, st_o, st_s):
    c = pl.program_id(1)

    @pl.when(c == 0)
    def _():
        st_s[...] = jnp.zeros(st_s.shape, F32)

    low = lax.broadcasted_iota(jnp.int32, (1, LANES), 1) < RET_DK
    zero = jnp.zeros((), BF16)
    for p in range(N_PAIR):
        sl = slice(p * LANES, (p + 1) * LANES)
        q, k, v = rq_ref[:, sl], rk_ref[:, sl], rv_ref[:, sl]
        qb, kb, vb = q.astype(BF16), k.astype(BF16), v.astype(BF16)
        in0 = _dot_nt(jnp.where(low, qb, zero), kb) * decay_ref[2 * p]
        in1 = _dot_nt(jnp.where(low, zero, qb), kb) * decay_ref[2 * p + 1]
        inner = jnp.concatenate([in0.astype(BF16), in1.astype(BF16)], axis=1)
        v_split = jnp.concatenate([jnp.where(low, vb, zero), jnp.where(low, zero, vb)], axis=0)
        state = st_s[p]
        o = _dot(inner, v_split) + _dot((q * qdec_ref[:, sl]).astype(BF16), state.astype(BF16))
        kd_t = jnp.transpose(k * kdec_ref[:, sl]).astype(BF16)
        st_s[p] = state * gcl_ref[:, sl] + _dot(kd_t, vb) * bd_ref[...]
        ret_o[:, sl] = _gated_group_norm(o, rg_ref[:, sl], ng_ref[:, sl], low).astype(BF16)

    @pl.when(c == pl.num_programs(1) - 1)
    def _():
        st_o[...] = st_s[...]


def _retention_tables(log_gamma, chunk):
    i = jnp.arange(chunk, dtype=F32)
    diff = i[:, None] - i[None, :]
    decay = jnp.where(diff >= 0, jnp.exp(log_gamma[:, None, None] * jnp.maximum(diff, 0.0)), 0.0)
    per_lane = lambda t: jnp.repeat(t, RET_DK, axis=-1)
    qdec = per_lane(jnp.exp((i + 1.0)[:, None] * log_gamma[None, :]))
    kdec = per_lane(jnp.exp((chunk - 1.0 - i)[:, None] * log_gamma[None, :]))
    gcl = per_lane(jnp.exp(chunk * log_gamma)[None, :])
    r = jnp.arange(LANES)
    bd = ((r[:, None] // RET_DK) == (r[None, :] // RET_DV)).astype(F32)
    return decay, qdec, kdec, gcl, bd


def _retention_prompt(rq, rk, rv, rg, tables, norm_g, *, batch, seq):
    decay, qdec, kdec, gcl, bd = tables
    nc = seq // RET_CHUNK
    row_spec = pl.BlockSpec((RET_CHUNK, RET_WIDTH), lambda b, c: (b * nc + c, 0))
    return pl.pallas_call(
        _ret_kernel,
        grid=(batch, nc),
        in_specs=[row_spec] * 4 + [_const_spec(decay.shape), _const_spec(qdec.shape), _const_spec(kdec.shape),
                                   _const_spec(gcl.shape), _const_spec(bd.shape), _const_spec(norm_g.shape)],
        out_specs=[row_spec, pl.BlockSpec((None, N_PAIR, LANES, LANES), lambda b, c: (b, 0, 0, 0))],
        out_shape=[jax.ShapeDtypeStruct((batch * seq, RET_WIDTH), BF16),
                   jax.ShapeDtypeStruct((batch, N_PAIR, LANES, LANES), F32)],
        scratch_shapes=[pltpu.VMEM((N_PAIR, LANES, LANES), F32)],
        compiler_params=_cparams("parallel", "arbitrary"),
        name="retention",
    )(rq, rk, rv, rg, decay, qdec, kdec, gcl, bd, norm_g)


def _layer_norm(v, g, b):
    mu = jnp.mean(v, -1, keepdims=True)
    d = v - mu
    var = jnp.mean(d * d, -1, keepdims=True)
    return d * lax.rsqrt(var + LN_EPS) * g + b


def _tail_kernel(x_ref, nsa_ref, ret_ref, wo_ref, g1_ref, b1_ref, wu_ref, wd_ref, g2_ref, b2_ref, y_ref, *, alpha):
    mix = _dot(nsa_ref[...], wo_ref[0:NSA_WIDTH, :]) + _dot(ret_ref[...], wo_ref[NSA_WIDTH:, :])
    x1 = _layer_norm(alpha * x_ref[...] + mix, g1_ref[...], b1_ref[...])
    x1b = x1.astype(BF16)
    f = jnp.zeros(x1.shape, F32)
    for j in range(D_FF // D_MODEL):
        sl = slice(j * D_MODEL, (j + 1) * D_MODEL)
        h = jnp.maximum(_dot(x1b, wu_ref[:, sl]), 0.0)
        f = f + _dot((h * h).astype(BF16), wd_ref[sl, :])
    y_ref[...] = _layer_norm(alpha * x1 + f, g2_ref[...], b2_ref[...])


def _tail(x2d, nsa, ret, wo, g1, b1, wu, wd, g2, b2, *, tm, alpha):
    rows = x2d.shape[0]
    row_spec = lambda w: pl.BlockSpec((tm, w), lambda i: (i, 0))
    consts = (wo, g1, b1, wu, wd, g2, b2)
    return pl.pallas_call(
        functools.partial(_tail_kernel, alpha=alpha),
        grid=(rows // tm,),
        in_specs=[row_spec(D_MODEL), row_spec(NSA_WIDTH), row_spec(RET_WIDTH)] + [_const_spec(c.shape) for c in consts],
        out_specs=row_spec(D_MODEL),
        out_shape=jax.ShapeDtypeStruct((rows, D_MODEL), F32),
        compiler_params=_cparams("parallel"),
        name="tail",
    )(x2d, nsa, ret, *consts)


DEC_ROWS = 16
N_DEC_SEL = TOP_N - 1


def _dec_stack_q(q, low):
    rows = lax.broadcasted_iota(jnp.int32, (DEC_ROWS, 1), 0)
    qs = jnp.zeros((DEC_ROWS, LANES), F32)
    for g in range(N_KV):
        for c in range(HPG):
            piece = jnp.where(low if g == 0 else ~low, q[:, c * LANES:(c + 1) * LANES], 0.0)
            qs = jnp.where(rows == g * HPG + c, piece, qs)
    return qs, rows


def _dec_score_kernel(q_ref, kcvc_ref, m_ref, oc_o, idx_o, score_s, *, past):
    lane = lax.broadcasted_iota(jnp.int32, (1, LANES), 1)
    low = lane < HEAD_DIM
    qs, rows = _dec_stack_q(q_ref[...], low)
    qsb = qs.astype(BF16)
    kcvc = kcvc_ref[...]
    kc = kcvc[:, 0:LANES].astype(BF16)
    vc = kcvc[:, LANES:2 * LANES].astype(BF16)
    n_cmp = kcvc.shape[0]
    cend = lax.broadcasted_iota(jnp.int32, (1, n_cmp), 1) * CMP_STRIDE + (CMP_BLOCK - 1)
    p = _masked_softmax(_dot_nt(qsb, kc), jnp.broadcast_to(cend <= past, (DEC_ROWS, n_cmp)))
    oc_o[...] = _dot(p.astype(BF16), vc)

    psum = jnp.zeros((DEC_ROWS, n_cmp), F32)
    for g in range(N_KV):
        acc = p[g * HPG:g * HPG + 1]
        for c in range(1, HPG):
            acc = acc + p[g * HPG + c:g * HPG + c + 1]
        psum = jnp.where(rows == g, acc, psum)
    m = m_ref[...]
    imp = sum(_dot(part, m) for part in _split3(psum))
    forced = (lane == 0) | (lane == N_SEL_BLOCKS - 1)
    b = pl.program_id(0)
    score_s[b] = imp + FORCE_BONUS * forced.astype(F32)

    @pl.when(b == pl.num_programs(0) - 1)
    def _():
        lane_f = lane.astype(F32)
        score = score_s[...].reshape(-1, LANES)
        picked = jnp.zeros(score.shape, F32)
        for r in range(N_DEC_SEL):
            best = jnp.max(score, -1, keepdims=True)
            first = jnp.min(jnp.where(score == best, lane_f, float(LANES)), -1, keepdims=True)
            picked = jnp.where(lane == r, first, picked)
            score = jnp.where(lane_f == first, BELOW_NEG, score)
        idx_o[...] = picked.astype(jnp.int32).reshape(idx_o.shape)


def _dec_score(q3, kcvc, m, *, past):
    nb = q3.shape[0]
    n_cmp = kcvc.shape[1]
    return pl.pallas_call(
        functools.partial(_dec_score_kernel, past=past),
        grid=(nb,),
        in_specs=[pl.BlockSpec((None, 1, 4 * LANES), lambda b: (b, 0, 0)),
                  pl.BlockSpec((None, n_cmp, 2 * LANES), lambda b: (b, 0, 0)),
                  _const_spec(m.shape)],
        out_specs=[pl.BlockSpec((None, DEC_ROWS, LANES), lambda b: (b, 0, 0)),
                   _const_spec((nb, DEC_ROWS, LANES))],
        out_shape=[jax.ShapeDtypeStruct((nb, DEC_ROWS, LANES), F32), jax.ShapeDtypeStruct((nb, DEC_ROWS, LANES), jnp.int32)],
        scratch_shapes=[pltpu.VMEM((nb, DEC_ROWS, LANES), F32)],
        compiler_params=_cparams("arbitrary"),
        name="dec_score",
    )(q3, kcvc, m)


def _round_bf16(x):
    return x.astype(BF16).astype(F32)


def _dec_attn_kernel(pt_ref, idx_ref, q_ref, g_ref, oc_ref, kvn_ref, wn_ref, cache_ref, cw_ref,
                     rq_ref, rk_ref, rv_ref, rg_ref, st_ref, gl_ref, ng_ref,
                     nsa_o, ret_o, win_o, st_o, kbuf, sem):
    b = pl.program_id(0)
    nb = pl.num_programs(0)
    slot = b % 2

    def blk_copy(bb, sl, g, k):
        j = idx_ref[bb, g, k]
        return pltpu.make_async_copy(
            cache_ref.at[pt_ref[bb, j >> 1], pl.ds(2 * LANES, 2 * LANES), :],
            kbuf.at[sl, g, :, pl.ds(k * PAGE_SIZE, PAGE_SIZE)],
            sem.at[sl])

    def start_all(bb, sl):
        for g in range(N_KV):
            for k in range(N_DEC_SEL):
                blk_copy(bb, sl, g, k).start()

    @pl.when(b == 0)
    def _():
        start_all(0, 0)

    @pl.when(b + 1 < nb)
    def _():
        start_all(b + 1, 1 - slot)

    lane = lax.broadcasted_iota(jnp.int32, (1, LANES), 1)
    low = lane < HEAD_DIM
    qs, rows = _dec_stack_q(q_ref[...], low)
    qsb = qs.astype(BF16)
    g0_rows = rows < HPG

    cw = cw_ref[...]
    wb = cw.shape[1]
    wn = wn_ref[...]
    col = lax.broadcasted_iota(jnp.int32, (1, wb), 1)
    valid = jnp.broadcast_to((wb - col) < WINDOW, (DEC_ROWS, wb))
    s_w = jnp.where(valid, _dot(qsb, cw[0:LANES].astype(BF16)), NEG)
    s_wn = jnp.sum(qs * _round_bf16(wn[:, 0:LANES]), -1, keepdims=True)
    m_w = jnp.maximum(jnp.max(s_w, -1, keepdims=True), s_wn)
    p_w = jnp.exp(s_w - m_w) * valid.astype(F32)
    p_wn = jnp.exp(s_wn - m_w)
    l_w = jnp.maximum(jnp.sum(p_w, -1, keepdims=True) + p_wn, 1e-30)
    o_w = _dot_nt((p_w / l_w).astype(BF16), cw[LANES:2 * LANES].astype(BF16)) + _round_bf16(p_wn / l_w) * _round_bf16(wn[:, LANES:2 * LANES])
    new_cols = jnp.concatenate(
        [jnp.concatenate([jnp.transpose(jnp.broadcast_to(wn[:, u * LANES:(u + 1) * LANES], (LANES, LANES)))] * (wb // LANES), axis=1)
         for u in range(2)], axis=0)
    win_o[...] = jnp.where(col == wb - 1, new_cols, pltpu.roll(cw, wb - 1, 1))

    for g in range(N_KV):
        for k in range(N_DEC_SEL):
            blk_copy(b, slot, g, k).wait()
    kvn = kvn_ref[...]
    s_g = [_dot(qsb, kbuf[slot, g, 0:LANES, :].astype(BF16)) for g in range(N_KV)]
    pos = lax.broadcasted_iota(jnp.int32, (1, N_DEC_SEL * PAGE_SIZE), 1)
    picked_half = []
    for g in range(N_KV):
        half = jnp.zeros(pos.shape, jnp.int32)
        for k in range(N_DEC_SEL):
            half = jnp.where(pos >> 7 == k, idx_ref[b, g, k] & 1, half)
        picked_half.append(half)
    in_block = ((pos >> 6) & 1) == jnp.where(g0_rows, picked_half[0], picked_half[1])
    s_s = jnp.where(in_block, jnp.where(g0_rows, s_g[0], s_g[1]), NEG)
    s_sn = jnp.sum(qs * _round_bf16(kvn[:, 2 * LANES:3 * LANES]), -1, keepdims=True)
    m_sel = jnp.maximum(jnp.max(s_s, -1, keepdims=True), s_sn)
    p_s = jnp.exp(s_s - m_sel)
    p_sn = jnp.exp(s_sn - m_sel)
    l_sel = jnp.maximum(jnp.sum(p_s, -1, keepdims=True) + p_sn, 1e-30)
    pn = (p_s / l_sel).astype(BF16)
    o_g = [_dot_nt(pn, kbuf[slot, g, LANES:2 * LANES, :].astype(BF16)) for g in range(N_KV)]
    o_s = jnp.where(g0_rows, o_g[0], o_g[1]) + _round_bf16(p_sn / l_sel) * _round_bf16(kvn[:, 3 * LANES:4 * LANES])

    gates = g_ref[...]
    gcol = [jnp.zeros((DEC_ROWS, 1), F32) for _ in range(3)]
    for h in range(N_NSA_HEADS):
        for k in range(3):
            gcol[k] = jnp.where(rows == h, gates[:, 3 * h + k:3 * h + k + 1], gcol[k])
    o = gcol[0] * oc_ref[...] + gcol[1] * o_s + gcol[2] * o_w
    for c in range(HPG):
        nsa_o[:, c * LANES:(c + 1) * LANES] = jnp.where(low, o[c:c + 1], o[HPG + c:HPG + c + 1])

    for p in range(N_PAIR):
        sl = slice(p * LANES, (p + 1) * LANES)
        q, k, v, gl = rq_ref[:, sl], rk_ref[:, sl], rv_ref[:, sl], gl_ref[:, sl]
        state = st_ref[p]
        inner = _head_sum(q * k, low)

        def per_dk(row):
            t = jnp.transpose(jnp.broadcast_to(row, (LANES, LANES)))
            return jnp.where(low, t[0:RET_DK], t[RET_DK:2 * RET_DK])

        o_r = inner * v + jnp.sum(per_dk(q * gl) * state, axis=0, keepdims=True)
        st_o[p] = gl * state + per_dk(k) * v
        ret_o[:, sl] = _gated_group_norm(o_r, rg_ref[:, sl], ng_ref[:, sl], low)


def _dec_attn(page_table, idx, q3, gates3, oc, kvn3, wn3, cache4, cwin, rq3, rk3, rv3, rg3, state, gl, ng):
    nb = q3.shape[0]
    wb = cwin.shape[2]
    row3 = lambda w: pl.BlockSpec((None, 1, w), lambda b, *_: (b, 0, 0))
    st_spec = pl.BlockSpec((None, N_PAIR, RET_DK, LANES), lambda b, *_: (b, 0, 0, 0))
    win_spec = pl.BlockSpec((None, 2 * LANES, wb), lambda b, *_: (b, 0, 0))
    grid_spec = pltpu.PrefetchScalarGridSpec(
        num_scalar_prefetch=2,
        grid=(nb,),
        in_specs=[row3(4 * LANES), row3(LANES), pl.BlockSpec((None, DEC_ROWS, LANES), lambda b, *_: (b, 0, 0)),
                  row3(4 * LANES), row3(2 * LANES), pl.BlockSpec(memory_space=pl.ANY), win_spec,
                  row3(4 * LANES), row3(4 * LANES), row3(4 * LANES), row3(4 * LANES), st_spec,
                  pl.BlockSpec(gl.shape, lambda b, *_: (0, 0)), pl.BlockSpec(ng.shape, lambda b, *_: (0, 0))],
        out_specs=[row3(4 * LANES), row3(4 * LANES), win_spec, st_spec],
        scratch_shapes=[pltpu.VMEM((2, N_KV, 2 * LANES, N_DEC_SEL * PAGE_SIZE), F32), pltpu.SemaphoreType.DMA((2,))],
    )
    return pl.pallas_call(
        _dec_attn_kernel,
        grid_spec=grid_spec,
        out_shape=[jax.ShapeDtypeStruct((nb, 1, 4 * LANES), F32), jax.ShapeDtypeStruct((nb, 1, 4 * LANES), F32),
                   jax.ShapeDtypeStruct((nb, 2 * LANES, wb), F32), jax.ShapeDtypeStruct(state.shape, F32)],
        compiler_params=_cparams("arbitrary"),
        name="dec_attn",
    )(page_table, idx, q3, gates3, oc, kvn3, wn3, cache4, cwin, rq3, rk3, rv3, rg3, state, gl, ng)


def _permute_w_in(w):
    p0 = NSA_WIDTH
    p1 = p0 + 6 * KV_WIDTH
    p2 = p1 + 3 * N_NSA_HEADS
    wq = w[:, :p0].reshape(D_MODEL, N_NSA_HEADS, HEAD_DIM)[:, jnp.array(HEAD_PERM)].reshape(D_MODEL, p0)
    w_gates = jnp.pad(w[:, p1:p2], ((0, 0), (0, LANES - 3 * N_NSA_HEADS)))
    return jnp.concatenate([wq, w[:, p0:p1], w[:, p2:], w_gates], axis=1).astype(BF16)


def _permute_w_o(w):
    w_nsa = w[:NSA_WIDTH].reshape(N_NSA_HEADS, HEAD_DIM, D_MODEL)[jnp.array(HEAD_PERM)].reshape(NSA_WIDTH, D_MODEL)
    return jnp.concatenate([w_nsa, w[NSA_WIDTH:]], axis=0).astype(BF16)


def kernel(x_prompt, x_sample, cache_kv, cache_win, state_ret, page_table, w_in, w_cmp_k, w_cmp_v, pos_cmp_k, pos_cmp_v, ret_norm_g, w_o, ln1_g, ln1_b, w_up, w_down, ln2_g, ln2_b):
    depth = w_in.shape[0]
    assert depth == 1 and x_sample.shape[1] == 1
    batch, seq, _ = x_prompt.shape
    nb = x_sample.shape[0]
    n_pages = page_table.shape[1]
    past = n_pages * PAGE_SIZE
    assert past // SEL_BLOCK == N_SEL_BLOCKS and cache_win.shape[2] == WINDOW
    alpha = (2 * depth) ** 0.25
    log_gamma = jnp.log1p(-jnp.exp2(-5.0 - jnp.arange(N_RET_HEADS, dtype=F32)))

    w_perm = _permute_w_in(w_in[0])
    wo = _permute_w_o(w_o[0])
    wu, wd = w_up[0].astype(BF16), w_down[0].astype(BF16)
    vec = lambda a: a[0].reshape(1, -1)
    cmp_w = _compress_weights(w_cmp_k[0], w_cmp_v[0], pos_cmp_k[0], pos_cmp_v[0])
    tail_consts = (wo, vec(ln1_g), vec(ln1_b), wu, wd, vec(ln2_g), vec(ln2_b))
    norm_g = vec(ret_norm_g)

    tm = 512
    pos_p = jnp.arange(seq, dtype=jnp.int32)
    tabs_p = _rope_tables(pos_p, ROPE_THETA, ROPE_DIMS) + _rope_tables(pos_p, RET_THETA, RET_DK)
    x2d = x_prompt.reshape(batch * seq, D_MODEL)
    kv4t, wint, rq, rk, rv, rg, qt, gt, kaug, vst, kw, vwt = _project(x2d, w_perm, tabs_p, tm=tm, seq=seq, prompt=True)
    seq_pages = seq // PAGE_SIZE
    pidx_p = jnp.broadcast_to(jnp.arange(batch, dtype=jnp.int32)[:, None], (batch, seq_pages))
    poff_p = jnp.broadcast_to(jnp.arange(seq_pages, dtype=jnp.int32)[None, :] * PAGE_SIZE, (batch, seq_pages))
    kcvc_p = _compress(pidx_p, poff_p, kv4t, *cmp_w)
    o_nsa = _nsa_prompt(qt, gt, kcvc_p, kaug, vst, kw, vwt, _importance_matrix(seq // CMP_STRIDE).T.astype(BF16), batch=batch, seq=seq)
    ret, st_p = _retention_prompt(rq, rk, rv, rg, _retention_tables(log_gamma, RET_CHUNK), norm_g, batch=batch, seq=seq)
    y_p = _tail(x2d, o_nsa, ret, *tail_consts, tm=tm, alpha=alpha)

    y_prompt = y_p.reshape(batch, seq, D_MODEL)
    to_rows = lambda a, n_slot: a.reshape(a.shape[0], n_slot, N_KV, HEAD_DIM, a.shape[-1]).transpose(0, 4, 1, 2, 3)[None]
    kv_prompt = to_rows(kv4t, 4)
    wlen = min(WINDOW, seq)
    win_prompt = to_rows(wint[:, :, seq - wlen:], 2)
    st6 = st_p.reshape(batch, N_PAIR, 2, RET_DK, 2, RET_DV)
    ret_prompt = jnp.stack([st6[:, :, 0, :, 0, :], st6[:, :, 1, :, 1, :]], axis=2).reshape(1, batch, N_RET_HEADS, RET_DK, RET_DV)

    pos_s = jnp.full((nb,), past, dtype=jnp.int32)
    tabs_s = _rope_tables(pos_s, ROPE_THETA, ROPE_DIMS) + _rope_tables(pos_s, RET_THETA, RET_DK)
    xs2d = x_sample.reshape(nb, D_MODEL)
    kv4_d, win_d, rq_d, rk_d, rv_d, rg_d, q_d, gates_d = _project(xs2d, w_perm, tabs_s, tm=nb, seq=seq, prompt=False)
    cache_t = cache_kv[0].transpose(0, 2, 3, 4, 1).reshape(cache_kv.shape[1], 4 * KV_WIDTH, PAGE_SIZE)
    cwin_t = cache_win[0].transpose(0, 2, 3, 4, 1).reshape(nb, 2 * KV_WIDTH, WINDOW)
    kcvc_d = _compress(page_table, jnp.zeros_like(page_table), cache_t, *cmp_w)
    row3 = lambda a: a.reshape(nb, 1, a.shape[-1])
    q3 = row3(q_d)
    oc_d, idx_d = _dec_score(q3, kcvc_d, _importance_matrix(past // CMP_STRIDE).astype(BF16), past=past)
    idx = idx_d[:, :N_KV, :TOP_N]
    state = state_ret[0].astype(F32).reshape(nb, N_PAIR, 2, RET_DK, RET_DV).transpose(0, 1, 3, 2, 4).reshape(nb, N_PAIR, RET_DK, LANES)
    gl = jnp.repeat(jnp.exp(log_gamma), RET_DK)[None, :]
    nsa_d, ret_d, win_st, st_d = _dec_attn(page_table, idx, q3, row3(gates_d), oc_d, row3(kv4_d), row3(win_d), cache_t, cwin_t,
                                          row3(rq_d), row3(rk_d), row3(rv_d), row3(rg_d), state, gl, norm_g)
    y_s = _tail(xs2d, nsa_d.reshape(nb, NSA_WIDTH).astype(BF16), ret_d.reshape(nb, RET_WIDTH).astype(BF16), *tail_consts, tm=nb, alpha=alpha)

    y_sample = y_s.reshape(nb, 1, D_MODEL)
    kv_sample = kv4_d.reshape(1, nb, 1, 4, N_KV, HEAD_DIM)
    win_sample = to_rows(win_st, 2)
    ret_sample = st_d.reshape(nb, N_PAIR, RET_DK, 2, RET_DV).transpose(0, 1, 3, 2, 4).reshape(1, nb, N_RET_HEADS, RET_DK, RET_DV)
    return (y_prompt, y_sample, kv_prompt, kv_sample, win_prompt, win_sample, ret_prompt, ret_sample)
```

```python
import functools

import jax
import jax.numpy as jnp
from jax import lax
from jax.experimental import pallas as pl
from jax.experimental.pallas import tpu as pltpu

F32 = jnp.float32
BF16 = jnp.bfloat16

D_MODEL = 1024
HEAD_DIM = 64
N_NSA_HEADS = 8
N_KV = 2
HPG = N_NSA_HEADS // N_KV
N_RET_HEADS = 8
RET_DK = 64
RET_DV = 64
NSA_WIDTH = N_NSA_HEADS * HEAD_DIM
RET_WIDTH = N_RET_HEADS * RET_DV
KV_WIDTH = N_KV * HEAD_DIM
D_FF = 4 * D_MODEL
CMP_STRIDE = 16
CMP_BLOCK = 32
SEL_BLOCK = 64
TOP_N = 16
N_FORCED = 3
WINDOW = 512
RET_CHUNK = 128
RET_CHUNKS_PER_STEP = 4
PAGE_SIZE = 128
ROPE_THETA = 500000.0
ROPE_DIMS = HEAD_DIM // 4
RET_THETA = 10000.0
LN_EPS = 1e-5
FORCE_BONUS = 1e4
NEG = -1e30
BELOW_NEG = -3e38

LANES = 128
SUBLANES = 8
VMEM_LIMIT = 56 * 1024 * 1024

LOG2E = 1.4426950408889634
BF16_SUBLANES = 16
ONES_ROWS = BF16_SUBLANES
CHUNK_PITCH = 20
PAGES_PER_STEP = 4
Q_TILE = 128
ROW_TILE = 512
KEY_TILE = 1024
assert KEY_TILE % ROW_TILE == 0
N_PAIR = N_RET_HEADS // 2
N_SEL_BLOCKS = 128

U_Q, U_KV, U_RQ, U_RK, U_RV, U_RG, U_G, U_TOTAL = 0, 4, 10, 14, 18, 22, 26, 27
HEAD_PERM = (0, 4, 1, 5, 2, 6, 3, 7)


def _dot(a, b):
    return jnp.dot(a, b, preferred_element_type=F32)


def _dot_nt(a, b):
    return lax.dot_general(a, b, (((1,), (1,)), ((), ())), preferred_element_type=F32)


def _cparams(*sem):
    return pltpu.CompilerParams(dimension_semantics=sem, vmem_limit_bytes=VMEM_LIMIT)


def _const_spec(shape):
    nd = len(shape)
    return pl.BlockSpec(shape, lambda *_: (0,) * nd)


def _rope(u, c, s_fwd, s_bwd, half):
    return u * c + pltpu.roll(u, LANES - half, 1) * s_fwd + pltpu.roll(u, half, 1) * s_bwd


def _proj_kernel(x_ref, w_ref, cq_ref, sqf_ref, sqb_ref, cr_ref, srf_ref, srb_ref, *outs, tm, seq, prompt):
    if prompt:
        kv4t_o, wint_o, rq_o, rk_o, rv_o, rg_o, qt_o, gt_o, kaug_o, vst_o, kw_o, vwt_o = outs
    else:
        kv4_o, win_o, rq_o, rk_o, rv_o, rg_o, q_o, g_o = outs
    xb = x_ref[...].astype(BF16)

    def seg(u0, n):
        return _dot(xb, w_ref[:, u0 * LANES:(u0 + n) * LANES])

    def unit(z, u):
        return z[:, u * LANES:(u + 1) * LANES]

    cq, sqf, sqb = cq_ref[...], sqf_ref[...], sqb_ref[...]
    cr, srf, srb = cr_ref[...], srf_ref[...], srb_ref[...]
    nsa_half, ret_half = ROPE_DIMS // 2, RET_DK // 2

    zq = seg(U_Q, 4)
    q = jnp.concatenate([_rope(unit(zq, c), cq, sqf, sqb, nsa_half) for c in range(4)], axis=1) * (HEAD_DIM ** -0.5)
    gates = jax.nn.sigmoid(seg(U_G, 1))
    zkv = seg(U_KV, 6)
    k_slc = _rope(unit(zkv, 2), cq, sqf, sqb, nsa_half)
    k_win = _rope(unit(zkv, 4), cq, sqf, sqb, nsa_half)
    kv4 = jnp.concatenate([zkv[:, 0:2 * LANES], k_slc, unit(zkv, 3)], axis=1)
    win = jnp.concatenate([k_win, unit(zkv, 5)], axis=1)
    if prompt:
        kv4t = jnp.transpose(kv4)
        wint = jnp.transpose(win)
        kv4t_o[...] = kv4t
        wint_o[...] = wint
        qt_o[...] = jnp.transpose(q * LOG2E).astype(BF16)
        gt_o[...] = jnp.transpose(gates)
        row = pl.program_id(0) * tm + lax.broadcasted_iota(jnp.int32, (tm, 1), 0)
        blk = (row & (seq - 1)) >> 6
        onehot = lax.broadcasted_iota(jnp.int32, (tm, LANES), 1) == blk
        kaug_o[:, 0:LANES] = k_slc.astype(BF16)
        kaug_o[:, LANES:2 * LANES] = jnp.where(onehot, 1.0, 0.0).astype(BF16)
        vst_o[0:LANES, :] = kv4t[3 * LANES:4 * LANES].astype(BF16)
        vst_o[LANES:, :] = jnp.ones((ONES_ROWS, tm), BF16)
        kw_o[...] = k_win.astype(BF16)
        vwt = wint[LANES:2 * LANES].astype(BF16)
        for j in range(tm // LANES):
            vwt_o[j] = vwt[:, j * LANES:(j + 1) * LANES]
    else:
        kv4_o[...] = kv4
        win_o[...] = win
        q_o[...] = q.astype(BF16).astype(F32)
        g_o[...] = gates

    zrq = seg(U_RQ, 4)
    zrk = seg(U_RK, 4)
    for c in range(4):
        sl = slice(c * LANES, (c + 1) * LANES)
        rq_o[:, sl] = _rope(unit(zrq, c), cr, srf, srb, ret_half)
        rk_o[:, sl] = _rope(unit(zrk, c), cr, srf, srb, ret_half) * (RET_DK ** -0.5)
    rv_o[...] = seg(U_RV, 4)
    rg_o[...] = seg(U_RG, 4)


def _rope_tables(pos, theta, n_rot):
    half = n_rot // 2
    inv = theta ** (-jnp.arange(half, dtype=F32) / half)
    ang = pos.astype(F32)[:, None] * inv[None, :]
    cos, sin = jnp.cos(ang), jnp.sin(ang)
    n = pos.shape[0]
    rest = HEAD_DIM - n_rot
    c = jnp.concatenate([cos, cos, jnp.ones((n, rest), F32)], 1)
    s_fwd = jnp.concatenate([-sin, jnp.zeros((n, half + rest), F32)], 1)
    s_bwd = jnp.concatenate([jnp.zeros((n, half), F32), sin, jnp.zeros((n, rest), F32)], 1)
    rep = LANES // HEAD_DIM
    return tuple(jnp.tile(t, (1, rep)) for t in (c, s_fwd, s_bwd))


def _project(x2d, w_perm, tables, *, tm, seq, prompt):
    rows = x2d.shape[0]
    n_tab_tiles = tables[0].shape[0] // tm
    row_spec = lambda w: pl.BlockSpec((tm, w), lambda i: (i, 0))
    tab_spec = pl.BlockSpec((tm, LANES), lambda i: (i % n_tab_tiles, 0))
    if prompt:
        assert tm == ROW_TILE and seq % KEY_TILE == 0
        batch, tpb = rows // seq, seq // tm
        feat_spec = lambda f: pl.BlockSpec((None, f, tm), lambda i: (i // tpb, 0, i % tpb))
        outs = [(feat_spec(4 * LANES), (batch, 4 * LANES, seq), F32), (feat_spec(2 * LANES), (batch, 2 * LANES, seq), F32)]
    else:
        outs = [(row_spec(4 * LANES), (rows, 4 * LANES), F32), (row_spec(2 * LANES), (rows, 2 * LANES), F32)]
    outs += [(row_spec(4 * LANES), (rows, 4 * LANES), F32)] * 4
    if prompt:
        outs += [(feat_spec(4 * LANES), (batch, 4 * LANES, seq), BF16),
                 (feat_spec(LANES), (batch, LANES, seq), F32),
                 (row_spec(2 * LANES), (rows, 2 * LANES), BF16),
                 (pl.BlockSpec((None, LANES + ONES_ROWS, tm), lambda i: (i, 0, 0)), (rows // tm, LANES + ONES_ROWS, tm), BF16),
                 (row_spec(LANES), (rows, LANES), BF16),
                 (pl.BlockSpec((tm // LANES, LANES, LANES), lambda i: (i, 0, 0)), (rows // LANES, LANES, LANES), BF16)]
    else:
        outs += [(row_spec(4 * LANES), (rows, 4 * LANES), F32), (row_spec(LANES), (rows, LANES), F32)]
    return pl.pallas_call(
        functools.partial(_proj_kernel, tm=tm, seq=seq, prompt=prompt),
        grid=(rows // tm,),
        in_specs=[row_spec(D_MODEL), _const_spec(w_perm.shape)] + [tab_spec] * 6,
        out_specs=[s for s, _, _ in outs],
        out_shape=[jax.ShapeDtypeStruct(shape, dt) for _, shape, dt in outs],
        compiler_params=_cparams("parallel"),
        name="proj_prompt" if prompt else "proj_decode",
    )(x2d, w_perm, *tables)


def _cmp_kernel(pn_ref, po_ref, src_ref, wlo_ref, whi_ref, plo_ref, phi_ref, out_ref, stage, slab, sem, *, n_pages):
    b = pl.program_id(0)
    nb = pl.num_programs(0)
    slot = b % 2

    def page_copy(bb, sl, p, kv):
        return pltpu.make_async_copy(
            src_ref.at[pn_ref[bb, p], pl.ds(kv * LANES, LANES), pl.ds(pl.multiple_of(po_ref[bb, p], PAGE_SIZE), PAGE_SIZE)],
            stage.at[sl, kv, p],
            sem.at[sl])

    def start_all(bb, sl):
        for p in range(n_pages):
            for kv in range(2):
                page_copy(bb, sl, p, kv).start()

    @pl.when(b == 0)
    def _():
        start_all(0, 0)

    @pl.when(b + 1 < nb)
    def _():
        start_all(b + 1, 1 - slot)

    for p in range(n_pages):
        for kv in range(2):
            page_copy(b, slot, p, kv).wait()

    chunks_per_page = PAGE_SIZE // CMP_STRIDE
    page_rows = chunks_per_page * CHUNK_PITCH

    def to_rows(pp, carry):
        for u in range(PAGES_PER_STEP):
            p = pp * PAGES_PER_STEP + u
            r0 = pl.multiple_of(p * page_rows, SUBLANES)
            for kv in range(2):
                x = jnp.transpose(stage[slot, kv, p])
                for c in range(chunks_per_page):
                    slab[kv, pl.ds(r0 + c * CHUNK_PITCH, CMP_STRIDE), :] = x[c * CMP_STRIDE:(c + 1) * CMP_STRIDE]
        return carry

    lax.fori_loop(0, n_pages // PAGES_PER_STEP, to_rows, 0)

    n_chunks = n_pages * chunks_per_page
    acc_lo = jnp.zeros((n_chunks, 2 * LANES), F32)
    acc_hi = jnp.zeros((n_chunks, 2 * LANES), F32)
    for r in range(CMP_STRIDE):
        xr = jnp.concatenate([slab[kv, pl.ds(r, n_chunks, stride=CHUNK_PITCH), :] for kv in range(2)], axis=1)
        acc_lo = acc_lo + _dot((xr + plo_ref[r:r + 1, :]).astype(BF16), wlo_ref[r])
        acc_hi = acc_hi + _dot((xr + phi_ref[r:r + 1, :]).astype(BF16), whi_ref[r])
    nxt = pltpu.roll(acc_hi, n_chunks - 1, 0)
    last = lax.broadcasted_iota(jnp.int32, (n_chunks, 1), 0) == n_chunks - 1
    out_ref[...] = jnp.where(last, 0.0, acc_lo + nxt)


def _compress(page_index, page_offset, src, wlo, whi, plo, phi):
    nb, n_pages = page_index.shape
    n_chunks = n_pages * PAGE_SIZE // CMP_STRIDE
    grid_spec = pltpu.PrefetchScalarGridSpec(
        num_scalar_prefetch=2,
        grid=(nb,),
        in_specs=[pl.BlockSpec(memory_space=pl.ANY),
                  pl.BlockSpec(wlo.shape, lambda b, *_: (0, 0, 0)),
                  pl.BlockSpec(whi.shape, lambda b, *_: (0, 0, 0)),
                  pl.BlockSpec(plo.shape, lambda b, *_: (0, 0)),
                  pl.BlockSpec(phi.shape, lambda b, *_: (0, 0))],
        out_specs=pl.BlockSpec((None, n_chunks, 2 * LANES), lambda b, *_: (b, 0, 0)),
        scratch_shapes=[pltpu.VMEM((2, 2, n_pages, LANES, PAGE_SIZE), F32),
                        pltpu.VMEM((2, n_chunks * CHUNK_PITCH, LANES), F32),
                        pltpu.SemaphoreType.DMA((2,))],
    )
    return pl.pallas_call(
        functools.partial(_cmp_kernel, n_pages=n_pages),
        grid_spec=grid_spec,
        out_shape=jax.ShapeDtypeStruct((nb, n_chunks, 2 * LANES), F32),
        compiler_params=_cparams("arbitrary"),
        name="compress",
    )(page_index, page_offset, src, wlo, whi, plo, phi)


def _compress_weights(w_k, w_v, pos_k, pos_v):
    def bd(wk_r, wv_r):
        z = jnp.zeros((HEAD_DIM, HEAD_DIM), F32)
        rows = [[wk_r, z, z, z], [z, wk_r, z, z], [z, z, wv_r, z], [z, z, z, wv_r]]
        return jnp.block(rows)

    w_all = jax.vmap(bd)(w_k, w_v).astype(BF16)
    p_all = jnp.concatenate([pos_k, pos_k, pos_v, pos_v], axis=1)
    return w_all[:CMP_STRIDE], w_all[CMP_STRIDE:], p_all[:CMP_STRIDE], p_all[CMP_STRIDE:]


def _masked_softmax(s, mask):
    s = jnp.where(mask, s, NEG)
    m = jnp.max(s, -1, keepdims=True)
    p = jnp.exp(s - m) * mask.astype(F32)
    return p / jnp.maximum(jnp.sum(p, -1, keepdims=True), 1e-30)


def _split3(x):
    hi = x.astype(BF16)
    r1 = x - hi.astype(F32)
    mid = r1.astype(BF16)
    lo = (r1 - mid.astype(F32)).astype(BF16)
    return hi, mid, lo


def _importance_matrix(n_cmp):
    n = jnp.arange(n_cmp)[:, None]
    j = jnp.arange(N_SEL_BLOCKS)[None, :]
    inner = (n >= 4 * j) & (n <= 4 * j + 2)
    edge = (n == 4 * j - 1) | (n == 4 * j + 3)
    return jnp.where(inner, 2.0, jnp.where(edge, 1.0, 0.0)).astype(F32)


def _softmax_over_keys(s, mask):
    s = jnp.where(mask, s, NEG)
    m = jnp.max(s, 0, keepdims=True)
    p = jnp.exp2(s - m)
    inv = jnp.where(m > 0.5 * NEG, 1.0 / jnp.maximum(jnp.sum(p, 0, keepdims=True), 1e-30), 0.0)
    return p * inv


def _nsa_kernel(qt_ref, gt_ref, kcvc_ref, kaug_ref, vst_ref, kw_ref, vwt_ref, mt_ref, out_ref,
                kc_s, vct_s, m_s, acc_s, s_buf, tmax_s):
    i = pl.program_id(1)
    t0 = i * Q_TILE
    n_cmp = kcvc_ref.shape[0]

    @pl.when(i == 0)
    def _():
        kcvc = kcvc_ref[...]
        kc_s[...] = kcvc[:, 0:LANES].astype(BF16)
        vct_s[...] = jnp.transpose(kcvc[:, LANES:2 * LANES]).astype(BF16)

    lane = lax.broadcasted_iota(jnp.int32, (1, Q_TILE), 1)
    tq = jnp.concatenate([t0 + lane] * N_NSA_HEADS, axis=1)
    low_rows = lax.broadcasted_iota(jnp.int32, (LANES, 1), 0) < HEAD_DIM
    qt = qt_ref[...]
    zero = jnp.zeros((), BF16)
    q_all = jnp.concatenate([jnp.where(low_rows if g == 0 else ~low_rows, qt[c * LANES:(c + 1) * LANES], zero)
                             for g in range(N_KV) for c in range(HPG)], axis=1)

    cend = lax.broadcasted_iota(jnp.int32, (n_cmp, 1), 0) * CMP_STRIDE + (CMP_BLOCK - 1)
    p_c = _softmax_over_keys(_dot(kc_s[...], q_all), cend <= tq)
    o_c = _dot(vct_s[...], p_c.astype(BF16))

    w0 = pl.multiple_of(jnp.maximum(t0 - WINDOW, 0), Q_TILE)
    n_win = WINDOW + Q_TILE
    dist = tq - (w0 + lax.broadcasted_iota(jnp.int32, (n_win, 1), 0))
    p_w = _softmax_over_keys(_dot(kw_ref[pl.ds(w0, n_win), :], q_all), (dist >= 0) & (dist < WINDOW))
    w_blk = w0 // LANES
    vwt = jnp.concatenate([vwt_ref[w_blk + j] for j in range(n_win // LANES)], axis=1)
    o_w = _dot(vwt, p_w.astype(BF16))
    gt = gt_ref[...]
    gk = [jnp.concatenate([gt[3 * h + k:3 * h + k + 1] for h in range(N_NSA_HEADS)], axis=1) for k in range(3)]
    o_cw = gk[0] * o_c + gk[2] * o_w

    psum = jnp.concatenate([sum(p_c[:, (g * HPG + c) * Q_TILE:(g * HPG + c + 1) * Q_TILE] for c in range(HPG))
                            for g in range(N_KV)], axis=1)
    mt = mt_ref[...]
    imp_t = sum(_dot(mt, part) for part in _split3(psum))
    blk = lax.broadcasted_iota(jnp.int32, (N_SEL_BLOCKS, 1), 0)
    blk_f = blk.astype(F32)
    cur = jnp.concatenate([(t0 + lane) >> 6] * N_KV, axis=1)
    forced = (blk == 0) | (blk == cur) | (blk == cur - 1)
    score = jnp.where(forced, BELOW_NEG, jnp.where(blk <= cur, imp_t, NEG))
    bias = jnp.where(forced, 0.0, NEG)
    for _ in range(TOP_N - N_FORCED):
        best = jnp.max(score, axis=0, keepdims=True)
        first = jnp.min(jnp.where(score == best, blk_f, float(N_SEL_BLOCKS)), axis=0, keepdims=True)
        pick = blk_f == first
        bias = jnp.where(pick, 0.0, bias)
        score = jnp.where(pick, BELOW_NEG, score)
    bias = bias.astype(BF16)
    bias_all = jnp.concatenate([bias[:, g * Q_TILE:(g + 1) * Q_TILE] for g in range(N_KV) for _ in range(HPG)], axis=1)
    q_aug = jnp.concatenate([q_all, bias_all], axis=0)

    m_s[...] = jnp.full(m_s.shape, NEG, F32)
    acc_s[...] = jnp.zeros(acc_s.shape, F32)

    def scores(kt, causal):
        k0 = pl.multiple_of(kt * KEY_TILE, KEY_TILE)
        s = _dot(kaug_ref[pl.ds(k0, KEY_TILE), :], q_aug)
        if causal:
            kpos = k0 + lax.broadcasted_iota(jnp.int32, (KEY_TILE, 1), 0)
            s = jnp.where(kpos <= tq, s, NEG)
        s_buf[...] = s
        tmax_s[...] = jnp.max(s, 0, keepdims=True)

    def accumulate(kt):
        m_old = m_s[...]
        m_new = jnp.maximum(m_old, tmax_s[...])
        alpha = jnp.exp2(m_old - m_new)
        p = jnp.exp2(s_buf[...] - m_new).astype(BF16)
        n_sub = KEY_TILE // ROW_TILE
        pv = sum(_dot(vst_ref[kt * n_sub + j], p[j * ROW_TILE:(j + 1) * ROW_TILE]) for j in range(n_sub))
        acc_s[...] = alpha * acc_s[...] + pv
        m_s[...] = m_new

    last = (t0 + Q_TILE - 1) // KEY_TILE

    @pl.when(last == 0)
    def _():
        scores(0, True)

    @pl.when(last > 0)
    def _():
        scores(0, False)

        def body(kt, carry):
            accumulate(kt)
            scores(kt + 1, False)
            return carry

        lax.fori_loop(0, last - 1, body, 0)
        accumulate(last - 1)
        scores(last, True)

    accumulate(last)
    o_s = acc_s[0:LANES, :] * (1.0 / acc_s[LANES:LANES + 1, :])

    o = o_cw + gk[1] * o_s
    half = HPG * Q_TILE
    comb = jnp.where(low_rows, o[:, 0:half], o[:, half:2 * half])
    for c in range(HPG):
        out_ref[:, c * LANES:(c + 1) * LANES] = jnp.transpose(comb[:, c * Q_TILE:(c + 1) * Q_TILE]).astype(BF16)


def _nsa_prompt(qt, gt, kcvc, kaug, vst, kw, vwt, mt, *, batch, seq):
    nq = seq // Q_TILE
    n_cols = N_NSA_HEADS * Q_TILE
    n_cmp = kcvc.shape[1]
    col_spec = lambda f: pl.BlockSpec((None, f, Q_TILE), lambda b, i: (b, 0, i))
    seq_spec = lambda w: pl.BlockSpec((seq, w), lambda b, i: (b, 0))
    return pl.pallas_call(
        _nsa_kernel,
        grid=(batch, nq),
        in_specs=[col_spec(4 * LANES), col_spec(LANES),
                  pl.BlockSpec((None, n_cmp, 2 * LANES), lambda b, i: (b, 0, 0)),
                  seq_spec(2 * LANES),
                  pl.BlockSpec((seq // ROW_TILE, LANES + ONES_ROWS, ROW_TILE), lambda b, i: (b, 0, 0)),
                  seq_spec(LANES),
                  pl.BlockSpec((seq // LANES, LANES, LANES), lambda b, i: (b, 0, 0)),
                  _const_spec(mt.shape)],
        out_specs=pl.BlockSpec((Q_TILE, 4 * LANES), lambda b, i: (b * nq + i, 0)),
        out_shape=jax.ShapeDtypeStruct((batch * seq, 4 * LANES), BF16),
        scratch_shapes=[pltpu.VMEM((n_cmp, LANES), BF16), pltpu.VMEM((LANES, n_cmp), BF16),
                        pltpu.VMEM((1, n_cols), F32), pltpu.VMEM((LANES + ONES_ROWS, n_cols), F32),
                        pltpu.VMEM((KEY_TILE, n_cols), F32), pltpu.VMEM((1, n_cols), F32)],
        compiler_params=_cparams("parallel", "arbitrary"),
        name="nsa_prompt",
    )(qt, gt, kcvc, kaug, vst, kw, vwt, mt)


def _head_sum(a, low):
    s0 = jnp.sum(jnp.where(low, a, 0.0), -1, keepdims=True)
    s1 = jnp.sum(jnp.where(low, 0.0, a), -1, keepdims=True)
    return jnp.where(low, s0, s1)


def _gated_group_norm(o, rg, norm_g, low):
    mu = _head_sum(o, low) * (1.0 / RET_DV)
    d = o - mu
    var = _head_sum(d * d, low) * (1.0 / RET_DV)
    gn = d * lax.rsqrt(var + LN_EPS) * norm_g
    return (rg * jax.nn.sigmoid(rg)) * gn


def _ret_kernel(rq_ref, rk_ref, rv_ref, rg_ref, decay_ref, qdec_ref, kdec_ref, gcl_ref, bd_ref, ng_ref,
                ret_o, st_o, st_s):
    c = pl.program_id(1)

    @pl.when(c == 0)
    def _():
        st_s[...] = jnp.zeros(st_s.shape, F32)

    low = lax.broadcasted_iota(jnp.int32, (1, LANES), 1) < RET_DK
    zero = jnp.zeros((), BF16)
    for j in range(RET_CHUNKS_PER_STEP):
        rows = slice(j * RET_CHUNK, (j + 1) * RET_CHUNK)
        for p in range(N_PAIR):
            sl = slice(p * LANES, (p + 1) * LANES)
            q, k, v = rq_ref[rows, sl], rk_ref[rows, sl], rv_ref[rows, sl]
            qb, kb, vb = q.astype(BF16), k.astype(BF16), v.astype(BF16)
            in0 = _dot_nt(jnp.where(low, qb, zero), kb) * decay_ref[2 * p]
            in1 = _dot_nt(jnp.where(low, zero, qb), kb) * decay_ref[2 * p + 1]
            inner = jnp.concatenate([in0.astype(BF16), in1.astype(BF16)], axis=1)
            v_split = jnp.concatenate([jnp.where(low, vb, zero), jnp.where(low, zero, vb)], axis=0)
            state = st_s[p]
            o = _dot(inner, v_split) + _dot((q * qdec_ref[:, sl]).astype(BF16), state.astype(BF16))
            kd_t = jnp.transpose(k * kdec_ref[:, sl]).astype(BF16)
            st_s[p] = state * gcl_ref[:, sl] + _dot(kd_t, vb) * bd_ref[...]
            ret_o[rows, sl] = _gated_group_norm(o, rg_ref[rows, sl], ng_ref[:, sl], low).astype(BF16)

    @pl.when(c == pl.num_programs(1) - 1)
    def _():
        st_o[...] = st_s[...]


def _retention_tables(log_gamma, chunk):
    i = jnp.arange(chunk, dtype=F32)
    diff = i[:, None] - i[None, :]
    decay = jnp.where(diff >= 0, jnp.exp(log_gamma[:, None, None] * jnp.maximum(diff, 0.0)), 0.0)
    per_lane = lambda t: jnp.repeat(t, RET_DK, axis=-1)
    qdec = per_lane(jnp.exp((i + 1.0)[:, None] * log_gamma[None, :]))
    kdec = per_lane(jnp.exp((chunk - 1.0 - i)[:, None] * log_gamma[None, :]))
    gcl = per_lane(jnp.exp(chunk * log_gamma)[None, :])
    r = jnp.arange(LANES)
    bd = ((r[:, None] // RET_DK) == (r[None, :] // RET_DV)).astype(F32)
    return decay, qdec, kdec, gcl, bd


def _retention_prompt(rq, rk, rv, rg, tables, norm_g, *, batch, seq):
    decay, qdec, kdec, gcl, bd = tables
    step_rows = RET_CHUNK * RET_CHUNKS_PER_STEP
    nc = seq // step_rows
    row_spec = pl.BlockSpec((step_rows, RET_WIDTH), lambda b, c: (b * nc + c, 0))
    return pl.pallas_call(
        _ret_kernel,
        grid=(batch, nc),
        in_specs=[row_spec] * 4 + [_const_spec(decay.shape), _const_spec(qdec.shape), _const_spec(kdec.shape),
                                   _const_spec(gcl.shape), _const_spec(bd.shape), _const_spec(norm_g.shape)],
        out_specs=[row_spec, pl.BlockSpec((None, N_PAIR, LANES, LANES), lambda b, c: (b, 0, 0, 0))],
        out_shape=[jax.ShapeDtypeStruct((batch * seq, RET_WIDTH), BF16),
                   jax.ShapeDtypeStruct((batch, N_PAIR, LANES, LANES), F32)],
        scratch_shapes=[pltpu.VMEM((N_PAIR, LANES, LANES), F32)],
        compiler_params=_cparams("parallel", "arbitrary"),
        name="retention",
    )(rq, rk, rv, rg, decay, qdec, kdec, gcl, bd, norm_g)


def _layer_norm(v, g, b):
    mu = jnp.mean(v, -1, keepdims=True)
    d = v - mu
    var = jnp.mean(d * d, -1, keepdims=True)
    return d * lax.rsqrt(var + LN_EPS) * g + b


def _tail_kernel(x_ref, nsa_ref, ret_ref, wo_ref, g1_ref, b1_ref, wu_ref, wd_ref, g2_ref, b2_ref, y_ref, *, alpha):
    mix = _dot(nsa_ref[...], wo_ref[0:NSA_WIDTH, :]) + _dot(ret_ref[...], wo_ref[NSA_WIDTH:, :])
    x1 = _layer_norm(alpha * x_ref[...] + mix, g1_ref[...], b1_ref[...])
    x1b = x1.astype(BF16)
    f = jnp.zeros(x1.shape, F32)
    for j in range(D_FF // D_MODEL):
        sl = slice(j * D_MODEL, (j + 1) * D_MODEL)
        h = jnp.maximum(_dot(x1b, wu_ref[:, sl]), 0.0)
        f = f + _dot((h * h).astype(BF16), wd_ref[sl, :])
    y_ref[...] = _layer_norm(alpha * x1 + f, g2_ref[...], b2_ref[...])


def _tail(x2d, nsa, ret, wo, g1, b1, wu, wd, g2, b2, *, tm, alpha):
    rows = x2d.shape[0]
    row_spec = lambda w: pl.BlockSpec((tm, w), lambda i: (i, 0))
    consts = (wo, g1, b1, wu, wd, g2, b2)
    return pl.pallas_call(
        functools.partial(_tail_kernel, alpha=alpha),
        grid=(rows // tm,),
        in_specs=[row_spec(D_MODEL), row_spec(NSA_WIDTH), row_spec(RET_WIDTH)] + [_const_spec(c.shape) for c in consts],
        out_specs=row_spec(D_MODEL),
        out_shape=jax.ShapeDtypeStruct((rows, D_MODEL), F32),
        compiler_params=_cparams("parallel"),
        name="tail",
    )(x2d, nsa, ret, *consts)


DEC_ROWS = 16
N_DEC_SEL = TOP_N - 1


def _dec_stack_q(q, low):
    rows = lax.broadcasted_iota(jnp.int32, (DEC_ROWS, 1), 0)
    qs = jnp.zeros((DEC_ROWS, LANES), F32)
    for g in range(N_KV):
        for c in range(HPG):
            piece = jnp.where(low if g == 0 else ~low, q[:, c * LANES:(c + 1) * LANES], 0.0)
            qs = jnp.where(rows == g * HPG + c, piece, qs)
    return qs, rows


def _dec_score_kernel(q_ref, kcvc_ref, m_ref, oc_o, idx_o, score_s, *, past):
    lane = lax.broadcasted_iota(jnp.int32, (1, LANES), 1)
    low = lane < HEAD_DIM
    qs, rows = _dec_stack_q(q_ref[...], low)
    qsb = qs.astype(BF16)
    kcvc = kcvc_ref[...]
    kc = kcvc[:, 0:LANES].astype(BF16)
    vc = kcvc[:, LANES:2 * LANES].astype(BF16)
    n_cmp = kcvc.shape[0]
    cend = lax.broadcasted_iota(jnp.int32, (1, n_cmp), 1) * CMP_STRIDE + (CMP_BLOCK - 1)
    p = _masked_softmax(_dot_nt(qsb, kc), jnp.broadcast_to(cend <= past, (DEC_ROWS, n_cmp)))
    oc_o[...] = _dot(p.astype(BF16), vc)

    psum = jnp.zeros((DEC_ROWS, n_cmp), F32)
    for g in range(N_KV):
        acc = p[g * HPG:g * HPG + 1]
        for c in range(1, HPG):
            acc = acc + p[g * HPG + c:g * HPG + c + 1]
        psum = jnp.where(rows == g, acc, psum)
    m = m_ref[...]
    imp = sum(_dot(part, m) for part in _split3(psum))
    forced = (lane == 0) | (lane == N_SEL_BLOCKS - 1)
    b = pl.program_id(0)
    score_s[b] = imp + FORCE_BONUS * forced.astype(F32)

    @pl.when(b == pl.num_programs(0) - 1)
    def _():
        lane_f = lane.astype(F32)
        score = score_s[...].reshape(-1, LANES)
        picked = jnp.zeros(score.shape, F32)
        for r in range(N_DEC_SEL):
            best = jnp.max(score, -1, keepdims=True)
            first = jnp.min(jnp.where(score == best, lane_f, float(LANES)), -1, keepdims=True)
            picked = jnp.where(lane == r, first, picked)
            score = jnp.where(lane_f == first, BELOW_NEG, score)
        idx_o[...] = picked.astype(jnp.int32).reshape(idx_o.shape)


def _dec_score(q3, kcvc, m, *, past):
    nb = q3.shape[0]
    n_cmp = kcvc.shape[1]
    return pl.pallas_call(
        functools.partial(_dec_score_kernel, past=past),
        grid=(nb,),
        in_specs=[pl.BlockSpec((None, 1, 4 * LANES), lambda b: (b, 0, 0)),
                  pl.BlockSpec((None, n_cmp, 2 * LANES), lambda b: (b, 0, 0)),
                  _const_spec(m.shape)],
        out_specs=[pl.BlockSpec((None, DEC_ROWS, LANES), lambda b: (b, 0, 0)),
                   _const_spec((nb, DEC_ROWS, LANES))],
        out_shape=[jax.ShapeDtypeStruct((nb, DEC_ROWS, LANES), F32), jax.ShapeDtypeStruct((nb, DEC_ROWS, LANES), jnp.int32)],
        scratch_shapes=[pltpu.VMEM((nb, DEC_ROWS, LANES), F32)],
        compiler_params=_cparams("arbitrary"),
        name="dec_score",
    )(q3, kcvc, m)


def _round_bf16(x):
    return x.astype(BF16).astype(F32)


def _dec_attn_kernel(pt_ref, idx_ref, q_ref, g_ref, oc_ref, kvn_ref, wn_ref, cache_ref, cw_ref,
                     rq_ref, rk_ref, rv_ref, rg_ref, st_ref, gl_ref, ng_ref,
                     nsa_o, ret_o, win_o, st_o, kbuf, sem):
    b = pl.program_id(0)
    nb = pl.num_programs(0)
    slot = b % 2

    def blk_copy(bb, sl, g, k):
        j = idx_ref[bb, g, k]
        return pltpu.make_async_copy(
            cache_ref.at[pt_ref[bb, j >> 1], pl.ds(2 * LANES, 2 * LANES), :],
            kbuf.at[sl, g, :, pl.ds(k * PAGE_SIZE, PAGE_SIZE)],
            sem.at[sl])

    def start_all(bb, sl):
        for g in range(N_KV):
            for k in range(N_DEC_SEL):
                blk_copy(bb, sl, g, k).start()

    @pl.when(b == 0)
    def _():
        start_all(0, 0)

    @pl.when(b + 1 < nb)
    def _():
        start_all(b + 1, 1 - slot)

    lane = lax.broadcasted_iota(jnp.int32, (1, LANES), 1)
    low = lane < HEAD_DIM
    qs, rows = _dec_stack_q(q_ref[...], low)
    qsb = qs.astype(BF16)
    g0_rows = rows < HPG

    cw = cw_ref[...]
    wb = cw.shape[1]
    wn = wn_ref[...]
    col = lax.broadcasted_iota(jnp.int32, (1, wb), 1)
    valid = jnp.broadcast_to((wb - col) < WINDOW, (DEC_ROWS, wb))
    s_w = jnp.where(valid, _dot(qsb, cw[0:LANES].astype(BF16)), NEG)
    s_wn = jnp.sum(qs * _round_bf16(wn[:, 0:LANES]), -1, keepdims=True)
    m_w = jnp.maximum(jnp.max(s_w, -1, keepdims=True), s_wn)
    p_w = jnp.exp(s_w - m_w) * valid.astype(F32)
    p_wn = jnp.exp(s_wn - m_w)
    l_w = jnp.maximum(jnp.sum(p_w, -1, keepdims=True) + p_wn, 1e-30)
    o_w = _dot_nt((p_w / l_w).astype(BF16), cw[LANES:2 * LANES].astype(BF16)) + _round_bf16(p_wn / l_w) * _round_bf16(wn[:, LANES:2 * LANES])
    new_cols = jnp.concatenate(
        [jnp.concatenate([jnp.transpose(jnp.broadcast_to(wn[:, u * LANES:(u + 1) * LANES], (LANES, LANES)))] * (wb // LANES), axis=1)
         for u in range(2)], axis=0)
    win_o[...] = jnp.where(col == wb - 1, new_cols, pltpu.roll(cw, wb - 1, 1))

    for g in range(N_KV):
        for k in range(N_DEC_SEL):
            blk_copy(b, slot, g, k).wait()
    kvn = kvn_ref[...]
    s_g = [_dot(qsb, kbuf[slot, g, 0:LANES, :].astype(BF16)) for g in range(N_KV)]
    pos = lax.broadcasted_iota(jnp.int32, (1, N_DEC_SEL * PAGE_SIZE), 1)
    picked_half = []
    for g in range(N_KV):
        half = jnp.zeros(pos.shape, jnp.int32)
        for k in range(N_DEC_SEL):
            half = jnp.where(pos >> 7 == k, idx_ref[b, g, k] & 1, half)
        picked_half.append(half)
    in_block = ((pos >> 6) & 1) == jnp.where(g0_rows, picked_half[0], picked_half[1])
    s_s = jnp.where(in_block, jnp.where(g0_rows, s_g[0], s_g[1]), NEG)
    s_sn = jnp.sum(qs * _round_bf16(kvn[:, 2 * LANES:3 * LANES]), -1, keepdims=True)
    m_sel = jnp.maximum(jnp.max(s_s, -1, keepdims=True), s_sn)
    p_s = jnp.exp(s_s - m_sel)
    p_sn = jnp.exp(s_sn - m_sel)
    l_sel = jnp.maximum(jnp.sum(p_s, -1, keepdims=True) + p_sn, 1e-30)
    pn = (p_s / l_sel).astype(BF16)
    o_g = [_dot_nt(pn, kbuf[slot, g, LANES:2 * LANES, :].astype(BF16)) for g in range(N_KV)]
    o_s = jnp.where(g0_rows, o_g[0], o_g[1]) + _round_bf16(p_sn / l_sel) * _round_bf16(kvn[:, 3 * LANES:4 * LANES])

    gates = g_ref[...]
    gcol = [jnp.zeros((DEC_ROWS, 1), F32) for _ in range(3)]
    for h in range(N_NSA_HEADS):
        for k in range(3):
            gcol[k] = jnp.where(rows == h, gates[:, 3 * h + k:3 * h + k + 1], gcol[k])
    o = gcol[0] * oc_ref[...] + gcol[1] * o_s + gcol[2] * o_w
    for c in range(HPG):
        nsa_o[:, c * LANES:(c + 1) * LANES] = jnp.where(low, o[c:c + 1], o[HPG + c:HPG + c + 1])

    top = lax.broadcasted_iota(jnp.int32, (LANES, 1), 0) < RET_DK
    for p in range(N_PAIR):
        sl = slice(p * LANES, (p + 1) * LANES)
        q, k, v, gl = rq_ref[:, sl], rk_ref[:, sl], rv_ref[:, sl], gl_ref[:, sl]
        state = st_ref[p]
        inner = _head_sum(q * k, low)

        def per_row(row):
            return jnp.transpose(jnp.broadcast_to(row, (LANES, LANES)))[:, 0:RET_DV]

        v_rows = jnp.where(top, v[:, 0:RET_DV], v[:, RET_DV:2 * RET_DV])
        qs_state = per_row(q * gl) * state
        cross = jnp.concatenate([jnp.sum(qs_state[0:RET_DK], axis=0, keepdims=True),
                                 jnp.sum(qs_state[RET_DK:2 * RET_DK], axis=0, keepdims=True)], axis=1)
        o_r = inner * v + cross
        st_o[p] = per_row(gl) * state + per_row(k) * v_rows
        ret_o[:, sl] = _gated_group_norm(o_r, rg_ref[:, sl], ng_ref[:, sl], low)


def _dec_attn(page_table, idx, q3, gates3, oc, kvn3, wn3, cache4, cwin, rq3, rk3, rv3, rg3, state, gl, ng):
    nb = q3.shape[0]
    wb = cwin.shape[2]
    row3 = lambda w: pl.BlockSpec((None, 1, w), lambda b, *_: (b, 0, 0))
    st_spec = pl.BlockSpec((None, N_PAIR, 2 * RET_DK, RET_DV), lambda b, *_: (b, 0, 0, 0))
    win_spec = pl.BlockSpec((None, 2 * LANES, wb), lambda b, *_: (b, 0, 0))
    grid_spec = pltpu.PrefetchScalarGridSpec(
        num_scalar_prefetch=2,
        grid=(nb,),
        in_specs=[row3(4 * LANES), row3(LANES), pl.BlockSpec((None, DEC_ROWS, LANES), lambda b, *_: (b, 0, 0)),
                  row3(4 * LANES), row3(2 * LANES), pl.BlockSpec(memory_space=pl.ANY), win_spec,
                  row3(4 * LANES), row3(4 * LANES), row3(4 * LANES), row3(4 * LANES), st_spec,
                  pl.BlockSpec(gl.shape, lambda b, *_: (0, 0)), pl.BlockSpec(ng.shape, lambda b, *_: (0, 0))],
        out_specs=[row3(4 * LANES), row3(4 * LANES), win_spec, st_spec],
        scratch_shapes=[pltpu.VMEM((2, N_KV, 2 * LANES, N_DEC_SEL * PAGE_SIZE), F32), pltpu.SemaphoreType.DMA((2,))],
    )
    return pl.pallas_call(
        _dec_attn_kernel,
        grid_spec=grid_spec,
        out_shape=[jax.ShapeDtypeStruct((nb, 1, 4 * LANES), F32), jax.ShapeDtypeStruct((nb, 1, 4 * LANES), F32),
                   jax.ShapeDtypeStruct((nb, 2 * LANES, wb), F32), jax.ShapeDtypeStruct(state.shape, F32)],
        compiler_params=_cparams("arbitrary"),
        name="dec_attn",
    )(page_table, idx, q3, gates3, oc, kvn3, wn3, cache4, cwin, rq3, rk3, rv3, rg3, state, gl, ng)


def _permute_w_in(w):
    p0 = NSA_WIDTH
    p1 = p0 + 6 * KV_WIDTH
    p2 = p1 + 3 * N_NSA_HEADS
    wq = w[:, :p0].reshape(D_MODEL, N_KV, HPG, HEAD_DIM).transpose(0, 2, 1, 3).reshape(D_MODEL, p0)
    w_gates = jnp.pad(w[:, p1:p2], ((0, 0), (0, LANES - 3 * N_NSA_HEADS)))
    return jnp.concatenate([wq, w[:, p0:p1], w[:, p2:], w_gates], axis=1).astype(BF16)


def _permute_w_o(w):
    w_nsa = w[:NSA_WIDTH].reshape(N_KV, HPG, HEAD_DIM, D_MODEL).transpose(1, 0, 2, 3).reshape(NSA_WIDTH, D_MODEL)
    return jnp.concatenate([w_nsa, w[NSA_WIDTH:]], axis=0).astype(BF16)


def kernel(x_prompt, x_sample, cache_kv, cache_win, state_ret, page_table, w_in, w_cmp_k, w_cmp_v, pos_cmp_k, pos_cmp_v, ret_norm_g, w_o, ln1_g, ln1_b, w_up, w_down, ln2_g, ln2_b):
    depth = w_in.shape[0]
    assert depth == 1 and x_sample.shape[1] == 1
    batch, seq, _ = x_prompt.shape
    nb = x_sample.shape[0]
    n_pages = page_table.shape[1]
    past = n_pages * PAGE_SIZE
    assert past // SEL_BLOCK == N_SEL_BLOCKS and cache_win.shape[2] == WINDOW
    alpha = (2 * depth) ** 0.25
    log_gamma = jnp.log1p(-jnp.exp2(-5.0 - jnp.arange(N_RET_HEADS, dtype=F32)))

    w_perm = _permute_w_in(w_in[0])
    wo = _permute_w_o(w_o[0])
    wu, wd = w_up[0].astype(BF16), w_down[0].astype(BF16)
    vec = lambda a: a[0].reshape(1, -1)
    cmp_w = _compress_weights(w_cmp_k[0], w_cmp_v[0], pos_cmp_k[0], pos_cmp_v[0])
    tail_consts = (wo, vec(ln1_g), vec(ln1_b), wu, wd, vec(ln2_g), vec(ln2_b))
    norm_g = vec(ret_norm_g)

    tm = ROW_TILE
    pos_p = jnp.arange(seq, dtype=jnp.int32)
    tabs_p = _rope_tables(pos_p, ROPE_THETA, ROPE_DIMS) + _rope_tables(pos_p, RET_THETA, RET_DK)
    x2d = x_prompt.reshape(batch * seq, D_MODEL)
    kv4t, wint, rq, rk, rv, rg, qt, gt, kaug, vst, kw, vwt = _project(x2d, w_perm, tabs_p, tm=tm, seq=seq, prompt=True)
    seq_pages = seq // PAGE_SIZE
    pidx_p = jnp.broadcast_to(jnp.arange(batch, dtype=jnp.int32)[:, None], (batch, seq_pages))
    poff_p = jnp.broadcast_to(jnp.arange(seq_pages, dtype=jnp.int32)[None, :] * PAGE_SIZE, (batch, seq_pages))
    kcvc_p = _compress(pidx_p, poff_p, kv4t, *cmp_w)
    o_nsa = _nsa_prompt(qt, gt, kcvc_p, kaug, vst, kw, vwt, _importance_matrix(seq // CMP_STRIDE).T.astype(BF16), batch=batch, seq=seq)
    ret, st_p = _retention_prompt(rq, rk, rv, rg, _retention_tables(log_gamma, RET_CHUNK), norm_g, batch=batch, seq=seq)
    y_p = _tail(x2d, o_nsa, ret, *tail_consts, tm=tm, alpha=alpha)

    y_prompt = y_p.reshape(batch, seq, D_MODEL)
    to_rows = lambda a, n_slot: a.reshape(a.shape[0], n_slot, N_KV, HEAD_DIM, a.shape[-1]).transpose(0, 4, 1, 2, 3)[None]
    kv_prompt = to_rows(kv4t, 4)
    wlen = min(WINDOW, seq)
    win_prompt = to_rows(wint[:, :, seq - wlen:], 2)
    st6 = st_p.reshape(batch, N_PAIR, 2, RET_DK, 2, RET_DV)
    ret_prompt = jnp.stack([st6[:, :, 0, :, 0, :], st6[:, :, 1, :, 1, :]], axis=2).reshape(1, batch, N_RET_HEADS, RET_DK, RET_DV)

    pos_s = jnp.full((nb,), past, dtype=jnp.int32)
    tabs_s = _rope_tables(pos_s, ROPE_THETA, ROPE_DIMS) + _rope_tables(pos_s, RET_THETA, RET_DK)
    xs2d = x_sample.reshape(nb, D_MODEL)
    kv4_d, win_d, rq_d, rk_d, rv_d, rg_d, q_d, gates_d = _project(xs2d, w_perm, tabs_s, tm=nb, seq=seq, prompt=False)
    cache_t = cache_kv[0].transpose(0, 2, 3, 4, 1).reshape(cache_kv.shape[1], 4 * KV_WIDTH, PAGE_SIZE)
    cwin_t = cache_win[0].transpose(0, 2, 3, 4, 1).reshape(nb, 2 * KV_WIDTH, WINDOW)
    kcvc_d = _compress(page_table, jnp.zeros_like(page_table), cache_t, *cmp_w)
    row3 = lambda a: a.reshape(nb, 1, a.shape[-1])
    q3 = row3(q_d)
    oc_d, idx_d = _dec_score(q3, kcvc_d, _importance_matrix(past // CMP_STRIDE).astype(BF16), past=past)
    idx = idx_d[:, :N_KV, :TOP_N]
    state = state_ret[0].astype(F32).reshape(nb, N_PAIR, 2 * RET_DK, RET_DV)
    gl = jnp.repeat(jnp.exp(log_gamma), RET_DK)[None, :]
    nsa_d, ret_d, win_st, st_d = _dec_attn(page_table, idx, q3, row3(gates_d), oc_d, row3(kv4_d), row3(win_d), cache_t, cwin_t,
                                          row3(rq_d), row3(rk_d), row3(rv_d), row3(rg_d), state, gl, norm_g)
    y_s = _tail(xs2d, nsa_d.reshape(nb, NSA_WIDTH).astype(BF16), ret_d.reshape(nb, RET_WIDTH).astype(BF16), *tail_consts, tm=nb, alpha=alpha)

    y_sample = y_s.reshape(nb, 1, D_MODEL)
    kv_sample = kv4_d.reshape(1, nb, 1, 4, N_KV, HEAD_DIM)
    win_sample = to_rows(win_st, 2)
    ret_sample = st_d.reshape(1, nb, N_RET_HEADS, RET_DK, RET_DV)
    return (y_prompt, y_sample, kv_prompt, kv_sample, win_prompt, win_sample, ret_prompt, ret_sample)
```

```python
import functools

import jax
import jax.numpy as jnp
from jax import lax
from jax.experimental import pallas as pl
from jax.experimental.pallas import tpu as pltpu

F32 = jnp.float32
BF16 = jnp.bfloat16

D_MODEL = 1024
HEAD_DIM = 64
N_NSA_HEADS = 8
N_KV = 2
HPG = N_NSA_HEADS // N_KV
N_RET_HEADS = 8
RET_DK = 64
RET_DV = 64
NSA_WIDTH = N_NSA_HEADS * HEAD_DIM
RET_WIDTH = N_RET_HEADS * RET_DV
KV_WIDTH = N_KV * HEAD_DIM
D_FF = 4 * D_MODEL
CMP_STRIDE = 16
CMP_BLOCK = 32
SEL_BLOCK = 64
TOP_N = 16
N_FORCED = 3
WINDOW = 512
RET_CHUNK = 128
RET_CHUNKS_PER_STEP = 4
PAGE_SIZE = 128
ROPE_THETA = 500000.0
ROPE_DIMS = HEAD_DIM // 4
RET_THETA = 10000.0
LN_EPS = 1e-5
FORCE_BONUS = 1e4
NEG = -1e30
BELOW_NEG = -3e38

LANES = 128
SUBLANES = 8
VMEM_LIMIT = 56 * 1024 * 1024

LOG2E = 1.4426950408889634
BF16_SUBLANES = 16
ONES_ROWS = BF16_SUBLANES
CMP_VARIANT_ROWS = 128
CHUNK_PITCH = 20
PAGES_PER_STEP = 4
Q_TILE = 128
ROW_TILE = 512
KEY_TILE = 1024
assert KEY_TILE % ROW_TILE == 0
N_PAIR = N_RET_HEADS // 2
N_SEL_BLOCKS = 128

U_Q, U_KV, U_RQ, U_RK, U_RV, U_RG, U_G, U_TOTAL = 0, 4, 10, 14, 18, 22, 26, 27
HEAD_PERM = (0, 4, 1, 5, 2, 6, 3, 7)


def _dot(a, b):
    return jnp.dot(a, b, preferred_element_type=F32)


def _dot_nt(a, b):
    return lax.dot_general(a, b, (((1,), (1,)), ((), ())), preferred_element_type=F32)


def _cparams(*sem):
    return pltpu.CompilerParams(dimension_semantics=sem, vmem_limit_bytes=VMEM_LIMIT)


def _const_spec(shape):
    nd = len(shape)
    return pl.BlockSpec(shape, lambda *_: (0,) * nd)


def _rope(u, c, s_fwd, s_bwd, half):
    return u * c + pltpu.roll(u, LANES - half, 1) * s_fwd + pltpu.roll(u, half, 1) * s_bwd


def _proj_kernel(x_ref, w_ref, cq_ref, sqf_ref, sqb_ref, cr_ref, srf_ref, srb_ref, *outs, tm, seq, prompt):
    if prompt:
        kv4t_o, wint_o, rq_o, rk_o, rv_o, rg_o, qt_o, gt_o, kaug_o, vst_o, kw_o, vwt_o = outs
    else:
        kv4_o, win_o, rq_o, rk_o, rv_o, rg_o, q_o, g_o = outs
    xb = x_ref[...].astype(BF16)

    def seg(u0, n):
        return _dot(xb, w_ref[:, u0 * LANES:(u0 + n) * LANES])

    def unit(z, u):
        return z[:, u * LANES:(u + 1) * LANES]

    cq, sqf, sqb = cq_ref[...], sqf_ref[...], sqb_ref[...]
    cr, srf, srb = cr_ref[...], srf_ref[...], srb_ref[...]
    nsa_half, ret_half = ROPE_DIMS // 2, RET_DK // 2

    zq = seg(U_Q, 4)
    q = jnp.concatenate([_rope(unit(zq, c), cq, sqf, sqb, nsa_half) for c in range(4)], axis=1) * (HEAD_DIM ** -0.5)
    gates = jax.nn.sigmoid(seg(U_G, 1))
    zkv = seg(U_KV, 6)
    k_slc = _rope(unit(zkv, 2), cq, sqf, sqb, nsa_half)
    k_win = _rope(unit(zkv, 4), cq, sqf, sqb, nsa_half)
    kv4 = jnp.concatenate([zkv[:, 0:2 * LANES], k_slc, unit(zkv, 3)], axis=1)
    win = jnp.concatenate([k_win, unit(zkv, 5)], axis=1)
    if prompt:
        kv4t = jnp.transpose(kv4)
        wint = jnp.transpose(win)
        kv4t_o[...] = kv4t
        wint_o[...] = wint
        qt_o[...] = jnp.transpose(q * LOG2E).astype(BF16)
        gt_o[...] = jnp.transpose(gates)
        row = pl.program_id(0) * tm + lax.broadcasted_iota(jnp.int32, (tm, 1), 0)
        blk = (row & (seq - 1)) >> 6
        onehot = lax.broadcasted_iota(jnp.int32, (tm, LANES), 1) == blk
        kaug_o[:, 0:LANES] = k_slc.astype(BF16)
        kaug_o[:, LANES:2 * LANES] = jnp.where(onehot, 1.0, 0.0).astype(BF16)
        vst_o[0:LANES, :] = kv4t[3 * LANES:4 * LANES].astype(BF16)
        vst_o[LANES:, :] = jnp.ones((ONES_ROWS, tm), BF16)
        kw_o[...] = k_win.astype(BF16)
        vwt = wint[LANES:2 * LANES].astype(BF16)
        for j in range(tm // LANES):
            vwt_o[j] = vwt[:, j * LANES:(j + 1) * LANES]
    else:
        kv4_o[...] = kv4
        win_o[...] = win
        q_o[...] = q.astype(BF16).astype(F32)
        g_o[...] = gates

    zrq = seg(U_RQ, 4)
    zrk = seg(U_RK, 4)
    for c in range(4):
        sl = slice(c * LANES, (c + 1) * LANES)
        rq_o[:, sl] = _rope(unit(zrq, c), cr, srf, srb, ret_half)
        rk_o[:, sl] = _rope(unit(zrk, c), cr, srf, srb, ret_half) * (RET_DK ** -0.5)
    rv_o[...] = seg(U_RV, 4)
    rg_o[...] = seg(U_RG, 4)


def _rope_tables(pos, theta, n_rot):
    half = n_rot // 2
    inv = theta ** (-jnp.arange(half, dtype=F32) / half)
    ang = pos.astype(F32)[:, None] * inv[None, :]
    cos, sin = jnp.cos(ang), jnp.sin(ang)
    n = pos.shape[0]
    rest = HEAD_DIM - n_rot
    c = jnp.concatenate([cos, cos, jnp.ones((n, rest), F32)], 1)
    s_fwd = jnp.concatenate([-sin, jnp.zeros((n, half + rest), F32)], 1)
    s_bwd = jnp.concatenate([jnp.zeros((n, half), F32), sin, jnp.zeros((n, rest), F32)], 1)
    rep = LANES // HEAD_DIM
    return tuple(jnp.tile(t, (1, rep)) for t in (c, s_fwd, s_bwd))


def _project(x2d, w_perm, tables, *, tm, seq, prompt):
    rows = x2d.shape[0]
    n_tab_tiles = tables[0].shape[0] // tm
    row_spec = lambda w: pl.BlockSpec((tm, w), lambda i: (i, 0))
    tab_spec = pl.BlockSpec((tm, LANES), lambda i: (i % n_tab_tiles, 0))
    if prompt:
        assert tm == ROW_TILE and seq % KEY_TILE == 0
        batch, tpb = rows // seq, seq // tm
        feat_spec = lambda f: pl.BlockSpec((None, f, tm), lambda i: (i // tpb, 0, i % tpb))
        outs = [(feat_spec(4 * LANES), (batch, 4 * LANES, seq), F32), (feat_spec(2 * LANES), (batch, 2 * LANES, seq), F32)]
    else:
        outs = [(row_spec(4 * LANES), (rows, 4 * LANES), F32), (row_spec(2 * LANES), (rows, 2 * LANES), F32)]
    outs += [(row_spec(4 * LANES), (rows, 4 * LANES), F32)] * 4
    if prompt:
        outs += [(feat_spec(4 * LANES), (batch, 4 * LANES, seq), BF16),
                 (feat_spec(LANES), (batch, LANES, seq), F32),
                 (row_spec(2 * LANES), (rows, 2 * LANES), BF16),
                 (pl.BlockSpec((None, LANES + ONES_ROWS, tm), lambda i: (i, 0, 0)), (rows // tm, LANES + ONES_ROWS, tm), BF16),
                 (row_spec(LANES), (rows, LANES), BF16),
                 (pl.BlockSpec((tm // LANES, LANES, LANES), lambda i: (i, 0, 0)), (rows // LANES, LANES, LANES), BF16)]
    else:
        outs += [(row_spec(4 * LANES), (rows, 4 * LANES), F32), (row_spec(LANES), (rows, LANES), F32)]
    return pl.pallas_call(
        functools.partial(_proj_kernel, tm=tm, seq=seq, prompt=prompt),
        grid=(rows // tm,),
        in_specs=[row_spec(D_MODEL), _const_spec(w_perm.shape)] + [tab_spec] * 6,
        out_specs=[s for s, _, _ in outs],
        out_shape=[jax.ShapeDtypeStruct(shape, dt) for _, shape, dt in outs],
        compiler_params=_cparams("parallel"),
        name="proj_prompt" if prompt else "proj_decode",
    )(x2d, w_perm, *tables)


def _cmp_kernel(pn_ref, po_ref, src_ref, wlo_ref, whi_ref, plo_ref, phi_ref, out_ref, stage, slab, sem, *, n_pages):
    b = pl.program_id(0)
    nb = pl.num_programs(0)
    slot = b % 2

    def page_copy(bb, sl, p, kv):
        return pltpu.make_async_copy(
            src_ref.at[pn_ref[bb, p], pl.ds(kv * LANES, LANES), pl.ds(pl.multiple_of(po_ref[bb, p], PAGE_SIZE), PAGE_SIZE)],
            stage.at[sl, kv, p],
            sem.at[sl])

    def start_all(bb, sl):
        for p in range(n_pages):
            for kv in range(2):
                page_copy(bb, sl, p, kv).start()

    @pl.when(b == 0)
    def _():
        start_all(0, 0)

    @pl.when(b + 1 < nb)
    def _():
        start_all(b + 1, 1 - slot)

    for p in range(n_pages):
        for kv in range(2):
            page_copy(b, slot, p, kv).wait()

    chunks_per_page = PAGE_SIZE // CMP_STRIDE
    page_rows = chunks_per_page * CHUNK_PITCH

    def to_rows(pp, carry):
        for u in range(PAGES_PER_STEP):
            p = pp * PAGES_PER_STEP + u
            r0 = pl.multiple_of(p * page_rows, SUBLANES)
            for kv in range(2):
                x = jnp.transpose(stage[slot, kv, p])
                for c in range(chunks_per_page):
                    slab[kv, pl.ds(r0 + c * CHUNK_PITCH, CMP_STRIDE), :] = x[c * CMP_STRIDE:(c + 1) * CMP_STRIDE]
        return carry

    lax.fori_loop(0, n_pages // PAGES_PER_STEP, to_rows, 0)

    n_chunks = n_pages * chunks_per_page
    acc_lo = jnp.zeros((n_chunks, 2 * LANES), F32)
    acc_hi = jnp.zeros((n_chunks, 2 * LANES), F32)
    for r in range(CMP_STRIDE):
        xr = jnp.concatenate([slab[kv, pl.ds(r, n_chunks, stride=CHUNK_PITCH), :] for kv in range(2)], axis=1)
        acc_lo = acc_lo + _dot((xr + plo_ref[r:r + 1, :]).astype(BF16), wlo_ref[r])
        acc_hi = acc_hi + _dot((xr + phi_ref[r:r + 1, :]).astype(BF16), whi_ref[r])
    nxt = pltpu.roll(acc_hi, n_chunks - 1, 0)
    last = lax.broadcasted_iota(jnp.int32, (n_chunks, 1), 0) == n_chunks - 1
    out_ref[...] = jnp.where(last, 0.0, acc_lo + nxt)


def _compress(page_index, page_offset, src, wlo, whi, plo, phi):
    nb, n_pages = page_index.shape
    n_chunks = n_pages * PAGE_SIZE // CMP_STRIDE
    grid_spec = pltpu.PrefetchScalarGridSpec(
        num_scalar_prefetch=2,
        grid=(nb,),
        in_specs=[pl.BlockSpec(memory_space=pl.ANY),
                  pl.BlockSpec(wlo.shape, lambda b, *_: (0, 0, 0)),
                  pl.BlockSpec(whi.shape, lambda b, *_: (0, 0, 0)),
                  pl.BlockSpec(plo.shape, lambda b, *_: (0, 0)),
                  pl.BlockSpec(phi.shape, lambda b, *_: (0, 0))],
        out_specs=pl.BlockSpec((None, n_chunks, 2 * LANES), lambda b, *_: (b, 0, 0)),
        scratch_shapes=[pltpu.VMEM((2, 2, n_pages, LANES, PAGE_SIZE), F32),
                        pltpu.VMEM((2, n_chunks * CHUNK_PITCH, LANES), F32),
                        pltpu.SemaphoreType.DMA((2,))],
    )
    return pl.pallas_call(
        functools.partial(_cmp_kernel, n_pages=n_pages),
        grid_spec=grid_spec,
        out_shape=jax.ShapeDtypeStruct((nb, n_chunks, 2 * LANES), F32),
        compiler_params=_cparams("arbitrary"),
        name="compress",
    )(page_index, page_offset, src, wlo, whi, plo, phi)


def _compress_weights(w_k, w_v, pos_k, pos_v):
    def bd(wk_r, wv_r):
        z = jnp.zeros((HEAD_DIM, HEAD_DIM), F32)
        rows = [[wk_r, z, z, z], [z, wk_r, z, z], [z, z, wv_r, z], [z, z, z, wv_r]]
        return jnp.block(rows)

    w_all = jax.vmap(bd)(w_k, w_v).astype(BF16)
    p_all = jnp.concatenate([pos_k, pos_k, pos_v, pos_v], axis=1)
    return w_all[:CMP_STRIDE], w_all[CMP_STRIDE:], p_all[:CMP_STRIDE], p_all[CMP_STRIDE:]


def _masked_softmax(s, mask):
    s = jnp.where(mask, s, NEG)
    m = jnp.max(s, -1, keepdims=True)
    p = jnp.exp(s - m) * mask.astype(F32)
    return p / jnp.maximum(jnp.sum(p, -1, keepdims=True), 1e-30)


def _split3(x):
    hi = x.astype(BF16)
    r1 = x - hi.astype(F32)
    mid = r1.astype(BF16)
    lo = (r1 - mid.astype(F32)).astype(BF16)
    return hi, mid, lo


def _importance_matrix(n_cmp):
    n = jnp.arange(n_cmp)[:, None]
    j = jnp.arange(N_SEL_BLOCKS)[None, :]
    inner = (n >= 4 * j) & (n <= 4 * j + 2)
    edge = (n == 4 * j - 1) | (n == 4 * j + 3)
    return jnp.where(inner, 2.0, jnp.where(edge, 1.0, 0.0)).astype(F32)


def _softmax_over_keys(s, mask):
    s = jnp.where(mask, s, NEG)
    m = jnp.max(s, 0, keepdims=True)
    p = jnp.exp2(s - m)
    inv = jnp.where(m > 0.5 * NEG, 1.0 / jnp.maximum(jnp.sum(p, 0, keepdims=True), 1e-30), 0.0)
    return p * inv


def _nsa_kernel(qt_ref, gt_ref, kcvc_ref, kaug_ref, vst_ref, kw_ref, vwt_ref, mt_ref, out_ref,
                kc_s, vct_s, m_s, acc_s, s_buf, tmax_s, qaug_s, ocw_s):
    i = pl.program_id(1)
    t0 = i * Q_TILE
    n_cmp = kcvc_ref.shape[0]

    @pl.when(i == 0)
    def _():
        kcvc = kcvc_ref[...]
        kc_s[...] = kcvc[:, 0:LANES].astype(BF16)
        vct_s[...] = jnp.transpose(kcvc[:, LANES:2 * LANES]).astype(BF16)

    lane = lax.broadcasted_iota(jnp.int32, (1, Q_TILE), 1)
    tq = jnp.concatenate([t0 + lane] * N_NSA_HEADS, axis=1)
    low_rows = lax.broadcasted_iota(jnp.int32, (LANES, 1), 0) < HEAD_DIM
    qt = qt_ref[...]
    zero = jnp.zeros((), BF16)
    q_all = jnp.concatenate([jnp.where(low_rows if g == 0 else ~low_rows, qt[c * LANES:(c + 1) * LANES], zero)
                             for g in range(N_KV) for c in range(HPG)], axis=1)

    qaug_s[0:LANES, :] = q_all
    gt = gt_ref[...]
    gk = [jnp.concatenate([gt[3 * h + k:3 * h + k + 1] for h in range(N_NSA_HEADS)], axis=1) for k in range(3)]
    cur = jnp.concatenate([(t0 + lane) >> 6] * N_KV, axis=1)

    def compressed_and_select(n_act, j_act):
        cend = lax.broadcasted_iota(jnp.int32, (n_act, 1), 0) * CMP_STRIDE + (CMP_BLOCK - 1)
        p_c = _softmax_over_keys(_dot(kc_s[0:n_act, :], q_all), cend <= tq)
        ocw_s[...] = gk[0] * _dot(vct_s[:, 0:n_act], p_c.astype(BF16))
        psum = jnp.concatenate([sum(p_c[:, (g * HPG + c) * Q_TILE:(g * HPG + c + 1) * Q_TILE] for c in range(HPG))
                                for g in range(N_KV)], axis=1)
        mt = mt_ref[0:j_act, 0:n_act]
        imp_t = sum(_dot(mt, part) for part in _split3(psum))
        blk = lax.broadcasted_iota(jnp.int32, (j_act, 1), 0)
        blk_f = blk.astype(F32)
        forced = (blk == 0) | (blk == cur) | (blk == cur - 1)
        score = jnp.where(forced, BELOW_NEG, jnp.where(blk <= cur, imp_t, NEG))
        bias = jnp.where(forced, 0.0, NEG)
        for _ in range(TOP_N - N_FORCED):
            best = jnp.max(score, axis=0, keepdims=True)
            first = jnp.min(jnp.where(score == best, blk_f, float(N_SEL_BLOCKS)), axis=0, keepdims=True)
            pick = blk_f == first
            bias = jnp.where(pick, 0.0, bias)
            score = jnp.where(pick, BELOW_NEG, score)
        bias = bias.astype(BF16)
        qaug_s[LANES:LANES + j_act, :] = jnp.concatenate(
            [bias[:, g * Q_TILE:(g + 1) * Q_TILE] for g in range(N_KV) for _ in range(HPG)], axis=1)
        if j_act < N_SEL_BLOCKS:
            qaug_s[LANES + j_act:, :] = jnp.full((N_SEL_BLOCKS - j_act, qaug_s.shape[1]), NEG, BF16)

    q_blocks_per_variant = CMP_VARIANT_ROWS * CMP_STRIDE // Q_TILE
    n_variants = n_cmp // CMP_VARIANT_ROWS
    for v in range(n_variants):
        @pl.when(i // q_blocks_per_variant == v)
        def _(v=v):
            compressed_and_select((v + 1) * CMP_VARIANT_ROWS, min(N_SEL_BLOCKS, (v + 1) * CMP_VARIANT_ROWS * CMP_STRIDE // SEL_BLOCK))

    m_s[...] = jnp.full(m_s.shape, NEG, F32)
    acc_s[...] = jnp.zeros(acc_s.shape, F32)

    def scores(kt, causal):
        k0 = pl.multiple_of(kt * KEY_TILE, KEY_TILE)
        s = _dot(kaug_ref[pl.ds(k0, KEY_TILE), :], qaug_s[...])
        if causal:
            kpos = k0 + lax.broadcasted_iota(jnp.int32, (KEY_TILE, 1), 0)
            s = jnp.where(kpos <= tq, s, NEG)
        s_buf[...] = s
        tmax_s[...] = jnp.max(s, 0, keepdims=True)

    def accumulate(kt):
        m_old = m_s[...]
        m_new = jnp.maximum(m_old, tmax_s[...])
        alpha = jnp.exp2(m_old - m_new)
        p = jnp.exp2(s_buf[...] - m_new).astype(BF16)
        n_sub = KEY_TILE // ROW_TILE
        pv = sum(_dot(vst_ref[kt * n_sub + j], p[j * ROW_TILE:(j + 1) * ROW_TILE]) for j in range(n_sub))
        acc_s[...] = alpha * acc_s[...] + pv
        m_s[...] = m_new

    last = (t0 + Q_TILE - 1) // KEY_TILE

    @pl.when(last == 0)
    def _():
        scores(0, True)

    @pl.when(last > 0)
    def _():
        scores(0, False)

        def body(kt, carry):
            accumulate(kt)
            scores(kt + 1, False)
            return carry

        lax.fori_loop(0, last - 1, body, 0)
        accumulate(last - 1)
        scores(last, True)

    accumulate(last)
    o_s = acc_s[0:LANES, :] * (1.0 / acc_s[LANES:LANES + 1, :])

    w0 = pl.multiple_of(jnp.maximum(t0 - WINDOW, 0), Q_TILE)
    n_win = WINDOW + Q_TILE
    dist = tq - (w0 + lax.broadcasted_iota(jnp.int32, (n_win, 1), 0))
    p_w = _softmax_over_keys(_dot(kw_ref[pl.ds(w0, n_win), :], q_all), (dist >= 0) & (dist < WINDOW))
    w_blk = w0 // LANES
    vwt = jnp.concatenate([vwt_ref[w_blk + j] for j in range(n_win // LANES)], axis=1)
    o_w = _dot(vwt, p_w.astype(BF16))

    o = ocw_s[...] + gk[2] * o_w + gk[1] * o_s
    half = HPG * Q_TILE
    comb = jnp.where(low_rows, o[:, 0:half], o[:, half:2 * half])
    for c in range(HPG):
        out_ref[:, c * LANES:(c + 1) * LANES] = jnp.transpose(comb[:, c * Q_TILE:(c + 1) * Q_TILE]).astype(BF16)


def _nsa_prompt(qt, gt, kcvc, kaug, vst, kw, vwt, mt, *, batch, seq):
    nq = seq // Q_TILE
    n_cols = N_NSA_HEADS * Q_TILE
    n_cmp = kcvc.shape[1]
    col_spec = lambda f: pl.BlockSpec((None, f, Q_TILE), lambda b, i: (b, 0, i))
    seq_spec = lambda w: pl.BlockSpec((seq, w), lambda b, i: (b, 0))
    return pl.pallas_call(
        _nsa_kernel,
        grid=(batch, nq),
        in_specs=[col_spec(4 * LANES), col_spec(LANES),
                  pl.BlockSpec((None, n_cmp, 2 * LANES), lambda b, i: (b, 0, 0)),
                  seq_spec(2 * LANES),
                  pl.BlockSpec((seq // ROW_TILE, LANES + ONES_ROWS, ROW_TILE), lambda b, i: (b, 0, 0)),
                  seq_spec(LANES),
                  pl.BlockSpec((seq // LANES, LANES, LANES), lambda b, i: (b, 0, 0)),
                  _const_spec(mt.shape)],
        out_specs=pl.BlockSpec((Q_TILE, 4 * LANES), lambda b, i: (b * nq + i, 0)),
        out_shape=jax.ShapeDtypeStruct((batch * seq, 4 * LANES), BF16),
        scratch_shapes=[pltpu.VMEM((n_cmp, LANES), BF16), pltpu.VMEM((LANES, n_cmp), BF16),
                        pltpu.VMEM((1, n_cols), F32), pltpu.VMEM((LANES + ONES_ROWS, n_cols), F32),
                        pltpu.VMEM((KEY_TILE, n_cols), F32), pltpu.VMEM((1, n_cols), F32),
                        pltpu.VMEM((2 * LANES, n_cols), BF16), pltpu.VMEM((LANES, n_cols), F32)],
        compiler_params=_cparams("parallel", "arbitrary"),
        name="nsa_prompt",
    )(qt, gt, kcvc, kaug, vst, kw, vwt, mt)


def _head_sum(a, low):
    s0 = jnp.sum(jnp.where(low, a, 0.0), -1, keepdims=True)
    s1 = jnp.sum(jnp.where(low, 0.0, a), -1, keepdims=True)
    return jnp.where(low, s0, s1)


def _gated_group_norm(o, rg, norm_g, low):
    mu = _head_sum(o, low) * (1.0 / RET_DV)
    d = o - mu
    var = _head_sum(d * d, low) * (1.0 / RET_DV)
    gn = d * lax.rsqrt(var + LN_EPS) * norm_g
    return (rg * jax.nn.sigmoid(rg)) * gn


def _ret_kernel(rq_ref, rk_ref, rv_ref, rg_ref, decay_ref, qdec_ref, kdec_ref, gcl_ref, bd_ref, ng_ref,
                ret_o, st_o, st_s):
    c = pl.program_id(1)

    @pl.when(c == 0)
    def _():
        st_s[...] = jnp.zeros(st_s.shape, F32)

    low = lax.broadcasted_iota(jnp.int32, (1, LANES), 1) < RET_DK
    zero = jnp.zeros((), BF16)
    for j in range(RET_CHUNKS_PER_STEP):
        rows = slice(j * RET_CHUNK, (j + 1) * RET_CHUNK)
        for p in range(N_PAIR):
            sl = slice(p * LANES, (p + 1) * LANES)
            q, k, v = rq_ref[rows, sl], rk_ref[rows, sl], rv_ref[rows, sl]
            qb, kb, vb = q.astype(BF16), k.astype(BF16), v.astype(BF16)
            in0 = _dot_nt(jnp.where(low, qb, zero), kb) * decay_ref[2 * p]
            in1 = _dot_nt(jnp.where(low, zero, qb), kb) * decay_ref[2 * p + 1]
            inner = jnp.concatenate([in0.astype(BF16), in1.astype(BF16)], axis=1)
            v_split = jnp.concatenate([jnp.where(low, vb, zero), jnp.where(low, zero, vb)], axis=0)
            state = st_s[p]
            o = _dot(inner, v_split) + _dot((q * qdec_ref[:, sl]).astype(BF16), state.astype(BF16))
            kd_t = jnp.transpose(k * kdec_ref[:, sl]).astype(BF16)
            st_s[p] = state * gcl_ref[:, sl] + _dot(kd_t, vb) * bd_ref[...]
            ret_o[rows, sl] = _gated_group_norm(o, rg_ref[rows, sl], ng_ref[:, sl], low).astype(BF16)

    @pl.when(c == pl.num_programs(1) - 1)
    def _():
        st_o[...] = st_s[...]


def _retention_tables(log_gamma, chunk):
    i = jnp.arange(chunk, dtype=F32)
    diff = i[:, None] - i[None, :]
    decay = jnp.where(diff >= 0, jnp.exp(log_gamma[:, None, None] * jnp.maximum(diff, 0.0)), 0.0)
    per_lane = lambda t: jnp.repeat(t, RET_DK, axis=-1)
    qdec = per_lane(jnp.exp((i + 1.0)[:, None] * log_gamma[None, :]))
    kdec = per_lane(jnp.exp((chunk - 1.0 - i)[:, None] * log_gamma[None, :]))
    gcl = per_lane(jnp.exp(chunk * log_gamma)[None, :])
    r = jnp.arange(LANES)
    bd = ((r[:, None] // RET_DK) == (r[None, :] // RET_DV)).astype(F32)
    return decay, qdec, kdec, gcl, bd


def _retention_prompt(rq, rk, rv, rg, tables, norm_g, *, batch, seq):
    decay, qdec, kdec, gcl, bd = tables
    step_rows = RET_CHUNK * RET_CHUNKS_PER_STEP
    nc = seq // step_rows
    row_spec = pl.BlockSpec((step_rows, RET_WIDTH), lambda b, c: (b * nc + c, 0))
    return pl.pallas_call(
        _ret_kernel,
        grid=(batch, nc),
        in_specs=[row_spec] * 4 + [_const_spec(decay.shape), _const_spec(qdec.shape), _const_spec(kdec.shape),
                                   _const_spec(gcl.shape), _const_spec(bd.shape), _const_spec(norm_g.shape)],
        out_specs=[row_spec, pl.BlockSpec((None, N_PAIR, LANES, LANES), lambda b, c: (b, 0, 0, 0))],
        out_shape=[jax.ShapeDtypeStruct((batch * seq, RET_WIDTH), BF16),
                   jax.ShapeDtypeStruct((batch, N_PAIR, LANES, LANES), F32)],
        scratch_shapes=[pltpu.VMEM((N_PAIR, LANES, LANES), F32)],
        compiler_params=_cparams("parallel", "arbitrary"),
        name="retention",
    )(rq, rk, rv, rg, decay, qdec, kdec, gcl, bd, norm_g)


def _layer_norm(v, g, b):
    mu = jnp.mean(v, -1, keepdims=True)
    d = v - mu
    var = jnp.mean(d * d, -1, keepdims=True)
    return d * lax.rsqrt(var + LN_EPS) * g + b


def _tail_kernel(x_ref, nsa_ref, ret_ref, wo_ref, g1_ref, b1_ref, wu_ref, wd_ref, g2_ref, b2_ref, y_ref, *, alpha):
    mix = _dot(nsa_ref[...], wo_ref[0:NSA_WIDTH, :]) + _dot(ret_ref[...], wo_ref[NSA_WIDTH:, :])
    x1 = _layer_norm(alpha * x_ref[...] + mix, g1_ref[...], b1_ref[...])
    x1b = x1.astype(BF16)
    f = jnp.zeros(x1.shape, F32)
    for j in range(D_FF // D_MODEL):
        sl = slice(j * D_MODEL, (j + 1) * D_MODEL)
        h = jnp.maximum(_dot(x1b, wu_ref[:, sl]), 0.0)
        f = f + _dot((h * h).astype(BF16), wd_ref[sl, :])
    y_ref[...] = _layer_norm(alpha * x1 + f, g2_ref[...], b2_ref[...])


def _tail(x2d, nsa, ret, wo, g1, b1, wu, wd, g2, b2, *, tm, alpha):
    rows = x2d.shape[0]
    row_spec = lambda w: pl.BlockSpec((tm, w), lambda i: (i, 0))
    consts = (wo, g1, b1, wu, wd, g2, b2)
    return pl.pallas_call(
        functools.partial(_tail_kernel, alpha=alpha),
        grid=(rows // tm,),
        in_specs=[row_spec(D_MODEL), row_spec(NSA_WIDTH), row_spec(RET_WIDTH)] + [_const_spec(c.shape) for c in consts],
        out_specs=row_spec(D_MODEL),
        out_shape=jax.ShapeDtypeStruct((rows, D_MODEL), F32),
        compiler_params=_cparams("parallel"),
        name="tail",
    )(x2d, nsa, ret, *consts)


DEC_ROWS = 16
N_DEC_SEL = TOP_N - 1


def _dec_stack_q(q, low):
    rows = lax.broadcasted_iota(jnp.int32, (DEC_ROWS, 1), 0)
    qs = jnp.zeros((DEC_ROWS, LANES), F32)
    for g in range(N_KV):
        for c in range(HPG):
            piece = jnp.where(low if g == 0 else ~low, q[:, c * LANES:(c + 1) * LANES], 0.0)
            qs = jnp.where(rows == g * HPG + c, piece, qs)
    return qs, rows


def _dec_score_kernel(q_ref, kcvc_ref, m_ref, oc_o, idx_o, score_s, *, past):
    lane = lax.broadcasted_iota(jnp.int32, (1, LANES), 1)
    low = lane < HEAD_DIM
    qs, rows = _dec_stack_q(q_ref[...], low)
    qsb = qs.astype(BF16)
    kcvc = kcvc_ref[...]
    kc = kcvc[:, 0:LANES].astype(BF16)
    vc = kcvc[:, LANES:2 * LANES].astype(BF16)
    n_cmp = kcvc.shape[0]
    cend = lax.broadcasted_iota(jnp.int32, (1, n_cmp), 1) * CMP_STRIDE + (CMP_BLOCK - 1)
    p = _masked_softmax(_dot_nt(qsb, kc), jnp.broadcast_to(cend <= past, (DEC_ROWS, n_cmp)))
    oc_o[...] = _dot(p.astype(BF16), vc)

    psum = jnp.zeros((DEC_ROWS, n_cmp), F32)
    for g in range(N_KV):
        acc = p[g * HPG:g * HPG + 1]
        for c in range(1, HPG):
            acc = acc + p[g * HPG + c:g * HPG + c + 1]
        psum = jnp.where(rows == g, acc, psum)
    m = m_ref[...]
    imp = sum(_dot(part, m) for part in _split3(psum))
    forced = (lane == 0) | (lane == N_SEL_BLOCKS - 1)
    b = pl.program_id(0)
    score_s[b] = imp + FORCE_BONUS * forced.astype(F32)

    @pl.when(b == pl.num_programs(0) - 1)
    def _():
        lane_f = lane.astype(F32)
        score = score_s[...].reshape(-1, LANES)
        picked = jnp.zeros(score.shape, F32)
        for r in range(N_DEC_SEL):
            best = jnp.max(score, -1, keepdims=True)
            first = jnp.min(jnp.where(score == best, lane_f, float(LANES)), -1, keepdims=True)
            picked = jnp.where(lane == r, first, picked)
            score = jnp.where(lane_f == first, BELOW_NEG, score)
        idx_o[...] = picked.astype(jnp.int32).reshape(idx_o.shape)


def _dec_score(q3, kcvc, m, *, past):
    nb = q3.shape[0]
    n_cmp = kcvc.shape[1]
    return pl.pallas_call(
        functools.partial(_dec_score_kernel, past=past),
        grid=(nb,),
        in_specs=[pl.BlockSpec((None, 1, 4 * LANES), lambda b: (b, 0, 0)),
                  pl.BlockSpec((None, n_cmp, 2 * LANES), lambda b: (b, 0, 0)),
                  _const_spec(m.shape)],
        out_specs=[pl.BlockSpec((None, DEC_ROWS, LANES), lambda b: (b, 0, 0)),
                   _const_spec((nb, DEC_ROWS, LANES))],
        out_shape=[jax.ShapeDtypeStruct((nb, DEC_ROWS, LANES), F32), jax.ShapeDtypeStruct((nb, DEC_ROWS, LANES), jnp.int32)],
        scratch_shapes=[pltpu.VMEM((nb, DEC_ROWS, LANES), F32)],
        compiler_params=_cparams("arbitrary"),
        name="dec_score",
    )(q3, kcvc, m)


def _round_bf16(x):
    return x.astype(BF16).astype(F32)


def _dec_attn_kernel(pt_ref, idx_ref, q_ref, g_ref, oc_ref, kvn_ref, wn_ref, cache_ref, cw_ref,
                     rq_ref, rk_ref, rv_ref, rg_ref, st_ref, gl_ref, ng_ref,
                     nsa_o, ret_o, win_o, st_o, kbuf, sem):
    b = pl.program_id(0)
    nb = pl.num_programs(0)
    slot = b % 2

    def blk_copy(bb, sl, g, k):
        j = idx_ref[bb, g, k]
        return pltpu.make_async_copy(
            cache_ref.at[pt_ref[bb, j >> 1], pl.ds(2 * LANES, 2 * LANES), :],
            kbuf.at[sl, g, :, pl.ds(k * PAGE_SIZE, PAGE_SIZE)],
            sem.at[sl])

    def start_all(bb, sl):
        for g in range(N_KV):
            for k in range(N_DEC_SEL):
                blk_copy(bb, sl, g, k).start()

    @pl.when(b == 0)
    def _():
        start_all(0, 0)

    @pl.when(b + 1 < nb)
    def _():
        start_all(b + 1, 1 - slot)

    lane = lax.broadcasted_iota(jnp.int32, (1, LANES), 1)
    low = lane < HEAD_DIM
    qs, rows = _dec_stack_q(q_ref[...], low)
    qsb = qs.astype(BF16)
    g0_rows = rows < HPG

    cw = cw_ref[...]
    wb = cw.shape[1]
    wn = wn_ref[...]
    col = lax.broadcasted_iota(jnp.int32, (1, wb), 1)
    valid = jnp.broadcast_to((wb - col) < WINDOW, (DEC_ROWS, wb))
    s_w = jnp.where(valid, _dot(qsb, cw[0:LANES].astype(BF16)), NEG)
    s_wn = jnp.sum(qs * _round_bf16(wn[:, 0:LANES]), -1, keepdims=True)
    m_w = jnp.maximum(jnp.max(s_w, -1, keepdims=True), s_wn)
    p_w = jnp.exp(s_w - m_w) * valid.astype(F32)
    p_wn = jnp.exp(s_wn - m_w)
    l_w = jnp.maximum(jnp.sum(p_w, -1, keepdims=True) + p_wn, 1e-30)
    o_w = _dot_nt((p_w / l_w).astype(BF16), cw[LANES:2 * LANES].astype(BF16)) + _round_bf16(p_wn / l_w) * _round_bf16(wn[:, LANES:2 * LANES])
    new_cols = jnp.concatenate(
        [jnp.concatenate([jnp.transpose(jnp.broadcast_to(wn[:, u * LANES:(u + 1) * LANES], (LANES, LANES)))] * (wb // LANES), axis=1)
         for u in range(2)], axis=0)
    win_o[...] = jnp.where(col == wb - 1, new_cols, pltpu.roll(cw, wb - 1, 1))

    for g in range(N_KV):
        for k in range(N_DEC_SEL):
            blk_copy(b, slot, g, k).wait()
    kvn = kvn_ref[...]
    s_g = [_dot(qsb, kbuf[slot, g, 0:LANES, :].astype(BF16)) for g in range(N_KV)]
    pos = lax.broadcasted_iota(jnp.int32, (1, N_DEC_SEL * PAGE_SIZE), 1)
    picked_half = []
    for g in range(N_KV):
        half = jnp.zeros(pos.shape, jnp.int32)
        for k in range(N_DEC_SEL):
            half = jnp.where(pos >> 7 == k, idx_ref[b, g, k] & 1, half)
        picked_half.append(half)
    in_block = ((pos >> 6) & 1) == jnp.where(g0_rows, picked_half[0], picked_half[1])
    s_s = jnp.where(in_block, jnp.where(g0_rows, s_g[0], s_g[1]), NEG)
    s_sn = jnp.sum(qs * _round_bf16(kvn[:, 2 * LANES:3 * LANES]), -1, keepdims=True)
    m_sel = jnp.maximum(jnp.max(s_s, -1, keepdims=True), s_sn)
    p_s = jnp.exp(s_s - m_sel)
    p_sn = jnp.exp(s_sn - m_sel)
    l_sel = jnp.maximum(jnp.sum(p_s, -1, keepdims=True) + p_sn, 1e-30)
    pn = (p_s / l_sel).astype(BF16)
    o_g = [_dot_nt(pn, kbuf[slot, g, LANES:2 * LANES, :].astype(BF16)) for g in range(N_KV)]
    o_s = jnp.where(g0_rows, o_g[0], o_g[1]) + _round_bf16(p_sn / l_sel) * _round_bf16(kvn[:, 3 * LANES:4 * LANES])

    gates = g_ref[...]
    gcol = [jnp.zeros((DEC_ROWS, 1), F32) for _ in range(3)]
    for h in range(N_NSA_HEADS):
        for k in range(3):
            gcol[k] = jnp.where(rows == h, gates[:, 3 * h + k:3 * h + k + 1], gcol[k])
    o = gcol[0] * oc_ref[...] + gcol[1] * o_s + gcol[2] * o_w
    for c in range(HPG):
        nsa_o[:, c * LANES:(c + 1) * LANES] = jnp.where(low, o[c:c + 1], o[HPG + c:HPG + c + 1])

    top = lax.broadcasted_iota(jnp.int32, (LANES, 1), 0) < RET_DK
    for p in range(N_PAIR):
        sl = slice(p * LANES, (p + 1) * LANES)
        q, k, v, gl = rq_ref[:, sl], rk_ref[:, sl], rv_ref[:, sl], gl_ref[:, sl]
        state = st_ref[p]
        inner = _head_sum(q * k, low)

        def per_row(row):
            return jnp.transpose(jnp.broadcast_to(row, (LANES, LANES)))[:, 0:RET_DV]

        v_rows = jnp.where(top, v[:, 0:RET_DV], v[:, RET_DV:2 * RET_DV])
        qs_state = per_row(q * gl) * state
        cross = jnp.concatenate([jnp.sum(qs_state[0:RET_DK], axis=0, keepdims=True),
                                 jnp.sum(qs_state[RET_DK:2 * RET_DK], axis=0, keepdims=True)], axis=1)
        o_r = inner * v + cross
        st_o[p] = per_row(gl) * state + per_row(k) * v_rows
        ret_o[:, sl] = _gated_group_norm(o_r, rg_ref[:, sl], ng_ref[:, sl], low)


def _dec_attn(page_table, idx, q3, gates3, oc, kvn3, wn3, cache4, cwin, rq3, rk3, rv3, rg3, state, gl, ng):
    nb = q3.shape[0]
    wb = cwin.shape[2]
    row3 = lambda w: pl.BlockSpec((None, 1, w), lambda b, *_: (b, 0, 0))
    st_spec = pl.BlockSpec((None, N_PAIR, 2 * RET_DK, RET_DV), lambda b, *_: (b, 0, 0, 0))
    win_spec = pl.BlockSpec((None, 2 * LANES, wb), lambda b, *_: (b, 0, 0))
    grid_spec = pltpu.PrefetchScalarGridSpec(
        num_scalar_prefetch=2,
        grid=(nb,),
        in_specs=[row3(4 * LANES), row3(LANES), pl.BlockSpec((None, DEC_ROWS, LANES), lambda b, *_: (b, 0, 0)),
                  row3(4 * LANES), row3(2 * LANES), pl.BlockSpec(memory_space=pl.ANY), win_spec,
                  row3(4 * LANES), row3(4 * LANES), row3(4 * LANES), row3(4 * LANES), st_spec,
                  pl.BlockSpec(gl.shape, lambda b, *_: (0, 0)), pl.BlockSpec(ng.shape, lambda b, *_: (0, 0))],
        out_specs=[row3(4 * LANES), row3(4 * LANES), win_spec, st_spec],
        scratch_shapes=[pltpu.VMEM((2, N_KV, 2 * LANES, N_DEC_SEL * PAGE_SIZE), F32), pltpu.SemaphoreType.DMA((2,))],
    )
    return pl.pallas_call(
        _dec_attn_kernel,
        grid_spec=grid_spec,
        out_shape=[jax.ShapeDtypeStruct((nb, 1, 4 * LANES), F32), jax.ShapeDtypeStruct((nb, 1, 4 * LANES), F32),
                   jax.ShapeDtypeStruct((nb, 2 * LANES, wb), F32), jax.ShapeDtypeStruct(state.shape, F32)],
        compiler_params=_cparams("arbitrary"),
        name="dec_attn",
    )(page_table, idx, q3, gates3, oc, kvn3, wn3, cache4, cwin, rq3, rk3, rv3, rg3, state, gl, ng)


def _permute_w_in(w):
    p0 = NSA_WIDTH
    p1 = p0 + 6 * KV_WIDTH
    p2 = p1 + 3 * N_NSA_HEADS
    wq = w[:, :p0].reshape(D_MODEL, N_KV, HPG, HEAD_DIM).transpose(0, 2, 1, 3).reshape(D_MODEL, p0)
    w_gates = jnp.pad(w[:, p1:p2], ((0, 0), (0, LANES - 3 * N_NSA_HEADS)))
    return jnp.concatenate([wq, w[:, p0:p1], w[:, p2:], w_gates], axis=1).astype(BF16)


def _permute_w_o(w):
    w_nsa = w[:NSA_WIDTH].reshape(N_KV, HPG, HEAD_DIM, D_MODEL).transpose(1, 0, 2, 3).reshape(NSA_WIDTH, D_MODEL)
    return jnp.concatenate([w_nsa, w[NSA_WIDTH:]], axis=0).astype(BF16)


def kernel(x_prompt, x_sample, cache_kv, cache_win, state_ret, page_table, w_in, w_cmp_k, w_cmp_v, pos_cmp_k, pos_cmp_v, ret_norm_g, w_o, ln1_g, ln1_b, w_up, w_down, ln2_g, ln2_b):
    depth = w_in.shape[0]
    assert depth == 1 and x_sample.shape[1] == 1
    batch, seq, _ = x_prompt.shape
    nb = x_sample.shape[0]
    n_pages = page_table.shape[1]
    past = n_pages * PAGE_SIZE
    assert past // SEL_BLOCK == N_SEL_BLOCKS and cache_win.shape[2] == WINDOW
    alpha = (2 * depth) ** 0.25
    log_gamma = jnp.log1p(-jnp.exp2(-5.0 - jnp.arange(N_RET_HEADS, dtype=F32)))

    w_perm = _permute_w_in(w_in[0])
    wo = _permute_w_o(w_o[0])
    wu, wd = w_up[0].astype(BF16), w_down[0].astype(BF16)
    vec = lambda a: a[0].reshape(1, -1)
    cmp_w = _compress_weights(w_cmp_k[0], w_cmp_v[0], pos_cmp_k[0], pos_cmp_v[0])
    tail_consts = (wo, vec(ln1_g), vec(ln1_b), wu, wd, vec(ln2_g), vec(ln2_b))
    norm_g = vec(ret_norm_g)

    tm = ROW_TILE
    pos_p = jnp.arange(seq, dtype=jnp.int32)
    tabs_p = _rope_tables(pos_p, ROPE_THETA, ROPE_DIMS) + _rope_tables(pos_p, RET_THETA, RET_DK)
    x2d = x_prompt.reshape(batch * seq, D_MODEL)
    kv4t, wint, rq, rk, rv, rg, qt, gt, kaug, vst, kw, vwt = _project(x2d, w_perm, tabs_p, tm=tm, seq=seq, prompt=True)
    seq_pages = seq // PAGE_SIZE
    pidx_p = jnp.broadcast_to(jnp.arange(batch, dtype=jnp.int32)[:, None], (batch, seq_pages))
    poff_p = jnp.broadcast_to(jnp.arange(seq_pages, dtype=jnp.int32)[None, :] * PAGE_SIZE, (batch, seq_pages))
    kcvc_p = _compress(pidx_p, poff_p, kv4t, *cmp_w)
    o_nsa = _nsa_prompt(qt, gt, kcvc_p, kaug, vst, kw, vwt, _importance_matrix(seq // CMP_STRIDE).T.astype(BF16), batch=batch, seq=seq)
    ret, st_p = _retention_prompt(rq, rk, rv, rg, _retention_tables(log_gamma, RET_CHUNK), norm_g, batch=batch, seq=seq)
    y_p = _tail(x2d, o_nsa, ret, *tail_consts, tm=tm, alpha=alpha)

    y_prompt = y_p.reshape(batch, seq, D_MODEL)
    to_rows = lambda a, n_slot: a.reshape(a.shape[0], n_slot, N_KV, HEAD_DIM, a.shape[-1]).transpose(0, 4, 1, 2, 3)[None]
    kv_prompt = to_rows(kv4t, 4)
    wlen = min(WINDOW, seq)
    win_prompt = to_rows(wint[:, :, seq - wlen:], 2)
    st6 = st_p.reshape(batch, N_PAIR, 2, RET_DK, 2, RET_DV)
    ret_prompt = jnp.stack([st6[:, :, 0, :, 0, :], st6[:, :, 1, :, 1, :]], axis=2).reshape(1, batch, N_RET_HEADS, RET_DK, RET_DV)

    pos_s = jnp.full((nb,), past, dtype=jnp.int32)
    tabs_s = _rope_tables(pos_s, ROPE_THETA, ROPE_DIMS) + _rope_tables(pos_s, RET_THETA, RET_DK)
    xs2d = x_sample.reshape(nb, D_MODEL)
    kv4_d, win_d, rq_d, rk_d, rv_d, rg_d, q_d, gates_d = _project(xs2d, w_perm, tabs_s, tm=nb, seq=seq, prompt=False)
    cache_t = cache_kv[0].transpose(0, 2, 3, 4, 1).reshape(cache_kv.shape[1], 4 * KV_WIDTH, PAGE_SIZE)
    cwin_t = cache_win[0].transpose(0, 2, 3, 4, 1).reshape(nb, 2 * KV_WIDTH, WINDOW)
    kcvc_d = _compress(page_table, jnp.zeros_like(page_table), cache_t, *cmp_w)
    row3 = lambda a: a.reshape(nb, 1, a.shape[-1])
    q3 = row3(q_d)
    oc_d, idx_d = _dec_score(q3, kcvc_d, _importance_matrix(past // CMP_STRIDE).astype(BF16), past=past)
    idx = idx_d[:, :N_KV, :TOP_N]
    state = state_ret[0].astype(F32).reshape(nb, N_PAIR, 2 * RET_DK, RET_DV)
    gl = jnp.repeat(jnp.exp(log_gamma), RET_DK)[None, :]
    nsa_d, ret_d, win_st, st_d = _dec_attn(page_table, idx, q3, row3(gates_d), oc_d, row3(kv4_d), row3(win_d), cache_t, cwin_t,
                                          row3(rq_d), row3(rk_d), row3(rv_d), row3(rg_d), state, gl, norm_g)
    y_s = _tail(xs2d, nsa_d.reshape(nb, NSA_WIDTH).astype(BF16), ret_d.reshape(nb, RET_WIDTH).astype(BF16), *tail_consts, tm=nb, alpha=alpha)

    y_sample = y_s.reshape(nb, 1, D_MODEL)
    kv_sample = kv4_d.reshape(1, nb, 1, 4, N_KV, HEAD_DIM)
    win_sample = to_rows(win_st, 2)
    ret_sample = st_d.reshape(1, nb, N_RET_HEADS, RET_DK, RET_DV)
    return (y_prompt, y_sample, kv_prompt, kv_sample, win_prompt, win_sample, ret_prompt, ret_sample)
```

```python
import functools

import jax
import jax.numpy as jnp
import numpy as np
from jax import lax
from jax.experimental import pallas as pl
from jax.experimental.pallas import tpu as pltpu

F32 = jnp.float32
BF16 = jnp.bfloat16

D_MODEL = 1024
HEAD_DIM = 64
N_NSA_HEADS = 8
N_KV = 2
HPG = N_NSA_HEADS // N_KV
N_RET_HEADS = 8
RET_DK = 64
RET_DV = 64
NSA_WIDTH = N_NSA_HEADS * HEAD_DIM
RET_WIDTH = N_RET_HEADS * RET_DV
KV_WIDTH = N_KV * HEAD_DIM
D_FF = 4 * D_MODEL
CMP_STRIDE = 16
CMP_BLOCK = 32
SEL_BLOCK = 64
TOP_N = 16
N_FORCED = 3
WINDOW = 512
RET_CHUNK = 128
RET_CHUNKS_PER_STEP = 4
PAGE_SIZE = 128
ROPE_THETA = 500000.0
ROPE_DIMS = HEAD_DIM // 4
RET_THETA = 10000.0
LN_EPS = 1e-5
FORCE_BONUS = 1e4
NEG = -1e30
BELOW_NEG = -3e38

LANES = 128
SUBLANES = 8
VMEM_LIMIT = 56 * 1024 * 1024

LOG2E = 1.4426950408889634
BF16_SUBLANES = 16
ONES_ROWS = BF16_SUBLANES
CMP_VARIANT_ROWS = 128
CHUNK_PITCH = 20
PAGES_PER_STEP = 4
Q_TILE = 128
ROW_TILE = 512
KEY_TILE = 1024
assert KEY_TILE % ROW_TILE == 0
N_PAIR = N_RET_HEADS // 2
N_SEL_BLOCKS = 128

U_Q, U_KV, U_RQ, U_RK, U_RV, U_RG, U_G, U_TOTAL = 0, 4, 10, 14, 18, 22, 26, 27
HEAD_PERM = (0, 4, 1, 5, 2, 6, 3, 7)


def _dot(a, b):
    return jnp.dot(a, b, preferred_element_type=F32)


def _dot_nt(a, b):
    return lax.dot_general(a, b, (((1,), (1,)), ((), ())), preferred_element_type=F32)


def _cparams(*sem):
    return pltpu.CompilerParams(dimension_semantics=sem, vmem_limit_bytes=VMEM_LIMIT)


def _const_spec(shape):
    nd = len(shape)
    return pl.BlockSpec(shape, lambda *_: (0,) * nd)


def _rope(u, c, s_fwd, s_bwd, half):
    return u * c + pltpu.roll(u, LANES - half, 1) * s_fwd + pltpu.roll(u, half, 1) * s_bwd


def _proj_kernel(x_ref, w_ref, cq_ref, sqf_ref, sqb_ref, cr_ref, srf_ref, srb_ref, *outs, tm, seq, prompt):
    if prompt:
        kv4t_o, wint_o, rq_o, rk_o, rv_o, rg_o, qt_o, gt_o, kaug_o, vst_o, kw_o, vwt_o = outs
    else:
        kv4_o, win_o, rq_o, rk_o, rv_o, rg_o, q_o, g_o = outs
    xb = x_ref[...].astype(BF16)

    def seg(u0, n):
        return _dot(xb, w_ref[:, u0 * LANES:(u0 + n) * LANES])

    def unit(z, u):
        return z[:, u * LANES:(u + 1) * LANES]

    cq, sqf, sqb = cq_ref[...], sqf_ref[...], sqb_ref[...]
    cr, srf, srb = cr_ref[...], srf_ref[...], srb_ref[...]
    nsa_half, ret_half = ROPE_DIMS // 2, RET_DK // 2

    zq = seg(U_Q, 4)
    q = jnp.concatenate([_rope(unit(zq, c), cq, sqf, sqb, nsa_half) for c in range(4)], axis=1) * (HEAD_DIM ** -0.5)
    gates = jax.nn.sigmoid(seg(U_G, 1))
    zkv = seg(U_KV, 6)
    k_slc = _rope(unit(zkv, 2), cq, sqf, sqb, nsa_half)
    k_win = _rope(unit(zkv, 4), cq, sqf, sqb, nsa_half)
    kv4 = jnp.concatenate([zkv[:, 0:2 * LANES], k_slc, unit(zkv, 3)], axis=1)
    win = jnp.concatenate([k_win, unit(zkv, 5)], axis=1)
    if prompt:
        kv4t = jnp.transpose(kv4)
        wint = jnp.transpose(win)
        kv4t_o[...] = kv4t
        wint_o[...] = wint
        qt_o[...] = jnp.transpose(q * LOG2E).astype(BF16)
        gt_o[...] = jnp.transpose(gates)
        row = pl.program_id(0) * tm + lax.broadcasted_iota(jnp.int32, (tm, 1), 0)
        blk = (row & (seq - 1)) >> 6
        onehot = lax.broadcasted_iota(jnp.int32, (tm, LANES), 1) == blk
        kaug_o[:, 0:LANES] = k_slc.astype(BF16)
        kaug_o[:, LANES:2 * LANES] = jnp.where(onehot, 1.0, 0.0).astype(BF16)
        vst_o[0:LANES, :] = kv4t[3 * LANES:4 * LANES].astype(BF16)
        vst_o[LANES:, :] = jnp.ones((ONES_ROWS, tm), BF16)
        kw_o[...] = k_win.astype(BF16)
        vwt = wint[LANES:2 * LANES].astype(BF16)
        for j in range(tm // LANES):
            vwt_o[j] = vwt[:, j * LANES:(j + 1) * LANES]
    else:
        kv4_o[...] = kv4
        win_o[...] = win
        q_o[...] = q.astype(BF16).astype(F32)
        g_o[...] = gates

    zrq = seg(U_RQ, 4)
    zrk = seg(U_RK, 4)
    for c in range(4):
        sl = slice(c * LANES, (c + 1) * LANES)
        rq_o[:, sl] = _rope(unit(zrq, c), cr, srf, srb, ret_half)
        rk_o[:, sl] = _rope(unit(zrk, c), cr, srf, srb, ret_half) * (RET_DK ** -0.5)
    rv_o[...] = seg(U_RV, 4)
    rg_o[...] = seg(U_RG, 4)


def _rope_tables(pos, theta, n_rot):
    half = n_rot // 2
    inv = theta ** (-np.arange(half, dtype=np.float64) / half)
    ang = np.asarray(pos, dtype=np.float64)[:, None] * inv[None, :]
    cos, sin = np.cos(ang), np.sin(ang)
    n = len(pos)
    rest = HEAD_DIM - n_rot
    c = np.concatenate([cos, cos, np.ones((n, rest))], 1)
    s_fwd = np.concatenate([-sin, np.zeros((n, half + rest))], 1)
    s_bwd = np.concatenate([np.zeros((n, half)), sin, np.zeros((n, rest))], 1)
    rep = LANES // HEAD_DIM
    return tuple(jnp.asarray(np.tile(t, (1, rep)), dtype=F32) for t in (c, s_fwd, s_bwd))


def _project(x2d, w_perm, tables, *, tm, seq, prompt):
    rows = x2d.shape[0]
    n_tab_tiles = tables[0].shape[0] // tm
    row_spec = lambda w: pl.BlockSpec((tm, w), lambda i: (i, 0))
    tab_spec = pl.BlockSpec((tm, LANES), lambda i: (i % n_tab_tiles, 0))
    if prompt:
        assert tm == ROW_TILE and seq % KEY_TILE == 0
        batch, tpb = rows // seq, seq // tm
        feat_spec = lambda f: pl.BlockSpec((None, f, tm), lambda i: (i // tpb, 0, i % tpb))
        outs = [(feat_spec(4 * LANES), (batch, 4 * LANES, seq), F32), (feat_spec(2 * LANES), (batch, 2 * LANES, seq), F32)]
    else:
        outs = [(row_spec(4 * LANES), (rows, 4 * LANES), F32), (row_spec(2 * LANES), (rows, 2 * LANES), F32)]
    outs += [(row_spec(4 * LANES), (rows, 4 * LANES), F32)] * 4
    if prompt:
        outs += [(feat_spec(4 * LANES), (batch, 4 * LANES, seq), BF16),
                 (feat_spec(LANES), (batch, LANES, seq), F32),
                 (row_spec(2 * LANES), (rows, 2 * LANES), BF16),
                 (pl.BlockSpec((None, LANES + ONES_ROWS, tm), lambda i: (i, 0, 0)), (rows // tm, LANES + ONES_ROWS, tm), BF16),
                 (row_spec(LANES), (rows, LANES), BF16),
                 (pl.BlockSpec((tm // LANES, LANES, LANES), lambda i: (i, 0, 0)), (rows // LANES, LANES, LANES), BF16)]
    else:
        outs += [(row_spec(4 * LANES), (rows, 4 * LANES), F32), (row_spec(LANES), (rows, LANES), F32)]
    return pl.pallas_call(
        functools.partial(_proj_kernel, tm=tm, seq=seq, prompt=prompt),
        grid=(rows // tm,),
        in_specs=[row_spec(D_MODEL), _const_spec(w_perm.shape)] + [tab_spec] * 6,
        out_specs=[s for s, _, _ in outs],
        out_shape=[jax.ShapeDtypeStruct(shape, dt) for _, shape, dt in outs],
        compiler_params=_cparams("parallel"),
        name="proj_prompt" if prompt else "proj_decode",
    )(x2d, w_perm, *tables)


def _cmp_kernel(pn_ref, po_ref, src_ref, wlo_ref, whi_ref, plo_ref, phi_ref, out_ref,
                stage, slab, sem, *, n_pages):
    b = pl.program_id(0)
    nb = pl.num_programs(0)
    cur = b % 2
    nxt = 1 - cur
    chunks_per_page = PAGE_SIZE // CMP_STRIDE
    page_rows = chunks_per_page * CHUNK_PITCH
    n_chunks = n_pages * chunks_per_page
    assert n_pages % CMP_STRIDE == 0
    pages_per_r = n_pages // CMP_STRIDE

    def page_copy(bb, sl, p, kv):
        return pltpu.make_async_copy(
            src_ref.at[pn_ref[bb, p], pl.ds(kv * LANES, LANES), pl.ds(pl.multiple_of(po_ref[bb, p], PAGE_SIZE), PAGE_SIZE)],
            stage.at[sl, kv, p],
            sem.at[sl])

    def start_all(bb, sl):
        for p in range(n_pages):
            for kv in range(2):
                page_copy(bb, sl, p, kv).start()

    def wait_all(bb, sl):
        for p in range(n_pages):
            for kv in range(2):
                page_copy(bb, sl, p, kv).wait()

    def transpose_pages(sl, first, count):
        for u in range(count):
            p = first + u
            r0 = pl.multiple_of(p * page_rows, SUBLANES)
            for kv in range(2):
                x = jnp.transpose(stage[sl, kv, p])
                for c in range(chunks_per_page):
                    slab[sl, kv, pl.ds(r0 + c * CHUNK_PITCH, CMP_STRIDE), :] = x[c * CMP_STRIDE:(c + 1) * CMP_STRIDE]

    def compress_rows(with_next):
        acc_lo = jnp.zeros((n_chunks, 2 * LANES), F32)
        acc_hi = jnp.zeros((n_chunks, 2 * LANES), F32)
        for r in range(CMP_STRIDE):
            xr = jnp.concatenate([slab[cur, kv, pl.ds(r, n_chunks, stride=CHUNK_PITCH), :] for kv in range(2)], axis=1)
            acc_lo = acc_lo + _dot((xr + plo_ref[r:r + 1, :]).astype(BF16), wlo_ref[r])
            acc_hi = acc_hi + _dot((xr + phi_ref[r:r + 1, :]).astype(BF16), whi_ref[r])
            if with_next:
                transpose_pages(nxt, r * pages_per_r, pages_per_r)
        shifted = pltpu.roll(acc_hi, n_chunks - 1, 0)
        last = lax.broadcasted_iota(jnp.int32, (n_chunks, 1), 0) == n_chunks - 1
        out_ref[...] = jnp.where(last, 0.0, acc_lo + shifted)

    @pl.when(b == 0)
    def _():
        start_all(0, 0)

        @pl.when(nb > 1)
        def _():
            start_all(1, 1)

        wait_all(0, 0)

        def first_rows(pp, carry):
            transpose_pages(0, pp * PAGES_PER_STEP, PAGES_PER_STEP)
            return carry

        lax.fori_loop(0, n_pages // PAGES_PER_STEP, first_rows, 0)

    @pl.when(b + 2 < nb)
    def _():
        start_all(b + 2, cur)

    @pl.when(b + 1 < nb)
    def _():
        wait_all(b + 1, nxt)
        compress_rows(True)

    @pl.when(b + 1 >= nb)
    def _():
        compress_rows(False)


def _compress(page_index, page_offset, src, wlo, whi, plo, phi):
    nb, n_pages = page_index.shape
    n_chunks = n_pages * PAGE_SIZE // CMP_STRIDE
    grid_spec = pltpu.PrefetchScalarGridSpec(
        num_scalar_prefetch=2,
        grid=(nb,),
        in_specs=[pl.BlockSpec(memory_space=pl.ANY),
                  pl.BlockSpec(wlo.shape, lambda b, *_: (0, 0, 0)),
                  pl.BlockSpec(whi.shape, lambda b, *_: (0, 0, 0)),
                  pl.BlockSpec(plo.shape, lambda b, *_: (0, 0)),
                  pl.BlockSpec(phi.shape, lambda b, *_: (0, 0))],
        out_specs=pl.BlockSpec((None, n_chunks, 2 * LANES), lambda b, *_: (b, 0, 0)),
        scratch_shapes=[pltpu.VMEM((2, 2, n_pages, LANES, PAGE_SIZE), F32),
                        pltpu.VMEM((2, 2, n_chunks * CHUNK_PITCH, LANES), F32),
                        pltpu.SemaphoreType.DMA((2,))],
    )
    return pl.pallas_call(
        functools.partial(_cmp_kernel, n_pages=n_pages),
        grid_spec=grid_spec,
        out_shape=jax.ShapeDtypeStruct((nb, n_chunks, 2 * LANES), F32),
        compiler_params=_cparams("arbitrary"),
        name="compress",
    )(page_index, page_offset, src, wlo, whi, plo, phi)


def _compress_weights(w_k, w_v, pos_k, pos_v):
    def bd(wk_r, wv_r):
        z = jnp.zeros((HEAD_DIM, HEAD_DIM), F32)
        rows = [[wk_r, z, z, z], [z, wk_r, z, z], [z, z, wv_r, z], [z, z, z, wv_r]]
        return jnp.block(rows)

    w_all = jax.vmap(bd)(w_k, w_v).astype(BF16)
    p_all = jnp.concatenate([pos_k, pos_k, pos_v, pos_v], axis=1)
    return w_all[:CMP_STRIDE], w_all[CMP_STRIDE:], p_all[:CMP_STRIDE], p_all[CMP_STRIDE:]


def _masked_softmax(s, mask):
    s = jnp.where(mask, s, NEG)
    m = jnp.max(s, -1, keepdims=True)
    p = jnp.exp(s - m) * mask.astype(F32)
    return p / jnp.maximum(jnp.sum(p, -1, keepdims=True), 1e-30)


def _split3(x):
    hi = x.astype(BF16)
    r1 = x - hi.astype(F32)
    mid = r1.astype(BF16)
    lo = (r1 - mid.astype(F32)).astype(BF16)
    return hi, mid, lo


def _importance_matrix(n_cmp):
    n = np.arange(n_cmp)[:, None]
    j = np.arange(N_SEL_BLOCKS)[None, :]
    inner = (n >= 4 * j) & (n <= 4 * j + 2)
    edge = (n == 4 * j - 1) | (n == 4 * j + 3)
    return np.where(inner, 2.0, np.where(edge, 1.0, 0.0))


def _softmax_over_keys(s, mask):
    s = jnp.where(mask, s, NEG)
    m = jnp.max(s, 0, keepdims=True)
    p = jnp.exp2(s - m)
    inv = jnp.where(m > 0.5 * NEG, 1.0 / jnp.maximum(jnp.sum(p, 0, keepdims=True), 1e-30), 0.0)
    return p * inv


def _nsa_kernel(qt_ref, gt_ref, kcvc_ref, kaug_ref, vst_ref, kw_ref, vwt_ref, mt_ref, out_ref,
                kc_s, vct_s, m_s, acc_s, s_buf, tmax_s, qaug_s, ocw_s):
    i = pl.program_id(1)
    t0 = i * Q_TILE
    n_cmp = kcvc_ref.shape[0]

    @pl.when(i == 0)
    def _():
        kcvc = kcvc_ref[...]
        kc_s[...] = kcvc[:, 0:LANES].astype(BF16)
        vct_s[...] = jnp.transpose(kcvc[:, LANES:2 * LANES]).astype(BF16)

    lane = lax.broadcasted_iota(jnp.int32, (1, Q_TILE), 1)
    tq = jnp.concatenate([t0 + lane] * N_NSA_HEADS, axis=1)
    low_rows = lax.broadcasted_iota(jnp.int32, (LANES, 1), 0) < HEAD_DIM
    qt = qt_ref[...]
    zero = jnp.zeros((), BF16)
    q_all = jnp.concatenate([jnp.where(low_rows if g == 0 else ~low_rows, qt[c * LANES:(c + 1) * LANES], zero)
                             for g in range(N_KV) for c in range(HPG)], axis=1)

    qaug_s[0:LANES, :] = q_all
    gt = gt_ref[...]
    gk = [jnp.concatenate([gt[3 * h + k:3 * h + k + 1] for h in range(N_NSA_HEADS)], axis=1) for k in range(3)]
    cur = jnp.concatenate([(t0 + lane) >> 6] * N_KV, axis=1)

    def compressed_and_select(n_act, j_act):
        cend = lax.broadcasted_iota(jnp.int32, (n_act, 1), 0) * CMP_STRIDE + (CMP_BLOCK - 1)
        p_c = _softmax_over_keys(_dot(kc_s[0:n_act, :], q_all), cend <= tq)
        ocw_s[...] = gk[0] * _dot(vct_s[:, 0:n_act], p_c.astype(BF16))
        psum = jnp.concatenate([sum(p_c[:, (g * HPG + c) * Q_TILE:(g * HPG + c + 1) * Q_TILE] for c in range(HPG))
                                for g in range(N_KV)], axis=1)
        mt = mt_ref[0:j_act, 0:n_act]
        imp_t = sum(_dot(mt, part) for part in _split3(psum))
        blk = lax.broadcasted_iota(jnp.int32, (j_act, 1), 0)
        blk_f = blk.astype(F32)
        forced = (blk == 0) | (blk == cur) | (blk == cur - 1)
        score = jnp.where(forced, BELOW_NEG, jnp.where(blk <= cur, imp_t, NEG))
        bias = jnp.where(forced, 0.0, NEG)
        for _ in range(TOP_N - N_FORCED):
            best = jnp.max(score, axis=0, keepdims=True)
            first = jnp.min(jnp.where(score == best, blk_f, float(N_SEL_BLOCKS)), axis=0, keepdims=True)
            pick = blk_f == first
            bias = jnp.where(pick, 0.0, bias)
            score = jnp.where(pick, BELOW_NEG, score)
        bias = bias.astype(BF16)
        qaug_s[LANES:LANES + j_act, :] = jnp.concatenate(
            [bias[:, g * Q_TILE:(g + 1) * Q_TILE] for g in range(N_KV) for _ in range(HPG)], axis=1)
        if j_act < N_SEL_BLOCKS:
            qaug_s[LANES + j_act:, :] = jnp.full((N_SEL_BLOCKS - j_act, qaug_s.shape[1]), NEG, BF16)

    q_blocks_per_variant = CMP_VARIANT_ROWS * CMP_STRIDE // Q_TILE
    n_variants = n_cmp // CMP_VARIANT_ROWS
    for v in range(n_variants):
        @pl.when(i // q_blocks_per_variant == v)
        def _(v=v):
            compressed_and_select((v + 1) * CMP_VARIANT_ROWS, min(N_SEL_BLOCKS, (v + 1) * CMP_VARIANT_ROWS * CMP_STRIDE // SEL_BLOCK))

    m_s[...] = jnp.full(m_s.shape, NEG, F32)
    acc_s[...] = jnp.zeros(acc_s.shape, F32)

    def scores(kt, causal):
        k0 = pl.multiple_of(kt * KEY_TILE, KEY_TILE)
        s = _dot(kaug_ref[pl.ds(k0, KEY_TILE), :], qaug_s[...])
        if causal:
            kpos = k0 + lax.broadcasted_iota(jnp.int32, (KEY_TILE, 1), 0)
            s = jnp.where(kpos <= tq, s, NEG)
        s_buf[...] = s
        tmax_s[...] = jnp.max(s, 0, keepdims=True)

    def accumulate(kt):
        m_old = m_s[...]
        m_new = jnp.maximum(m_old, tmax_s[...])
        alpha = jnp.exp2(m_old - m_new)
        p = jnp.exp2(s_buf[...] - m_new).astype(BF16)
        n_sub = KEY_TILE // ROW_TILE
        pv = sum(_dot(vst_ref[kt * n_sub + j], p[j * ROW_TILE:(j + 1) * ROW_TILE]) for j in range(n_sub))
        acc_s[...] = alpha * acc_s[...] + pv
        m_s[...] = m_new

    last = (t0 + Q_TILE - 1) // KEY_TILE

    @pl.when(last == 0)
    def _():
        scores(0, True)

    @pl.when(last > 0)
    def _():
        scores(0, False)

        def body(kt, carry):
            accumulate(kt)
            scores(kt + 1, False)
            return carry

        lax.fori_loop(0, last - 1, body, 0)
        accumulate(last - 1)
        scores(last, True)

    accumulate(last)
    o_s = acc_s[0:LANES, :] * (1.0 / acc_s[LANES:LANES + 1, :])

    w0 = pl.multiple_of(jnp.maximum(t0 - WINDOW, 0), Q_TILE)
    n_win = WINDOW + Q_TILE
    dist = tq - (w0 + lax.broadcasted_iota(jnp.int32, (n_win, 1), 0))
    p_w = _softmax_over_keys(_dot(kw_ref[pl.ds(w0, n_win), :], q_all), (dist >= 0) & (dist < WINDOW))
    w_blk = w0 // LANES
    vwt = jnp.concatenate([vwt_ref[w_blk + j] for j in range(n_win // LANES)], axis=1)
    o_w = _dot(vwt, p_w.astype(BF16))

    o = ocw_s[...] + gk[2] * o_w + gk[1] * o_s
    half = HPG * Q_TILE
    comb = jnp.where(low_rows, o[:, 0:half], o[:, half:2 * half])
    for c in range(HPG):
        out_ref[:, c * LANES:(c + 1) * LANES] = jnp.transpose(comb[:, c * Q_TILE:(c + 1) * Q_TILE]).astype(BF16)


def _nsa_prompt(qt, gt, kcvc, kaug, vst, kw, vwt, mt, *, batch, seq):
    nq = seq // Q_TILE
    n_cols = N_NSA_HEADS * Q_TILE
    n_cmp = kcvc.shape[1]
    col_spec = lambda f: pl.BlockSpec((None, f, Q_TILE), lambda b, i: (b, 0, i))
    seq_spec = lambda w: pl.BlockSpec((seq, w), lambda b, i: (b, 0))
    return pl.pallas_call(
        _nsa_kernel,
        grid=(batch, nq),
        in_specs=[col_spec(4 * LANES), col_spec(LANES),
                  pl.BlockSpec((None, n_cmp, 2 * LANES), lambda b, i: (b, 0, 0)),
                  seq_spec(2 * LANES),
                  pl.BlockSpec((seq // ROW_TILE, LANES + ONES_ROWS, ROW_TILE), lambda b, i: (b, 0, 0)),
                  seq_spec(LANES),
                  pl.BlockSpec((seq // LANES, LANES, LANES), lambda b, i: (b, 0, 0)),
                  _const_spec(mt.shape)],
        out_specs=pl.BlockSpec((Q_TILE, 4 * LANES), lambda b, i: (b * nq + i, 0)),
        out_shape=jax.ShapeDtypeStruct((batch * seq, 4 * LANES), BF16),
        scratch_shapes=[pltpu.VMEM((n_cmp, LANES), BF16), pltpu.VMEM((LANES, n_cmp), BF16),
                        pltpu.VMEM((1, n_cols), F32), pltpu.VMEM((LANES + ONES_ROWS, n_cols), F32),
                        pltpu.VMEM((KEY_TILE, n_cols), F32), pltpu.VMEM((1, n_cols), F32),
                        pltpu.VMEM((2 * LANES, n_cols), BF16), pltpu.VMEM((LANES, n_cols), F32)],
        compiler_params=_cparams("parallel", "arbitrary"),
        name="nsa_prompt",
    )(qt, gt, kcvc, kaug, vst, kw, vwt, mt)


def _head_sum(a, low):
    s0 = jnp.sum(jnp.where(low, a, 0.0), -1, keepdims=True)
    s1 = jnp.sum(jnp.where(low, 0.0, a), -1, keepdims=True)
    return jnp.where(low, s0, s1)


def _gated_group_norm(o, rg, norm_g, low):
    mu = _head_sum(o, low) * (1.0 / RET_DV)
    d = o - mu
    var = _head_sum(d * d, low) * (1.0 / RET_DV)
    gn = d * lax.rsqrt(var + LN_EPS) * norm_g
    return (rg * jax.nn.sigmoid(rg)) * gn


def _ret_kernel(rq_ref, rk_ref, rv_ref, rg_ref, decay_ref, qdec_ref, kdec_ref, gcl_ref, bd_ref, ng_ref,
                ret_o, st_o, st_s):
    c = pl.program_id(1)

    @pl.when(c == 0)
    def _():
        st_s[...] = jnp.zeros(st_s.shape, F32)

    low = lax.broadcasted_iota(jnp.int32, (1, LANES), 1) < RET_DK
    zero = jnp.zeros((), BF16)
    for j in range(RET_CHUNKS_PER_STEP):
        rows = slice(j * RET_CHUNK, (j + 1) * RET_CHUNK)
        for p in range(N_PAIR):
            sl = slice(p * LANES, (p + 1) * LANES)
            q, k, v = rq_ref[rows, sl], rk_ref[rows, sl], rv_ref[rows, sl]
            qb, kb, vb = q.astype(BF16), k.astype(BF16), v.astype(BF16)
            in0 = _dot_nt(jnp.where(low, qb, zero), kb) * decay_ref[2 * p]
            in1 = _dot_nt(jnp.where(low, zero, qb), kb) * decay_ref[2 * p + 1]
            inner = jnp.concatenate([in0.astype(BF16), in1.astype(BF16)], axis=1)
            v_split = jnp.concatenate([jnp.where(low, vb, zero), jnp.where(low, zero, vb)], axis=0)
            state = st_s[p]
            o = _dot(inner, v_split) + _dot((q * qdec_ref[:, sl]).astype(BF16), state.astype(BF16))
            kd_t = jnp.transpose(k * kdec_ref[:, sl]).astype(BF16)
            st_s[p] = state * gcl_ref[:, sl] + _dot(kd_t, vb) * bd_ref[...]
            ret_o[rows, sl] = _gated_group_norm(o, rg_ref[rows, sl], ng_ref[:, sl], low).astype(BF16)

    @pl.when(c == pl.num_programs(1) - 1)
    def _():
        st_o[...] = st_s[...]


def _log_gamma():
    return np.log1p(-np.exp2(-5.0 - np.arange(N_RET_HEADS, dtype=np.float64)))


def _retention_tables(chunk):
    log_gamma = _log_gamma()
    i = np.arange(chunk, dtype=np.float64)
    diff = i[:, None] - i[None, :]
    decay = np.where(diff >= 0, np.exp(log_gamma[:, None, None] * np.maximum(diff, 0.0)), 0.0)
    per_lane = lambda t: np.repeat(t, RET_DK, axis=-1)
    qdec = per_lane(np.exp((i + 1.0)[:, None] * log_gamma[None, :]))
    kdec = per_lane(np.exp((chunk - 1.0 - i)[:, None] * log_gamma[None, :]))
    gcl = per_lane(np.exp(chunk * log_gamma)[None, :])
    r = np.arange(LANES)
    bd = (r[:, None] // RET_DK) == (r[None, :] // RET_DV)
    return tuple(jnp.asarray(t, dtype=F32) for t in (decay, qdec, kdec, gcl, bd))


def _retention_prompt(rq, rk, rv, rg, tables, norm_g, *, batch, seq):
    decay, qdec, kdec, gcl, bd = tables
    step_rows = RET_CHUNK * RET_CHUNKS_PER_STEP
    nc = seq // step_rows
    row_spec = pl.BlockSpec((step_rows, RET_WIDTH), lambda b, c: (b * nc + c, 0))
    return pl.pallas_call(
        _ret_kernel,
        grid=(batch, nc),
        in_specs=[row_spec] * 4 + [_const_spec(decay.shape), _const_spec(qdec.shape), _const_spec(kdec.shape),
                                   _const_spec(gcl.shape), _const_spec(bd.shape), _const_spec(norm_g.shape)],
        out_specs=[row_spec, pl.BlockSpec((None, N_PAIR, LANES, LANES), lambda b, c: (b, 0, 0, 0))],
        out_shape=[jax.ShapeDtypeStruct((batch * seq, RET_WIDTH), BF16),
                   jax.ShapeDtypeStruct((batch, N_PAIR, LANES, LANES), F32)],
        scratch_shapes=[pltpu.VMEM((N_PAIR, LANES, LANES), F32)],
        compiler_params=_cparams("parallel", "arbitrary"),
        name="retention",
    )(rq, rk, rv, rg, decay, qdec, kdec, gcl, bd, norm_g)


def _layer_norm(v, g, b):
    mu = jnp.mean(v, -1, keepdims=True)
    d = v - mu
    var = jnp.mean(d * d, -1, keepdims=True)
    return d * lax.rsqrt(var + LN_EPS) * g + b


def _tail_kernel(x_ref, nsa_ref, ret_ref, wo_ref, g1_ref, b1_ref, wu_ref, wd_ref, g2_ref, b2_ref, y_ref, *, alpha):
    mix = _dot(nsa_ref[...], wo_ref[0:NSA_WIDTH, :]) + _dot(ret_ref[...], wo_ref[NSA_WIDTH:, :])
    x1 = _layer_norm(alpha * x_ref[...] + mix, g1_ref[...], b1_ref[...])
    x1b = x1.astype(BF16)
    f = jnp.zeros(x1.shape, F32)
    for j in range(D_FF // D_MODEL):
        sl = slice(j * D_MODEL, (j + 1) * D_MODEL)
        h = jnp.maximum(_dot(x1b, wu_ref[:, sl]), 0.0)
        f = f + _dot((h * h).astype(BF16), wd_ref[sl, :])
    y_ref[...] = _layer_norm(alpha * x1 + f, g2_ref[...], b2_ref[...])


def _tail(x2d, nsa, ret, wo, g1, b1, wu, wd, g2, b2, *, tm, alpha):
    rows = x2d.shape[0]
    row_spec = lambda w: pl.BlockSpec((tm, w), lambda i: (i, 0))
    consts = (wo, g1, b1, wu, wd, g2, b2)
    return pl.pallas_call(
        functools.partial(_tail_kernel, alpha=alpha),
        grid=(rows // tm,),
        in_specs=[row_spec(D_MODEL), row_spec(NSA_WIDTH), row_spec(RET_WIDTH)] + [_const_spec(c.shape) for c in consts],
        out_specs=row_spec(D_MODEL),
        out_shape=jax.ShapeDtypeStruct((rows, D_MODEL), F32),
        compiler_params=_cparams("parallel"),
        name="tail",
    )(x2d, nsa, ret, *consts)


DEC_ROWS = 16
N_DEC_SEL = TOP_N - 1


def _dec_stack_q(q, low):
    rows = lax.broadcasted_iota(jnp.int32, (DEC_ROWS, 1), 0)
    qs = jnp.zeros((DEC_ROWS, LANES), F32)
    for g in range(N_KV):
        for c in range(HPG):
            piece = jnp.where(low if g == 0 else ~low, q[:, c * LANES:(c + 1) * LANES], 0.0)
            qs = jnp.where(rows == g * HPG + c, piece, qs)
    return qs, rows


def _dec_score_kernel(q_ref, kcvc_ref, m_ref, oc_o, idx_o, score_s, *, past):
    lane = lax.broadcasted_iota(jnp.int32, (1, LANES), 1)
    low = lane < HEAD_DIM
    qs, rows = _dec_stack_q(q_ref[...], low)
    qsb = qs.astype(BF16)
    kcvc = kcvc_ref[...]
    kc = kcvc[:, 0:LANES].astype(BF16)
    vc = kcvc[:, LANES:2 * LANES].astype(BF16)
    n_cmp = kcvc.shape[0]
    cend = lax.broadcasted_iota(jnp.int32, (1, n_cmp), 1) * CMP_STRIDE + (CMP_BLOCK - 1)
    p = _masked_softmax(_dot_nt(qsb, kc), jnp.broadcast_to(cend <= past, (DEC_ROWS, n_cmp)))
    oc_o[...] = _dot(p.astype(BF16), vc)

    psum = jnp.zeros((DEC_ROWS, n_cmp), F32)
    for g in range(N_KV):
        acc = p[g * HPG:g * HPG + 1]
        for c in range(1, HPG):
            acc = acc + p[g * HPG + c:g * HPG + c + 1]
        psum = jnp.where(rows == g, acc, psum)
    m = m_ref[...]
    imp = sum(_dot(part, m) for part in _split3(psum))
    forced = (lane == 0) | (lane == N_SEL_BLOCKS - 1)
    b = pl.program_id(0)
    score_s[b] = imp + FORCE_BONUS * forced.astype(F32)

    @pl.when(b == pl.num_programs(0) - 1)
    def _():
        lane_f = lane.astype(F32)
        score = score_s[...].reshape(-1, LANES)
        picked = jnp.zeros(score.shape, F32)
        for r in range(N_DEC_SEL):
            best = jnp.max(score, -1, keepdims=True)
            first = jnp.min(jnp.where(score == best, lane_f, float(LANES)), -1, keepdims=True)
            picked = jnp.where(lane == r, first, picked)
            score = jnp.where(lane_f == first, BELOW_NEG, score)
        idx_o[...] = picked.astype(jnp.int32).reshape(idx_o.shape)


def _dec_score(q3, kcvc, m, *, past):
    nb = q3.shape[0]
    n_cmp = kcvc.shape[1]
    return pl.pallas_call(
        functools.partial(_dec_score_kernel, past=past),
        grid=(nb,),
        in_specs=[pl.BlockSpec((None, 1, 4 * LANES), lambda b: (b, 0, 0)),
                  pl.BlockSpec((None, n_cmp, 2 * LANES), lambda b: (b, 0, 0)),
                  _const_spec(m.shape)],
        out_specs=[pl.BlockSpec((None, DEC_ROWS, LANES), lambda b: (b, 0, 0)),
                   _const_spec((nb, DEC_ROWS, LANES))],
        out_shape=[jax.ShapeDtypeStruct((nb, DEC_ROWS, LANES), F32), jax.ShapeDtypeStruct((nb, DEC_ROWS, LANES), jnp.int32)],
        scratch_shapes=[pltpu.VMEM((nb, DEC_ROWS, LANES), F32)],
        compiler_params=_cparams("arbitrary"),
        name="dec_score",
    )(q3, kcvc, m)


def _round_bf16(x):
    return x.astype(BF16).astype(F32)


def _dec_attn_kernel(pt_ref, idx_ref, q_ref, g_ref, oc_ref, kvn_ref, wn_ref, cache_ref, cw_ref,
                     rq_ref, rk_ref, rv_ref, rg_ref, st_ref, gl_ref, ng_ref,
                     nsa_o, ret_o, win_o, st_o, kbuf, sem):
    b = pl.program_id(0)
    nb = pl.num_programs(0)
    slot = b % 2

    def blk_copy(bb, sl, g, k):
        j = idx_ref[bb, g, k]
        return pltpu.make_async_copy(
            cache_ref.at[pt_ref[bb, j >> 1], pl.ds(2 * LANES, 2 * LANES), :],
            kbuf.at[sl, g, :, pl.ds(k * PAGE_SIZE, PAGE_SIZE)],
            sem.at[sl])

    def start_all(bb, sl):
        for g in range(N_KV):
            for k in range(N_DEC_SEL):
                blk_copy(bb, sl, g, k).start()

    @pl.when(b == 0)
    def _():
        start_all(0, 0)

    @pl.when(b + 1 < nb)
    def _():
        start_all(b + 1, 1 - slot)

    lane = lax.broadcasted_iota(jnp.int32, (1, LANES), 1)
    low = lane < HEAD_DIM
    qs, rows = _dec_stack_q(q_ref[...], low)
    qsb = qs.astype(BF16)
    g0_rows = rows < HPG

    cw = cw_ref[...]
    wb = cw.shape[1]
    wn = wn_ref[...]
    col = lax.broadcasted_iota(jnp.int32, (1, wb), 1)
    valid = jnp.broadcast_to((wb - col) < WINDOW, (DEC_ROWS, wb))
    s_w = jnp.where(valid, _dot(qsb, cw[0:LANES].astype(BF16)), NEG)
    s_wn = jnp.sum(qs * _round_bf16(wn[:, 0:LANES]), -1, keepdims=True)
    m_w = jnp.maximum(jnp.max(s_w, -1, keepdims=True), s_wn)
    p_w = jnp.exp(s_w - m_w) * valid.astype(F32)
    p_wn = jnp.exp(s_wn - m_w)
    l_w = jnp.maximum(jnp.sum(p_w, -1, keepdims=True) + p_wn, 1e-30)
    o_w = _dot_nt((p_w / l_w).astype(BF16), cw[LANES:2 * LANES].astype(BF16)) + _round_bf16(p_wn / l_w) * _round_bf16(wn[:, LANES:2 * LANES])
    new_cols = jnp.concatenate(
        [jnp.concatenate([jnp.transpose(jnp.broadcast_to(wn[:, u * LANES:(u + 1) * LANES], (LANES, LANES)))] * (wb // LANES), axis=1)
         for u in range(2)], axis=0)
    win_o[...] = jnp.where(col == wb - 1, new_cols, pltpu.roll(cw, wb - 1, 1))

    for g in range(N_KV):
        for k in range(N_DEC_SEL):
            blk_copy(b, slot, g, k).wait()
    kvn = kvn_ref[...]
    s_g = [_dot(qsb, kbuf[slot, g, 0:LANES, :].astype(BF16)) for g in range(N_KV)]
    pos = lax.broadcasted_iota(jnp.int32, (1, N_DEC_SEL * PAGE_SIZE), 1)
    picked_half = []
    for g in range(N_KV):
        half = jnp.zeros(pos.shape, jnp.int32)
        for k in range(N_DEC_SEL):
            half = jnp.where(pos >> 7 == k, idx_ref[b, g, k] & 1, half)
        picked_half.append(half)
    in_block = ((pos >> 6) & 1) == jnp.where(g0_rows, picked_half[0], picked_half[1])
    s_s = jnp.where(in_block, jnp.where(g0_rows, s_g[0], s_g[1]), NEG)
    s_sn = jnp.sum(qs * _round_bf16(kvn[:, 2 * LANES:3 * LANES]), -1, keepdims=True)
    m_sel = jnp.maximum(jnp.max(s_s, -1, keepdims=True), s_sn)
    p_s = jnp.exp(s_s - m_sel)
    p_sn = jnp.exp(s_sn - m_sel)
    l_sel = jnp.maximum(jnp.sum(p_s, -1, keepdims=True) + p_sn, 1e-30)
    pn = (p_s / l_sel).astype(BF16)
    o_g = [_dot_nt(pn, kbuf[slot, g, LANES:2 * LANES, :].astype(BF16)) for g in range(N_KV)]
    o_s = jnp.where(g0_rows, o_g[0], o_g[1]) + _round_bf16(p_sn / l_sel) * _round_bf16(kvn[:, 3 * LANES:4 * LANES])

    gates = g_ref[...]
    gcol = [jnp.zeros((DEC_ROWS, 1), F32) for _ in range(3)]
    for h in range(N_NSA_HEADS):
        for k in range(3):
            gcol[k] = jnp.where(rows == h, gates[:, 3 * h + k:3 * h + k + 1], gcol[k])
    o = gcol[0] * oc_ref[...] + gcol[1] * o_s + gcol[2] * o_w
    for c in range(HPG):
        nsa_o[:, c * LANES:(c + 1) * LANES] = jnp.where(low, o[c:c + 1], o[HPG + c:HPG + c + 1])

    top = lax.broadcasted_iota(jnp.int32, (LANES, 1), 0) < RET_DK
    for p in range(N_PAIR):
        sl = slice(p * LANES, (p + 1) * LANES)
        q, k, v, gl = rq_ref[:, sl], rk_ref[:, sl], rv_ref[:, sl], gl_ref[:, sl]
        state = st_ref[p]
        inner = _head_sum(q * k, low)

        def per_row(row):
            return jnp.transpose(jnp.broadcast_to(row, (LANES, LANES)))[:, 0:RET_DV]

        v_rows = jnp.where(top, v[:, 0:RET_DV], v[:, RET_DV:2 * RET_DV])
        qs_state = per_row(q * gl) * state
        cross = jnp.concatenate([jnp.sum(qs_state[0:RET_DK], axis=0, keepdims=True),
                                 jnp.sum(qs_state[RET_DK:2 * RET_DK], axis=0, keepdims=True)], axis=1)
        o_r = inner * v + cross
        st_o[p] = per_row(gl) * state + per_row(k) * v_rows
        ret_o[:, sl] = _gated_group_norm(o_r, rg_ref[:, sl], ng_ref[:, sl], low)


def _dec_attn(page_table, idx, q3, gates3, oc, kvn3, wn3, cache4, cwin, rq3, rk3, rv3, rg3, state, gl, ng):
    nb = q3.shape[0]
    wb = cwin.shape[2]
    row3 = lambda w: pl.BlockSpec((None, 1, w), lambda b, *_: (b, 0, 0))
    st_spec = pl.BlockSpec((None, N_PAIR, 2 * RET_DK, RET_DV), lambda b, *_: (b, 0, 0, 0))
    win_spec = pl.BlockSpec((None, 2 * LANES, wb), lambda b, *_: (b, 0, 0))
    grid_spec = pltpu.PrefetchScalarGridSpec(
        num_scalar_prefetch=2,
        grid=(nb,),
        in_specs=[row3(4 * LANES), row3(LANES), pl.BlockSpec((None, DEC_ROWS, LANES), lambda b, *_: (b, 0, 0)),
                  row3(4 * LANES), row3(2 * LANES), pl.BlockSpec(memory_space=pl.ANY), win_spec,
                  row3(4 * LANES), row3(4 * LANES), row3(4 * LANES), row3(4 * LANES), st_spec,
                  pl.BlockSpec(gl.shape, lambda b, *_: (0, 0)), pl.BlockSpec(ng.shape, lambda b, *_: (0, 0))],
        out_specs=[row3(4 * LANES), row3(4 * LANES), win_spec, st_spec],
        scratch_shapes=[pltpu.VMEM((2, N_KV, 2 * LANES, N_DEC_SEL * PAGE_SIZE), F32), pltpu.SemaphoreType.DMA((2,))],
    )
    return pl.pallas_call(
        _dec_attn_kernel,
        grid_spec=grid_spec,
        out_shape=[jax.ShapeDtypeStruct((nb, 1, 4 * LANES), F32), jax.ShapeDtypeStruct((nb, 1, 4 * LANES), F32),
                   jax.ShapeDtypeStruct((nb, 2 * LANES, wb), F32), jax.ShapeDtypeStruct(state.shape, F32)],
        compiler_params=_cparams("arbitrary"),
        name="dec_attn",
    )(page_table, idx, q3, gates3, oc, kvn3, wn3, cache4, cwin, rq3, rk3, rv3, rg3, state, gl, ng)


def _permute_w_in(w):
    p0 = NSA_WIDTH
    p1 = p0 + 6 * KV_WIDTH
    p2 = p1 + 3 * N_NSA_HEADS
    wq = w[:, :p0].reshape(D_MODEL, N_KV, HPG, HEAD_DIM).transpose(0, 2, 1, 3).reshape(D_MODEL, p0)
    w_gates = jnp.pad(w[:, p1:p2], ((0, 0), (0, LANES - 3 * N_NSA_HEADS)))
    return jnp.concatenate([wq, w[:, p0:p1], w[:, p2:], w_gates], axis=1).astype(BF16)


def _permute_w_o(w):
    w_nsa = w[:NSA_WIDTH].reshape(N_KV, HPG, HEAD_DIM, D_MODEL).transpose(1, 0, 2, 3).reshape(NSA_WIDTH, D_MODEL)
    return jnp.concatenate([w_nsa, w[NSA_WIDTH:]], axis=0).astype(BF16)


def kernel(x_prompt, x_sample, cache_kv, cache_win, state_ret, page_table, w_in, w_cmp_k, w_cmp_v, pos_cmp_k, pos_cmp_v, ret_norm_g, w_o, ln1_g, ln1_b, w_up, w_down, ln2_g, ln2_b):
    depth = w_in.shape[0]
    assert depth == 1 and x_sample.shape[1] == 1
    batch, seq, _ = x_prompt.shape
    nb = x_sample.shape[0]
    n_pages = page_table.shape[1]
    past = n_pages * PAGE_SIZE
    assert past // SEL_BLOCK == N_SEL_BLOCKS and cache_win.shape[2] == WINDOW
    alpha = (2 * depth) ** 0.25

    w_perm = _permute_w_in(w_in[0])
    wo = _permute_w_o(w_o[0])
    wu, wd = w_up[0].astype(BF16), w_down[0].astype(BF16)
    vec = lambda a: a[0].reshape(1, -1)
    cmp_w = _compress_weights(w_cmp_k[0], w_cmp_v[0], pos_cmp_k[0], pos_cmp_v[0])
    tail_consts = (wo, vec(ln1_g), vec(ln1_b), wu, wd, vec(ln2_g), vec(ln2_b))
    norm_g = vec(ret_norm_g)

    tm = ROW_TILE
    pos_p = np.arange(seq)
    tabs_p = _rope_tables(pos_p, ROPE_THETA, ROPE_DIMS) + _rope_tables(pos_p, RET_THETA, RET_DK)
    x2d = x_prompt.reshape(batch * seq, D_MODEL)
    kv4t, wint, rq, rk, rv, rg, qt, gt, kaug, vst, kw, vwt = _project(x2d, w_perm, tabs_p, tm=tm, seq=seq, prompt=True)
    seq_pages = seq // PAGE_SIZE
    pidx_p = jnp.broadcast_to(jnp.arange(batch, dtype=jnp.int32)[:, None], (batch, seq_pages))
    poff_p = jnp.broadcast_to(jnp.arange(seq_pages, dtype=jnp.int32)[None, :] * PAGE_SIZE, (batch, seq_pages))
    kcvc_p = _compress(pidx_p, poff_p, kv4t, *cmp_w)
    o_nsa = _nsa_prompt(qt, gt, kcvc_p, kaug, vst, kw, vwt, jnp.asarray(_importance_matrix(seq // CMP_STRIDE).T, dtype=BF16), batch=batch, seq=seq)
    ret, st_p = _retention_prompt(rq, rk, rv, rg, _retention_tables(RET_CHUNK), norm_g, batch=batch, seq=seq)
    y_p = _tail(x2d, o_nsa, ret, *tail_consts, tm=tm, alpha=alpha)

    y_prompt = y_p.reshape(batch, seq, D_MODEL)
    to_rows = lambda a, n_slot: a.reshape(a.shape[0], n_slot, N_KV, HEAD_DIM, a.shape[-1]).transpose(0, 4, 1, 2, 3)[None]
    kv_prompt = to_rows(kv4t, 4)
    wlen = min(WINDOW, seq)
    win_prompt = to_rows(wint[:, :, seq - wlen:], 2)
    st6 = st_p.reshape(batch, N_PAIR, 2, RET_DK, 2, RET_DV)
    ret_prompt = jnp.stack([st6[:, :, 0, :, 0, :], st6[:, :, 1, :, 1, :]], axis=2).reshape(1, batch, N_RET_HEADS, RET_DK, RET_DV)

    pos_s = np.full((nb,), past)
    tabs_s = _rope_tables(pos_s, ROPE_THETA, ROPE_DIMS) + _rope_tables(pos_s, RET_THETA, RET_DK)
    xs2d = x_sample.reshape(nb, D_MODEL)
    kv4_d, win_d, rq_d, rk_d, rv_d, rg_d, q_d, gates_d = _project(xs2d, w_perm, tabs_s, tm=nb, seq=seq, prompt=False)
    cache_t = cache_kv[0].transpose(0, 2, 3, 4, 1).reshape(cache_kv.shape[1], 4 * KV_WIDTH, PAGE_SIZE)
    cwin_t = cache_win[0].transpose(0, 2, 3, 4, 1).reshape(nb, 2 * KV_WIDTH, WINDOW)
    kcvc_d = _compress(page_table, jnp.zeros_like(page_table), cache_t, *cmp_w)
    row3 = lambda a: a.reshape(nb, 1, a.shape[-1])
    q3 = row3(q_d)
    oc_d, idx_d = _dec_score(q3, kcvc_d, jnp.asarray(_importance_matrix(past // CMP_STRIDE), dtype=BF16), past=past)
    idx = idx_d[:, :N_KV, :TOP_N]
    state = state_ret[0].astype(F32).reshape(nb, N_PAIR, 2 * RET_DK, RET_DV)
    gl = jnp.asarray(np.repeat(np.exp(_log_gamma()), RET_DK)[None, :], dtype=F32)
    nsa_d, ret_d, win_st, st_d = _dec_attn(page_table, idx, q3, row3(gates_d), oc_d, row3(kv4_d), row3(win_d), cache_t, cwin_t,
                                          row3(rq_d), row3(rk_d), row3(rv_d), row3(rg_d), state, gl, norm_g)
    y_s = _tail(xs2d, nsa_d.reshape(nb, NSA_WIDTH).astype(BF16), ret_d.reshape(nb, RET_WIDTH).astype(BF16), *tail_consts, tm=nb, alpha=alpha)

    y_sample = y_s.reshape(nb, 1, D_MODEL)
    kv_sample = kv4_d.reshape(1, nb, 1, 4, N_KV, HEAD_DIM)
    win_sample = to_rows(win_st, 2)
    ret_sample = st_d.reshape(1, nb, N_RET_HEADS, RET_DK, RET_DV)
    return (y_prompt, y_sample, kv_prompt, kv_sample, win_prompt, win_sample, ret_prompt, ret_sample)
```

```python
import functools

import jax
import jax.numpy as jnp
import numpy as np
from jax import lax
from jax.experimental import pallas as pl
from jax.experimental.pallas import tpu as pltpu

F32 = jnp.float32
BF16 = jnp.bfloat16

D_MODEL = 1024
HEAD_DIM = 64
N_NSA_HEADS = 8
N_KV = 2
HPG = N_NSA_HEADS // N_KV
N_RET_HEADS = 8
RET_DK = 64
RET_DV = 64
NSA_WIDTH = N_NSA_HEADS * HEAD_DIM
RET_WIDTH = N_RET_HEADS * RET_DV
KV_WIDTH = N_KV * HEAD_DIM
D_FF = 4 * D_MODEL
CMP_STRIDE = 16
CMP_BLOCK = 32
SEL_BLOCK = 64
TOP_N = 16
N_FORCED = 3
WINDOW = 512
RET_CHUNK = 128
RET_CHUNKS_PER_STEP = 4
PAGE_SIZE = 128
ROPE_THETA = 500000.0
ROPE_DIMS = HEAD_DIM // 4
RET_THETA = 10000.0
LN_EPS = 1e-5
FORCE_BONUS = 1e4
NEG = -1e30
BELOW_NEG = -3e38

LANES = 128
SUBLANES = 8
VMEM_LIMIT = 56 * 1024 * 1024

LOG2E = 1.4426950408889634
BF16_SUBLANES = 16
ONES_ROWS = BF16_SUBLANES
CMP_VARIANT_ROWS = 128
CHUNK_PITCH = 20
PAGES_PER_STEP = 4
Q_TILE = 256
ROW_TILE = 512
KEY_TILE = 512
assert KEY_TILE % ROW_TILE == 0
N_PAIR = N_RET_HEADS // 2
N_SEL_BLOCKS = 128

U_Q, U_KV, U_RQ, U_RK, U_RV, U_RG, U_G, U_TOTAL = 0, 4, 10, 14, 18, 22, 26, 27
HEAD_PERM = (0, 4, 1, 5, 2, 6, 3, 7)


def _dot(a, b):
    return jnp.dot(a, b, preferred_element_type=F32)


def _dot_nt(a, b):
    return lax.dot_general(a, b, (((1,), (1,)), ((), ())), preferred_element_type=F32)


def _cparams(*sem):
    return pltpu.CompilerParams(dimension_semantics=sem, vmem_limit_bytes=VMEM_LIMIT)


def _const_spec(shape):
    nd = len(shape)
    return pl.BlockSpec(shape, lambda *_: (0,) * nd)


def _rope(u, c, s_fwd, s_bwd, half):
    return u * c + pltpu.roll(u, LANES - half, 1) * s_fwd + pltpu.roll(u, half, 1) * s_bwd


def _proj_kernel(x_ref, w_ref, cq_ref, sqf_ref, sqb_ref, cr_ref, srf_ref, srb_ref, *outs, tm, seq, prompt):
    if prompt:
        kv4t_o, wint_o, rq_o, rk_o, rv_o, rg_o, qt_o, gt_o, kaug_o, vst_o, kw_o, vwt_o = outs
    else:
        kv4_o, win_o, rq_o, rk_o, rv_o, rg_o, q_o, g_o = outs
    xb = x_ref[...].astype(BF16)

    def seg(u0, n):
        return _dot(xb, w_ref[:, u0 * LANES:(u0 + n) * LANES])

    def unit(z, u):
        return z[:, u * LANES:(u + 1) * LANES]

    cq, sqf, sqb = cq_ref[...], sqf_ref[...], sqb_ref[...]
    cr, srf, srb = cr_ref[...], srf_ref[...], srb_ref[...]
    nsa_half, ret_half = ROPE_DIMS // 2, RET_DK // 2

    zq = seg(U_Q, 4)
    q = jnp.concatenate([_rope(unit(zq, c), cq, sqf, sqb, nsa_half) for c in range(4)], axis=1) * (HEAD_DIM ** -0.5)
    gates = jax.nn.sigmoid(seg(U_G, 1))
    zkv = seg(U_KV, 6)
    k_slc = _rope(unit(zkv, 2), cq, sqf, sqb, nsa_half)
    k_win = _rope(unit(zkv, 4), cq, sqf, sqb, nsa_half)
    kv4 = jnp.concatenate([zkv[:, 0:2 * LANES], k_slc, unit(zkv, 3)], axis=1)
    win = jnp.concatenate([k_win, unit(zkv, 5)], axis=1)
    if prompt:
        kv4t = jnp.transpose(kv4)
        wint = jnp.transpose(win)
        kv4t_o[...] = kv4t
        wint_o[...] = wint
        qt_o[...] = jnp.transpose(q * LOG2E).astype(BF16)
        gt_o[...] = jnp.transpose(gates)
        row = pl.program_id(0) * tm + lax.broadcasted_iota(jnp.int32, (tm, 1), 0)
        blk = (row & (seq - 1)) >> 6
        onehot = lax.broadcasted_iota(jnp.int32, (tm, LANES), 1) == blk
        kaug_o[:, 0:LANES] = k_slc.astype(BF16)
        kaug_o[:, LANES:2 * LANES] = jnp.where(onehot, 1.0, 0.0).astype(BF16)
        vst_o[0:LANES, :] = kv4t[3 * LANES:4 * LANES].astype(BF16)
        vst_o[LANES:, :] = jnp.ones((ONES_ROWS, tm), BF16)
        kw_o[...] = k_win.astype(BF16)
        vwt = wint[LANES:2 * LANES].astype(BF16)
        for j in range(tm // LANES):
            vwt_o[j] = vwt[:, j * LANES:(j + 1) * LANES]
    else:
        kv4_o[...] = kv4
        win_o[...] = win
        q_o[...] = q.astype(BF16).astype(F32)
        g_o[...] = gates

    zrq = seg(U_RQ, 4)
    zrk = seg(U_RK, 4)
    for c in range(4):
        sl = slice(c * LANES, (c + 1) * LANES)
        rq_o[:, sl] = _rope(unit(zrq, c), cr, srf, srb, ret_half)
        rk_o[:, sl] = _rope(unit(zrk, c), cr, srf, srb, ret_half) * (RET_DK ** -0.5)
    rv_o[...] = seg(U_RV, 4)
    rg_o[...] = seg(U_RG, 4)


def _rope_tables(pos, theta, n_rot):
    half = n_rot // 2
    inv = theta ** (-np.arange(half, dtype=np.float64) / half)
    ang = np.asarray(pos, dtype=np.float64)[:, None] * inv[None, :]
    cos, sin = np.cos(ang), np.sin(ang)
    n = len(pos)
    rest = HEAD_DIM - n_rot
    c = np.concatenate([cos, cos, np.ones((n, rest))], 1)
    s_fwd = np.concatenate([-sin, np.zeros((n, half + rest))], 1)
    s_bwd = np.concatenate([np.zeros((n, half)), sin, np.zeros((n, rest))], 1)
    rep = LANES // HEAD_DIM
    return tuple(jnp.asarray(np.tile(t, (1, rep)), dtype=F32) for t in (c, s_fwd, s_bwd))


def _project(x2d, w_perm, tables, *, tm, seq, prompt):
    rows = x2d.shape[0]
    n_tab_tiles = tables[0].shape[0] // tm
    row_spec = lambda w: pl.BlockSpec((tm, w), lambda i: (i, 0))
    tab_spec = pl.BlockSpec((tm, LANES), lambda i: (i % n_tab_tiles, 0))
    if prompt:
        assert tm == ROW_TILE and seq % KEY_TILE == 0
        batch, tpb = rows // seq, seq // tm
        feat_spec = lambda f: pl.BlockSpec((None, f, tm), lambda i: (i // tpb, 0, i % tpb))
        outs = [(feat_spec(4 * LANES), (batch, 4 * LANES, seq), F32), (feat_spec(2 * LANES), (batch, 2 * LANES, seq), F32)]
    else:
        outs = [(row_spec(4 * LANES), (rows, 4 * LANES), F32), (row_spec(2 * LANES), (rows, 2 * LANES), F32)]
    outs += [(row_spec(4 * LANES), (rows, 4 * LANES), F32)] * 4
    if prompt:
        outs += [(feat_spec(4 * LANES), (batch, 4 * LANES, seq), BF16),
                 (feat_spec(LANES), (batch, LANES, seq), F32),
                 (row_spec(2 * LANES), (rows, 2 * LANES), BF16),
                 (pl.BlockSpec((None, LANES + ONES_ROWS, tm), lambda i: (i, 0, 0)), (rows // tm, LANES + ONES_ROWS, tm), BF16),
                 (row_spec(LANES), (rows, LANES), BF16),
                 (pl.BlockSpec((tm // LANES, LANES, LANES), lambda i: (i, 0, 0)), (rows // LANES, LANES, LANES), BF16)]
    else:
        outs += [(row_spec(4 * LANES), (rows, 4 * LANES), F32), (row_spec(LANES), (rows, LANES), F32)]
    return pl.pallas_call(
        functools.partial(_proj_kernel, tm=tm, seq=seq, prompt=prompt),
        grid=(rows // tm,),
        in_specs=[row_spec(D_MODEL), _const_spec(w_perm.shape)] + [tab_spec] * 6,
        out_specs=[s for s, _, _ in outs],
        out_shape=[jax.ShapeDtypeStruct(shape, dt) for _, shape, dt in outs],
        compiler_params=_cparams("parallel"),
        name="proj_prompt" if prompt else "proj_decode",
    )(x2d, w_perm, *tables)


def _cmp_kernel(pn_ref, po_ref, src_ref, wlo_ref, whi_ref, plo_ref, phi_ref, out_ref,
                stage, slab, sem, *, n_pages):
    b = pl.program_id(0)
    nb = pl.num_programs(0)
    cur = b % 2
    nxt = 1 - cur
    chunks_per_page = PAGE_SIZE // CMP_STRIDE
    page_rows = chunks_per_page * CHUNK_PITCH
    n_chunks = n_pages * chunks_per_page
    assert n_pages % CMP_STRIDE == 0
    pages_per_r = n_pages // CMP_STRIDE

    def page_copy(bb, sl, p, kv):
        return pltpu.make_async_copy(
            src_ref.at[pn_ref[bb, p], pl.ds(kv * LANES, LANES), pl.ds(pl.multiple_of(po_ref[bb, p], PAGE_SIZE), PAGE_SIZE)],
            stage.at[sl, kv, p],
            sem.at[sl])

    def start_all(bb, sl):
        for p in range(n_pages):
            for kv in range(2):
                page_copy(bb, sl, p, kv).start()

    def wait_all(bb, sl):
        for p in range(n_pages):
            for kv in range(2):
                page_copy(bb, sl, p, kv).wait()

    def transpose_pages(sl, first, count):
        for u in range(count):
            p = first + u
            r0 = pl.multiple_of(p * page_rows, SUBLANES)
            for kv in range(2):
                x = jnp.transpose(stage[sl, kv, p])
                for c in range(chunks_per_page):
                    slab[sl, kv, pl.ds(r0 + c * CHUNK_PITCH, CMP_STRIDE), :] = x[c * CMP_STRIDE:(c + 1) * CMP_STRIDE]

    def compress_rows(with_next):
        acc_lo = jnp.zeros((n_chunks, 2 * LANES), F32)
        acc_hi = jnp.zeros((n_chunks, 2 * LANES), F32)
        for r in range(CMP_STRIDE):
            xr = jnp.concatenate([slab[cur, kv, pl.ds(r, n_chunks, stride=CHUNK_PITCH), :] for kv in range(2)], axis=1)
            acc_lo = acc_lo + _dot((xr + plo_ref[r:r + 1, :]).astype(BF16), wlo_ref[r])
            acc_hi = acc_hi + _dot((xr + phi_ref[r:r + 1, :]).astype(BF16), whi_ref[r])
            if with_next:
                transpose_pages(nxt, r * pages_per_r, pages_per_r)
        shifted = pltpu.roll(acc_hi, n_chunks - 1, 0)
        last = lax.broadcasted_iota(jnp.int32, (n_chunks, 1), 0) == n_chunks - 1
        out_ref[...] = jnp.where(last, 0.0, acc_lo + shifted)

    @pl.when(b == 0)
    def _():
        start_all(0, 0)

        @pl.when(nb > 1)
        def _():
            start_all(1, 1)

        wait_all(0, 0)

        def first_rows(pp, carry):
            transpose_pages(0, pp * PAGES_PER_STEP, PAGES_PER_STEP)
            return carry

        lax.fori_loop(0, n_pages // PAGES_PER_STEP, first_rows, 0)

    @pl.when(b + 2 < nb)
    def _():
        start_all(b + 2, cur)

    @pl.when(b + 1 < nb)
    def _():
        wait_all(b + 1, nxt)
        compress_rows(True)

    @pl.when(b + 1 >= nb)
    def _():
        compress_rows(False)


def _compress(page_index, page_offset, src, wlo, whi, plo, phi):
    nb, n_pages = page_index.shape
    n_chunks = n_pages * PAGE_SIZE // CMP_STRIDE
    grid_spec = pltpu.PrefetchScalarGridSpec(
        num_scalar_prefetch=2,
        grid=(nb,),
        in_specs=[pl.BlockSpec(memory_space=pl.ANY),
                  pl.BlockSpec(wlo.shape, lambda b, *_: (0, 0, 0)),
                  pl.BlockSpec(whi.shape, lambda b, *_: (0, 0, 0)),
                  pl.BlockSpec(plo.shape, lambda b, *_: (0, 0)),
                  pl.BlockSpec(phi.shape, lambda b, *_: (0, 0))],
        out_specs=pl.BlockSpec((None, n_chunks, 2 * LANES), lambda b, *_: (b, 0, 0)),
        scratch_shapes=[pltpu.VMEM((2, 2, n_pages, LANES, PAGE_SIZE), F32),
                        pltpu.VMEM((2, 2, n_chunks * CHUNK_PITCH, LANES), F32),
                        pltpu.SemaphoreType.DMA((2,))],
    )
    return pl.pallas_call(
        functools.partial(_cmp_kernel, n_pages=n_pages),
        grid_spec=grid_spec,
        out_shape=jax.ShapeDtypeStruct((nb, n_chunks, 2 * LANES), F32),
        compiler_params=_cparams("arbitrary"),
        name="compress",
    )(page_index, page_offset, src, wlo, whi, plo, phi)


def _compress_weights(w_k, w_v, pos_k, pos_v):
    def bd(wk_r, wv_r):
        z = jnp.zeros((HEAD_DIM, HEAD_DIM), F32)
        rows = [[wk_r, z, z, z], [z, wk_r, z, z], [z, z, wv_r, z], [z, z, z, wv_r]]
        return jnp.block(rows)

    w_all = jax.vmap(bd)(w_k, w_v).astype(BF16)
    p_all = jnp.concatenate([pos_k, pos_k, pos_v, pos_v], axis=1)
    return w_all[:CMP_STRIDE], w_all[CMP_STRIDE:], p_all[:CMP_STRIDE], p_all[CMP_STRIDE:]


def _masked_softmax(s, mask):
    s = jnp.where(mask, s, NEG)
    m = jnp.max(s, -1, keepdims=True)
    p = jnp.exp(s - m) * mask.astype(F32)
    return p / jnp.maximum(jnp.sum(p, -1, keepdims=True), 1e-30)


def _split3(x):
    hi = x.astype(BF16)
    r1 = x - hi.astype(F32)
    mid = r1.astype(BF16)
    lo = (r1 - mid.astype(F32)).astype(BF16)
    return hi, mid, lo


def _importance_matrix(n_cmp):
    n = np.arange(n_cmp)[:, None]
    j = np.arange(N_SEL_BLOCKS)[None, :]
    inner = (n >= 4 * j) & (n <= 4 * j + 2)
    edge = (n == 4 * j - 1) | (n == 4 * j + 3)
    return np.where(inner, 2.0, np.where(edge, 1.0, 0.0))


def _softmax_over_keys(s, mask):
    s = jnp.where(mask, s, NEG)
    m = jnp.max(s, 0, keepdims=True)
    p = jnp.exp2(s - m)
    inv = jnp.where(m > 0.5 * NEG, 1.0 / jnp.maximum(jnp.sum(p, 0, keepdims=True), 1e-30), 0.0)
    return p * inv


def _nsa_kernel(qt_ref, gt_ref, kcvc_ref, kaug_ref, vst_ref, kw_ref, vwt_ref, mt_ref, out_ref,
                kc_s, vct_s, m_s, acc_s, s_buf, tmax_s, qaug_s, ocw_s):
    i = pl.program_id(1)
    t0 = i * Q_TILE
    n_cmp = kcvc_ref.shape[0]

    @pl.when(i == 0)
    def _():
        kcvc = kcvc_ref[...]
        kc_s[...] = kcvc[:, 0:LANES].astype(BF16)
        vct_s[...] = jnp.transpose(kcvc[:, LANES:2 * LANES]).astype(BF16)

    lane = lax.broadcasted_iota(jnp.int32, (1, Q_TILE), 1)
    tq = jnp.concatenate([t0 + lane] * N_NSA_HEADS, axis=1)
    low_rows = lax.broadcasted_iota(jnp.int32, (LANES, 1), 0) < HEAD_DIM
    qt = qt_ref[...]
    zero = jnp.zeros((), BF16)
    q_all = jnp.concatenate([jnp.where(low_rows if g == 0 else ~low_rows, qt[c * LANES:(c + 1) * LANES], zero)
                             for g in range(N_KV) for c in range(HPG)], axis=1)

    qaug_s[0:LANES, :] = q_all
    gt = gt_ref[...]
    gk = [jnp.concatenate([gt[3 * h + k:3 * h + k + 1] for h in range(N_NSA_HEADS)], axis=1) for k in range(3)]
    cur = jnp.concatenate([(t0 + lane) >> 6] * N_KV, axis=1)

    def compressed_and_select(n_act, j_act):
        cend = lax.broadcasted_iota(jnp.int32, (n_act, 1), 0) * CMP_STRIDE + (CMP_BLOCK - 1)
        p_c = _softmax_over_keys(_dot(kc_s[0:n_act, :], q_all), cend <= tq)
        ocw_s[...] = gk[0] * _dot(vct_s[:, 0:n_act], p_c.astype(BF16))
        psum = jnp.concatenate([sum(p_c[:, (g * HPG + c) * Q_TILE:(g * HPG + c + 1) * Q_TILE] for c in range(HPG))
                                for g in range(N_KV)], axis=1)
        mt = mt_ref[0:j_act, 0:n_act]
        imp_t = sum(_dot(mt, part) for part in _split3(psum))
        blk = lax.broadcasted_iota(jnp.int32, (j_act, 1), 0)
        blk_f = blk.astype(F32)
        forced = (blk == 0) | (blk == cur) | (blk == cur - 1)
        score = jnp.where(forced, BELOW_NEG, jnp.where(blk <= cur, imp_t, NEG))
        bias = jnp.where(forced, 0.0, NEG)
        for _ in range(TOP_N - N_FORCED):
            best = jnp.max(score, axis=0, keepdims=True)
            first = jnp.min(jnp.where(score == best, blk_f, float(N_SEL_BLOCKS)), axis=0, keepdims=True)
            pick = blk_f == first
            bias = jnp.where(pick, 0.0, bias)
            score = jnp.where(pick, BELOW_NEG, score)
        bias = bias.astype(BF16)
        qaug_s[LANES:LANES + j_act, :] = jnp.concatenate(
            [bias[:, g * Q_TILE:(g + 1) * Q_TILE] for g in range(N_KV) for _ in range(HPG)], axis=1)
        if j_act < N_SEL_BLOCKS:
            qaug_s[LANES + j_act:, :] = jnp.full((N_SEL_BLOCKS - j_act, qaug_s.shape[1]), NEG, BF16)

    q_blocks_per_variant = CMP_VARIANT_ROWS * CMP_STRIDE // Q_TILE
    n_variants = n_cmp // CMP_VARIANT_ROWS
    for v in range(n_variants):
        @pl.when(i // q_blocks_per_variant == v)
        def _(v=v):
            compressed_and_select((v + 1) * CMP_VARIANT_ROWS, min(N_SEL_BLOCKS, (v + 1) * CMP_VARIANT_ROWS * CMP_STRIDE // SEL_BLOCK))

    m_s[...] = jnp.full(m_s.shape, NEG, F32)
    acc_s[...] = jnp.zeros(acc_s.shape, F32)

    def scores(kt, causal):
        k0 = pl.multiple_of(kt * KEY_TILE, KEY_TILE)
        s = _dot(kaug_ref[pl.ds(k0, KEY_TILE), :], qaug_s[...])
        if causal:
            kpos = k0 + lax.broadcasted_iota(jnp.int32, (KEY_TILE, 1), 0)
            s = jnp.where(kpos <= tq, s, NEG)
        s_buf[...] = s
        tmax_s[...] = jnp.max(s, 0, keepdims=True)

    def accumulate(kt):
        m_old = m_s[...]
        m_new = jnp.maximum(m_old, tmax_s[...])
        alpha = jnp.exp2(m_old - m_new)
        p = jnp.exp2(s_buf[...] - m_new).astype(BF16)
        n_sub = KEY_TILE // ROW_TILE
        pv = sum(_dot(vst_ref[kt * n_sub + j], p[j * ROW_TILE:(j + 1) * ROW_TILE]) for j in range(n_sub))
        acc_s[...] = alpha * acc_s[...] + pv
        m_s[...] = m_new

    last = (t0 + Q_TILE - 1) // KEY_TILE

    @pl.when(last == 0)
    def _():
        scores(0, True)

    @pl.when(last > 0)
    def _():
        scores(0, False)

        def body(kt, carry):
            accumulate(kt)
            scores(kt + 1, False)
            return carry

        lax.fori_loop(0, last - 1, body, 0)
        accumulate(last - 1)
        scores(last, True)

    accumulate(last)
    o_s = acc_s[0:LANES, :] * (1.0 / acc_s[LANES:LANES + 1, :])

    w0 = pl.multiple_of(jnp.maximum(t0 - WINDOW, 0), Q_TILE)
    n_win = WINDOW + Q_TILE
    dist = tq - (w0 + lax.broadcasted_iota(jnp.int32, (n_win, 1), 0))
    p_w = _softmax_over_keys(_dot(kw_ref[pl.ds(w0, n_win), :], q_all), (dist >= 0) & (dist < WINDOW))
    w_blk = w0 // LANES
    vwt = jnp.concatenate([vwt_ref[w_blk + j] for j in range(n_win // LANES)], axis=1)
    o_w = _dot(vwt, p_w.astype(BF16))

    o = ocw_s[...] + gk[2] * o_w + gk[1] * o_s
    half = HPG * Q_TILE
    comb = jnp.where(low_rows, o[:, 0:half], o[:, half:2 * half])
    for c in range(HPG):
        out_ref[:, c * LANES:(c + 1) * LANES] = jnp.transpose(comb[:, c * Q_TILE:(c + 1) * Q_TILE]).astype(BF16)


def _nsa_prompt(qt, gt, kcvc, kaug, vst, kw, vwt, mt, *, batch, seq):
    nq = seq // Q_TILE
    n_cols = N_NSA_HEADS * Q_TILE
    n_cmp = kcvc.shape[1]
    col_spec = lambda f: pl.BlockSpec((None, f, Q_TILE), lambda b, i: (b, 0, i))
    seq_spec = lambda w: pl.BlockSpec((seq, w), lambda b, i: (b, 0))
    return pl.pallas_call(
        _nsa_kernel,
        grid=(batch, nq),
        in_specs=[col_spec(4 * LANES), col_spec(LANES),
                  pl.BlockSpec((None, n_cmp, 2 * LANES), lambda b, i: (b, 0, 0)),
                  seq_spec(2 * LANES),
                  pl.BlockSpec((seq // ROW_TILE, LANES + ONES_ROWS, ROW_TILE), lambda b, i: (b, 0, 0)),
                  seq_spec(LANES),
                  pl.BlockSpec((seq // LANES, LANES, LANES), lambda b, i: (b, 0, 0)),
                  _const_spec(mt.shape)],
        out_specs=pl.BlockSpec((Q_TILE, 4 * LANES), lambda b, i: (b * nq + i, 0)),
        out_shape=jax.ShapeDtypeStruct((batch * seq, 4 * LANES), BF16),
        scratch_shapes=[pltpu.VMEM((n_cmp, LANES), BF16), pltpu.VMEM((LANES, n_cmp), BF16),
                        pltpu.VMEM((1, n_cols), F32), pltpu.VMEM((LANES + ONES_ROWS, n_cols), F32),
                        pltpu.VMEM((KEY_TILE, n_cols), F32), pltpu.VMEM((1, n_cols), F32),
                        pltpu.VMEM((2 * LANES, n_cols), BF16), pltpu.VMEM((LANES, n_cols), F32)],
        compiler_params=_cparams("parallel", "arbitrary"),
        name="nsa_prompt",
    )(qt, gt, kcvc, kaug, vst, kw, vwt, mt)


def _head_sum(a, low):
    s0 = jnp.sum(jnp.where(low, a, 0.0), -1, keepdims=True)
    s1 = jnp.sum(jnp.where(low, 0.0, a), -1, keepdims=True)
    return jnp.where(low, s0, s1)


def _gated_group_norm(o, rg, norm_g, low):
    mu = _head_sum(o, low) * (1.0 / RET_DV)
    d = o - mu
    var = _head_sum(d * d, low) * (1.0 / RET_DV)
    gn = d * lax.rsqrt(var + LN_EPS) * norm_g
    return (rg * jax.nn.sigmoid(rg)) * gn


def _ret_kernel(rq_ref, rk_ref, rv_ref, rg_ref, decay_ref, qdec_ref, kdec_ref, gcl_ref, bd_ref, ng_ref,
                ret_o, st_o, st_s):
    c = pl.program_id(1)

    @pl.when(c == 0)
    def _():
        st_s[...] = jnp.zeros(st_s.shape, F32)

    low = lax.broadcasted_iota(jnp.int32, (1, LANES), 1) < RET_DK
    zero = jnp.zeros((), BF16)
    for j in range(RET_CHUNKS_PER_STEP):
        rows = slice(j * RET_CHUNK, (j + 1) * RET_CHUNK)
        for p in range(N_PAIR):
            sl = slice(p * LANES, (p + 1) * LANES)
            q, k, v = rq_ref[rows, sl], rk_ref[rows, sl], rv_ref[rows, sl]
            qb, kb, vb = q.astype(BF16), k.astype(BF16), v.astype(BF16)
            in0 = _dot_nt(jnp.where(low, qb, zero), kb) * decay_ref[2 * p]
            in1 = _dot_nt(jnp.where(low, zero, qb), kb) * decay_ref[2 * p + 1]
            inner = jnp.concatenate([in0.astype(BF16), in1.astype(BF16)], axis=1)
            v_split = jnp.concatenate([jnp.where(low, vb, zero), jnp.where(low, zero, vb)], axis=0)
            state = st_s[p]
            o = _dot(inner, v_split) + _dot((q * qdec_ref[:, sl]).astype(BF16), state.astype(BF16))
            kd_t = jnp.transpose(k * kdec_ref[:, sl]).astype(BF16)
            st_s[p] = state * gcl_ref[:, sl] + _dot(kd_t, vb) * bd_ref[...]
            ret_o[rows, sl] = _gated_group_norm(o, rg_ref[rows, sl], ng_ref[:, sl], low).astype(BF16)

    @pl.when(c == pl.num_programs(1) - 1)
    def _():
        st_o[...] = st_s[...]


def _log_gamma():
    return np.log1p(-np.exp2(-5.0 - np.arange(N_RET_HEADS, dtype=np.float64)))


def _retention_tables(chunk):
    log_gamma = _log_gamma()
    i = np.arange(chunk, dtype=np.float64)
    diff = i[:, None] - i[None, :]
    decay = np.where(diff >= 0, np.exp(log_gamma[:, None, None] * np.maximum(diff, 0.0)), 0.0)
    per_lane = lambda t: np.repeat(t, RET_DK, axis=-1)
    qdec = per_lane(np.exp((i + 1.0)[:, None] * log_gamma[None, :]))
    kdec = per_lane(np.exp((chunk - 1.0 - i)[:, None] * log_gamma[None, :]))
    gcl = per_lane(np.exp(chunk * log_gamma)[None, :])
    r = np.arange(LANES)
    bd = (r[:, None] // RET_DK) == (r[None, :] // RET_DV)
    return tuple(jnp.asarray(t, dtype=F32) for t in (decay, qdec, kdec, gcl, bd))


def _retention_prompt(rq, rk, rv, rg, tables, norm_g, *, batch, seq):
    decay, qdec, kdec, gcl, bd = tables
    step_rows = RET_CHUNK * RET_CHUNKS_PER_STEP
    nc = seq // step_rows
    row_spec = pl.BlockSpec((step_rows, RET_WIDTH), lambda b, c: (b * nc + c, 0))
    return pl.pallas_call(
        _ret_kernel,
        grid=(batch, nc),
        in_specs=[row_spec] * 4 + [_const_spec(decay.shape), _const_spec(qdec.shape), _const_spec(kdec.shape),
                                   _const_spec(gcl.shape), _const_spec(bd.shape), _const_spec(norm_g.shape)],
        out_specs=[row_spec, pl.BlockSpec((None, N_PAIR, LANES, LANES), lambda b, c: (b, 0, 0, 0))],
        out_shape=[jax.ShapeDtypeStruct((batch * seq, RET_WIDTH), BF16),
                   jax.ShapeDtypeStruct((batch, N_PAIR, LANES, LANES), F32)],
        scratch_shapes=[pltpu.VMEM((N_PAIR, LANES, LANES), F32)],
        compiler_params=_cparams("parallel", "arbitrary"),
        name="retention",
    )(rq, rk, rv, rg, decay, qdec, kdec, gcl, bd, norm_g)


def _layer_norm(v, g, b):
    mu = jnp.mean(v, -1, keepdims=True)
    d = v - mu
    var = jnp.mean(d * d, -1, keepdims=True)
    return d * lax.rsqrt(var + LN_EPS) * g + b


def _tail_kernel(x_ref, nsa_ref, ret_ref, wo_ref, g1_ref, b1_ref, wu_ref, wd_ref, g2_ref, b2_ref, y_ref, *, alpha):
    mix = _dot(nsa_ref[...], wo_ref[0:NSA_WIDTH, :]) + _dot(ret_ref[...], wo_ref[NSA_WIDTH:, :])
    x1 = _layer_norm(alpha * x_ref[...] + mix, g1_ref[...], b1_ref[...])
    x1b = x1.astype(BF16)
    f = jnp.zeros(x1.shape, F32)
    for j in range(D_FF // D_MODEL):
        sl = slice(j * D_MODEL, (j + 1) * D_MODEL)
        h = jnp.maximum(_dot(x1b, wu_ref[:, sl]), 0.0)
        f = f + _dot((h * h).astype(BF16), wd_ref[sl, :])
    y_ref[...] = _layer_norm(alpha * x1 + f, g2_ref[...], b2_ref[...])


def _tail(x2d, nsa, ret, wo, g1, b1, wu, wd, g2, b2, *, tm, alpha):
    rows = x2d.shape[0]
    row_spec = lambda w: pl.BlockSpec((tm, w), lambda i: (i, 0))
    consts = (wo, g1, b1, wu, wd, g2, b2)
    return pl.pallas_call(
        functools.partial(_tail_kernel, alpha=alpha),
        grid=(rows // tm,),
        in_specs=[row_spec(D_MODEL), row_spec(NSA_WIDTH), row_spec(RET_WIDTH)] + [_const_spec(c.shape) for c in consts],
        out_specs=row_spec(D_MODEL),
        out_shape=jax.ShapeDtypeStruct((rows, D_MODEL), F32),
        compiler_params=_cparams("parallel"),
        name="tail",
    )(x2d, nsa, ret, *consts)


DEC_ROWS = 16
N_DEC_SEL = TOP_N - 1


def _dec_stack_q(q, low):
    rows = lax.broadcasted_iota(jnp.int32, (DEC_ROWS, 1), 0)
    qs = jnp.zeros((DEC_ROWS, LANES), F32)
    for g in range(N_KV):
        for c in range(HPG):
            piece = jnp.where(low if g == 0 else ~low, q[:, c * LANES:(c + 1) * LANES], 0.0)
            qs = jnp.where(rows == g * HPG + c, piece, qs)
    return qs, rows


def _dec_score_kernel(q_ref, kcvc_ref, m_ref, oc_o, idx_o, score_s, *, past):
    lane = lax.broadcasted_iota(jnp.int32, (1, LANES), 1)
    low = lane < HEAD_DIM
    qs, rows = _dec_stack_q(q_ref[...], low)
    qsb = qs.astype(BF16)
    kcvc = kcvc_ref[...]
    kc = kcvc[:, 0:LANES].astype(BF16)
    vc = kcvc[:, LANES:2 * LANES].astype(BF16)
    n_cmp = kcvc.shape[0]
    cend = lax.broadcasted_iota(jnp.int32, (1, n_cmp), 1) * CMP_STRIDE + (CMP_BLOCK - 1)
    p = _masked_softmax(_dot_nt(qsb, kc), jnp.broadcast_to(cend <= past, (DEC_ROWS, n_cmp)))
    oc_o[...] = _dot(p.astype(BF16), vc)

    psum = jnp.zeros((DEC_ROWS, n_cmp), F32)
    for g in range(N_KV):
        acc = p[g * HPG:g * HPG + 1]
        for c in range(1, HPG):
            acc = acc + p[g * HPG + c:g * HPG + c + 1]
        psum = jnp.where(rows == g, acc, psum)
    m = m_ref[...]
    imp = sum(_dot(part, m) for part in _split3(psum))
    forced = (lane == 0) | (lane == N_SEL_BLOCKS - 1)
    b = pl.program_id(0)
    score_s[b] = imp + FORCE_BONUS * forced.astype(F32)

    @pl.when(b == pl.num_programs(0) - 1)
    def _():
        lane_f = lane.astype(F32)
        score = score_s[...].reshape(-1, LANES)
        picked = jnp.zeros(score.shape, F32)
        for r in range(N_DEC_SEL):
            best = jnp.max(score, -1, keepdims=True)
            first = jnp.min(jnp.where(score == best, lane_f, float(LANES)), -1, keepdims=True)
            picked = jnp.where(lane == r, first, picked)
            score = jnp.where(lane_f == first, BELOW_NEG, score)
        idx_o[...] = picked.astype(jnp.int32).reshape(idx_o.shape)


def _dec_score(q3, kcvc, m, *, past):
    nb = q3.shape[0]
    n_cmp = kcvc.shape[1]
    return pl.pallas_call(
        functools.partial(_dec_score_kernel, past=past),
        grid=(nb,),
        in_specs=[pl.BlockSpec((None, 1, 4 * LANES), lambda b: (b, 0, 0)),
                  pl.BlockSpec((None, n_cmp, 2 * LANES), lambda b: (b, 0, 0)),
                  _const_spec(m.shape)],
        out_specs=[pl.BlockSpec((None, DEC_ROWS, LANES), lambda b: (b, 0, 0)),
                   _const_spec((nb, DEC_ROWS, LANES))],
        out_shape=[jax.ShapeDtypeStruct((nb, DEC_ROWS, LANES), F32), jax.ShapeDtypeStruct((nb, DEC_ROWS, LANES), jnp.int32)],
        scratch_shapes=[pltpu.VMEM((nb, DEC_ROWS, LANES), F32)],
        compiler_params=_cparams("arbitrary"),
        name="dec_score",
    )(q3, kcvc, m)


def _round_bf16(x):
    return x.astype(BF16).astype(F32)


def _dec_attn_kernel(pt_ref, idx_ref, q_ref, g_ref, oc_ref, kvn_ref, wn_ref, cache_ref, cw_ref,
                     rq_ref, rk_ref, rv_ref, rg_ref, st_ref, gl_ref, ng_ref,
                     nsa_o, ret_o, win_o, st_o, kbuf, sem):
    b = pl.program_id(0)
    nb = pl.num_programs(0)
    slot = b % 2

    def blk_copy(bb, sl, g, k):
        j = idx_ref[bb, g, k]
        return pltpu.make_async_copy(
            cache_ref.at[pt_ref[bb, j >> 1], pl.ds(2 * LANES, 2 * LANES), :],
            kbuf.at[sl, g, :, pl.ds(k * PAGE_SIZE, PAGE_SIZE)],
            sem.at[sl])

    def start_all(bb, sl):
        for g in range(N_KV):
            for k in range(N_DEC_SEL):
                blk_copy(bb, sl, g, k).start()

    @pl.when(b == 0)
    def _():
        start_all(0, 0)

    @pl.when(b + 1 < nb)
    def _():
        start_all(b + 1, 1 - slot)

    lane = lax.broadcasted_iota(jnp.int32, (1, LANES), 1)
    low = lane < HEAD_DIM
    qs, rows = _dec_stack_q(q_ref[...], low)
    qsb = qs.astype(BF16)
    g0_rows = rows < HPG

    cw = cw_ref[...]
    wb = cw.shape[1]
    wn = wn_ref[...]
    col = lax.broadcasted_iota(jnp.int32, (1, wb), 1)
    valid = jnp.broadcast_to((wb - col) < WINDOW, (DEC_ROWS, wb))
    s_w = jnp.where(valid, _dot(qsb, cw[0:LANES].astype(BF16)), NEG)
    s_wn = jnp.sum(qs * _round_bf16(wn[:, 0:LANES]), -1, keepdims=True)
    m_w = jnp.maximum(jnp.max(s_w, -1, keepdims=True), s_wn)
    p_w = jnp.exp(s_w - m_w) * valid.astype(F32)
    p_wn = jnp.exp(s_wn - m_w)
    l_w = jnp.maximum(jnp.sum(p_w, -1, keepdims=True) + p_wn, 1e-30)
    o_w = _dot_nt((p_w / l_w).astype(BF16), cw[LANES:2 * LANES].astype(BF16)) + _round_bf16(p_wn / l_w) * _round_bf16(wn[:, LANES:2 * LANES])
    new_cols = jnp.concatenate(
        [jnp.concatenate([jnp.transpose(jnp.broadcast_to(wn[:, u * LANES:(u + 1) * LANES], (LANES, LANES)))] * (wb // LANES), axis=1)
         for u in range(2)], axis=0)
    win_o[...] = jnp.where(col == wb - 1, new_cols, pltpu.roll(cw, wb - 1, 1))

    for g in range(N_KV):
        for k in range(N_DEC_SEL):
            blk_copy(b, slot, g, k).wait()
    kvn = kvn_ref[...]
    s_g = [_dot(qsb, kbuf[slot, g, 0:LANES, :].astype(BF16)) for g in range(N_KV)]
    pos = lax.broadcasted_iota(jnp.int32, (1, N_DEC_SEL * PAGE_SIZE), 1)
    picked_half = []
    for g in range(N_KV):
        half = jnp.zeros(pos.shape, jnp.int32)
        for k in range(N_DEC_SEL):
            half = jnp.where(pos >> 7 == k, idx_ref[b, g, k] & 1, half)
        picked_half.append(half)
    in_block = ((pos >> 6) & 1) == jnp.where(g0_rows, picked_half[0], picked_half[1])
    s_s = jnp.where(in_block, jnp.where(g0_rows, s_g[0], s_g[1]), NEG)
    s_sn = jnp.sum(qs * _round_bf16(kvn[:, 2 * LANES:3 * LANES]), -1, keepdims=True)
    m_sel = jnp.maximum(jnp.max(s_s, -1, keepdims=True), s_sn)
    p_s = jnp.exp(s_s - m_sel)
    p_sn = jnp.exp(s_sn - m_sel)
    l_sel = jnp.maximum(jnp.sum(p_s, -1, keepdims=True) + p_sn, 1e-30)
    pn = (p_s / l_sel).astype(BF16)
    o_g = [_dot_nt(pn, kbuf[slot, g, LANES:2 * LANES, :].astype(BF16)) for g in range(N_KV)]
    o_s = jnp.where(g0_rows, o_g[0], o_g[1]) + _round_bf16(p_sn / l_sel) * _round_bf16(kvn[:, 3 * LANES:4 * LANES])

    gates = g_ref[...]
    gcol = [jnp.zeros((DEC_ROWS, 1), F32) for _ in range(3)]
    for h in range(N_NSA_HEADS):
        for k in range(3):
            gcol[k] = jnp.where(rows == h, gates[:, 3 * h + k:3 * h + k + 1], gcol[k])
    o = gcol[0] * oc_ref[...] + gcol[1] * o_s + gcol[2] * o_w
    for c in range(HPG):
        nsa_o[:, c * LANES:(c + 1) * LANES] = jnp.where(low, o[c:c + 1], o[HPG + c:HPG + c + 1])

    top = lax.broadcasted_iota(jnp.int32, (LANES, 1), 0) < RET_DK
    for p in range(N_PAIR):
        sl = slice(p * LANES, (p + 1) * LANES)
        q, k, v, gl = rq_ref[:, sl], rk_ref[:, sl], rv_ref[:, sl], gl_ref[:, sl]
        state = st_ref[p]
        inner = _head_sum(q * k, low)

        def per_row(row):
            return jnp.transpose(jnp.broadcast_to(row, (LANES, LANES)))[:, 0:RET_DV]

        v_rows = jnp.where(top, v[:, 0:RET_DV], v[:, RET_DV:2 * RET_DV])
        qs_state = per_row(q * gl) * state
        cross = jnp.concatenate([jnp.sum(qs_state[0:RET_DK], axis=0, keepdims=True),
                                 jnp.sum(qs_state[RET_DK:2 * RET_DK], axis=0, keepdims=True)], axis=1)
        o_r = inner * v + cross
        st_o[p] = per_row(gl) * state + per_row(k) * v_rows
        ret_o[:, sl] = _gated_group_norm(o_r, rg_ref[:, sl], ng_ref[:, sl], low)


def _dec_attn(page_table, idx, q3, gates3, oc, kvn3, wn3, cache4, cwin, rq3, rk3, rv3, rg3, state, gl, ng):
    nb = q3.shape[0]
    wb = cwin.shape[2]
    row3 = lambda w: pl.BlockSpec((None, 1, w), lambda b, *_: (b, 0, 0))
    st_spec = pl.BlockSpec((None, N_PAIR, 2 * RET_DK, RET_DV), lambda b, *_: (b, 0, 0, 0))
    win_spec = pl.BlockSpec((None, 2 * LANES, wb), lambda b, *_: (b, 0, 0))
    grid_spec = pltpu.PrefetchScalarGridSpec(
        num_scalar_prefetch=2,
        grid=(nb,),
        in_specs=[row3(4 * LANES), row3(LANES), pl.BlockSpec((None, DEC_ROWS, LANES), lambda b, *_: (b, 0, 0)),
                  row3(4 * LANES), row3(2 * LANES), pl.BlockSpec(memory_space=pl.ANY), win_spec,
                  row3(4 * LANES), row3(4 * LANES), row3(4 * LANES), row3(4 * LANES), st_spec,
                  pl.BlockSpec(gl.shape, lambda b, *_: (0, 0)), pl.BlockSpec(ng.shape, lambda b, *_: (0, 0))],
        out_specs=[row3(4 * LANES), row3(4 * LANES), win_spec, st_spec],
        scratch_shapes=[pltpu.VMEM((2, N_KV, 2 * LANES, N_DEC_SEL * PAGE_SIZE), F32), pltpu.SemaphoreType.DMA((2,))],
    )
    return pl.pallas_call(
        _dec_attn_kernel,
        grid_spec=grid_spec,
        out_shape=[jax.ShapeDtypeStruct((nb, 1, 4 * LANES), F32), jax.ShapeDtypeStruct((nb, 1, 4 * LANES), F32),
                   jax.ShapeDtypeStruct((nb, 2 * LANES, wb), F32), jax.ShapeDtypeStruct(state.shape, F32)],
        compiler_params=_cparams("arbitrary"),
        name="dec_attn",
    )(page_table, idx, q3, gates3, oc, kvn3, wn3, cache4, cwin, rq3, rk3, rv3, rg3, state, gl, ng)


def _permute_w_in(w):
    p0 = NSA_WIDTH
    p1 = p0 + 6 * KV_WIDTH
    p2 = p1 + 3 * N_NSA_HEADS
    wq = w[:, :p0].reshape(D_MODEL, N_KV, HPG, HEAD_DIM).transpose(0, 2, 1, 3).reshape(D_MODEL, p0)
    w_gates = jnp.pad(w[:, p1:p2], ((0, 0), (0, LANES - 3 * N_NSA_HEADS)))
    return jnp.concatenate([wq, w[:, p0:p1], w[:, p2:], w_gates], axis=1).astype(BF16)


def _permute_w_o(w):
    w_nsa = w[:NSA_WIDTH].reshape(N_KV, HPG, HEAD_DIM, D_MODEL).transpose(1, 0, 2, 3).reshape(NSA_WIDTH, D_MODEL)
    return jnp.concatenate([w_nsa, w[NSA_WIDTH:]], axis=0).astype(BF16)


def kernel(x_prompt, x_sample, cache_kv, cache_win, state_ret, page_table, w_in, w_cmp_k, w_cmp_v, pos_cmp_k, pos_cmp_v, ret_norm_g, w_o, ln1_g, ln1_b, w_up, w_down, ln2_g, ln2_b):
    depth = w_in.shape[0]
    assert depth == 1 and x_sample.shape[1] == 1
    batch, seq, _ = x_prompt.shape
    nb = x_sample.shape[0]
    n_pages = page_table.shape[1]
    past = n_pages * PAGE_SIZE
    assert past // SEL_BLOCK == N_SEL_BLOCKS and cache_win.shape[2] == WINDOW
    alpha = (2 * depth) ** 0.25

    w_perm = _permute_w_in(w_in[0])
    wo = _permute_w_o(w_o[0])
    wu, wd = w_up[0].astype(BF16), w_down[0].astype(BF16)
    vec = lambda a: a[0].reshape(1, -1)
    cmp_w = _compress_weights(w_cmp_k[0], w_cmp_v[0], pos_cmp_k[0], pos_cmp_v[0])
    tail_consts = (wo, vec(ln1_g), vec(ln1_b), wu, wd, vec(ln2_g), vec(ln2_b))
    norm_g = vec(ret_norm_g)

    tm = ROW_TILE
    pos_p = np.arange(seq)
    tabs_p = _rope_tables(pos_p, ROPE_THETA, ROPE_DIMS) + _rope_tables(pos_p, RET_THETA, RET_DK)
    x2d = x_prompt.reshape(batch * seq, D_MODEL)
    kv4t, wint, rq, rk, rv, rg, qt, gt, kaug, vst, kw, vwt = _project(x2d, w_perm, tabs_p, tm=tm, seq=seq, prompt=True)
    seq_pages = seq // PAGE_SIZE
    pidx_p = jnp.broadcast_to(jnp.arange(batch, dtype=jnp.int32)[:, None], (batch, seq_pages))
    poff_p = jnp.broadcast_to(jnp.arange(seq_pages, dtype=jnp.int32)[None, :] * PAGE_SIZE, (batch, seq_pages))
    kcvc_p = _compress(pidx_p, poff_p, kv4t, *cmp_w)
    o_nsa = _nsa_prompt(qt, gt, kcvc_p, kaug, vst, kw, vwt, jnp.asarray(_importance_matrix(seq // CMP_STRIDE).T, dtype=BF16), batch=batch, seq=seq)
    ret, st_p = _retention_prompt(rq, rk, rv, rg, _retention_tables(RET_CHUNK), norm_g, batch=batch, seq=seq)
    y_p = _tail(x2d, o_nsa, ret, *tail_consts, tm=tm, alpha=alpha)

    y_prompt = y_p.reshape(batch, seq, D_MODEL)
    to_rows = lambda a, n_slot: a.reshape(a.shape[0], n_slot, N_KV, HEAD_DIM, a.shape[-1]).transpose(0, 4, 1, 2, 3)[None]
    kv_prompt = to_rows(kv4t, 4)
    wlen = min(WINDOW, seq)
    win_prompt = to_rows(wint[:, :, seq - wlen:], 2)
    st6 = st_p.reshape(batch, N_PAIR, 2, RET_DK, 2, RET_DV)
    ret_prompt = jnp.stack([st6[:, :, 0, :, 0, :], st6[:, :, 1, :, 1, :]], axis=2).reshape(1, batch, N_RET_HEADS, RET_DK, RET_DV)

    pos_s = np.full((nb,), past)
    tabs_s = _rope_tables(pos_s, ROPE_THETA, ROPE_DIMS) + _rope_tables(pos_s, RET_THETA, RET_DK)
    xs2d = x_sample.reshape(nb, D_MODEL)
    kv4_d, win_d, rq_d, rk_d, rv_d, rg_d, q_d, gates_d = _project(xs2d, w_perm, tabs_s, tm=nb, seq=seq, prompt=False)
    cache_t = cache_kv[0].transpose(0, 2, 3, 4, 1).reshape(cache_kv.shape[1], 4 * KV_WIDTH, PAGE_SIZE)
    cwin_t = cache_win[0].transpose(0, 2, 3, 4, 1).reshape(nb, 2 * KV_WIDTH, WINDOW)
    kcvc_d = _compress(page_table, jnp.zeros_like(page_table), cache_t, *cmp_w)
    row3 = lambda a: a.reshape(nb, 1, a.shape[-1])
    q3 = row3(q_d)
    oc_d, idx_d = _dec_score(q3, kcvc_d, jnp.asarray(_importance_matrix(past // CMP_STRIDE), dtype=BF16), past=past)
    idx = idx_d[:, :N_KV, :TOP_N]
    state = state_ret[0].astype(F32).reshape(nb, N_PAIR, 2 * RET_DK, RET_DV)
    gl = jnp.asarray(np.repeat(np.exp(_log_gamma()), RET_DK)[None, :], dtype=F32)
    nsa_d, ret_d, win_st, st_d = _dec_attn(page_table, idx, q3, row3(gates_d), oc_d, row3(kv4_d), row3(win_d), cache_t, cwin_t,
                                          row3(rq_d), row3(rk_d), row3(rv_d), row3(rg_d), state, gl, norm_g)
    y_s = _tail(xs2d, nsa_d.reshape(nb, NSA_WIDTH).astype(BF16), ret_d.reshape(nb, RET_WIDTH).astype(BF16), *tail_consts, tm=nb, alpha=alpha)

    y_sample = y_s.reshape(nb, 1, D_MODEL)
    kv_sample = kv4_d.reshape(1, nb, 1, 4, N_KV, HEAD_DIM)
    win_sample = to_rows(win_st, 2)
    ret_sample = st_d.reshape(1, nb, N_RET_HEADS, RET_DK, RET_DV)
    return (y_prompt, y_sample, kv_prompt, kv_sample, win_prompt, win_sample, ret_prompt, ret_sample)
```

```python
import functools

import jax
import jax.numpy as jnp
import numpy as np
from jax import lax
from jax.experimental import pallas as pl
from jax.experimental.pallas import tpu as pltpu

F32 = jnp.float32
BF16 = jnp.bfloat16

D_MODEL = 1024
HEAD_DIM = 64
N_NSA_HEADS = 8
N_KV = 2
HPG = N_NSA_HEADS // N_KV
N_RET_HEADS = 8
RET_DK = 64
RET_DV = 64
NSA_WIDTH = N_NSA_HEADS * HEAD_DIM
RET_WIDTH = N_RET_HEADS * RET_DV
KV_WIDTH = N_KV * HEAD_DIM
D_FF = 4 * D_MODEL
CMP_STRIDE = 16
CMP_BLOCK = 32
SEL_BLOCK = 64
TOP_N = 16
N_FORCED = 3
WINDOW = 512
RET_CHUNK = 128
RET_CHUNKS_PER_STEP = 4
PAGE_SIZE = 128
ROPE_THETA = 500000.0
ROPE_DIMS = HEAD_DIM // 4
RET_THETA = 10000.0
LN_EPS = 1e-5
FORCE_BONUS = 1e4
NEG = -1e30
BELOW_NEG = -3e38

LANES = 128
SUBLANES = 8
VMEM_LIMIT = 56 * 1024 * 1024

LOG2E = 1.4426950408889634
BF16_SUBLANES = 16
ONES_ROWS = BF16_SUBLANES
CMP_VARIANT_ROWS = 128
CHUNK_PITCH = 20
PAGES_PER_STEP = 4
Q_TILE = 256
ROW_TILE = 512
KEY_TILE = 512
assert KEY_TILE % ROW_TILE == 0
N_PAIR = N_RET_HEADS // 2
N_SEL_BLOCKS = 128

U_Q, U_KV, U_RQ, U_RK, U_RV, U_RG, U_G, U_TOTAL = 0, 4, 10, 14, 18, 22, 26, 27
HEAD_PERM = (0, 4, 1, 5, 2, 6, 3, 7)


def _dot(a, b):
    return jnp.dot(a, b, preferred_element_type=F32)


def _dot_nt(a, b):
    return lax.dot_general(a, b, (((1,), (1,)), ((), ())), preferred_element_type=F32)


def _cparams(*sem):
    return pltpu.CompilerParams(dimension_semantics=sem, vmem_limit_bytes=VMEM_LIMIT)


def _const_spec(shape):
    nd = len(shape)
    return pl.BlockSpec(shape, lambda *_: (0,) * nd)


def _rope(u, c, s_fwd, s_bwd, half):
    return u * c + pltpu.roll(u, LANES - half, 1) * s_fwd + pltpu.roll(u, half, 1) * s_bwd


def _proj_kernel(x_ref, w_ref, cq_ref, sqf_ref, sqb_ref, cr_ref, srf_ref, srb_ref, *outs, tm, seq, prompt):
    if prompt:
        kv4t_o, wint_o, rq_o, rk_o, rv_o, rg_o, qt_o, gt_o, kaug_o, vst_o, kw_o, vwt_o = outs
    else:
        kv4_o, win_o, rq_o, rk_o, rv_o, rg_o, q_o, g_o = outs
    xb = x_ref[...].astype(BF16)

    def seg(u0, n):
        return _dot(xb, w_ref[:, u0 * LANES:(u0 + n) * LANES])

    def unit(z, u):
        return z[:, u * LANES:(u + 1) * LANES]

    cq, sqf, sqb = cq_ref[...], sqf_ref[...], sqb_ref[...]
    cr, srf, srb = cr_ref[...], srf_ref[...], srb_ref[...]
    nsa_half, ret_half = ROPE_DIMS // 2, RET_DK // 2

    zq = seg(U_Q, 4)
    q = jnp.concatenate([_rope(unit(zq, c), cq, sqf, sqb, nsa_half) for c in range(4)], axis=1) * (HEAD_DIM ** -0.5)
    gates = jax.nn.sigmoid(seg(U_G, 1))
    zkv = seg(U_KV, 6)
    k_slc = _rope(unit(zkv, 2), cq, sqf, sqb, nsa_half)
    k_win = _rope(unit(zkv, 4), cq, sqf, sqb, nsa_half)
    kv4 = jnp.concatenate([zkv[:, 0:2 * LANES], k_slc, unit(zkv, 3)], axis=1)
    win = jnp.concatenate([k_win, unit(zkv, 5)], axis=1)
    if prompt:
        kv4t = jnp.transpose(kv4)
        wint = jnp.transpose(win)
        kv4t_o[...] = kv4t
        wint_o[...] = wint
        qt_o[...] = jnp.transpose(q * LOG2E).astype(BF16)
        gt_o[...] = jnp.transpose(gates)
        row = pl.program_id(0) * tm + lax.broadcasted_iota(jnp.int32, (tm, 1), 0)
        blk = (row & (seq - 1)) >> 6
        onehot = lax.broadcasted_iota(jnp.int32, (tm, LANES), 1) == blk
        kaug_o[:, 0:LANES] = k_slc.astype(BF16)
        kaug_o[:, LANES:2 * LANES] = jnp.where(onehot, 1.0, 0.0).astype(BF16)
        vst_o[0:LANES, :] = kv4t[3 * LANES:4 * LANES].astype(BF16)
        vst_o[LANES:, :] = jnp.ones((ONES_ROWS, tm), BF16)
        kw_o[...] = k_win.astype(BF16)
        vwt = wint[LANES:2 * LANES].astype(BF16)
        for j in range(tm // LANES):
            vwt_o[j] = vwt[:, j * LANES:(j + 1) * LANES]
    else:
        kv4_o[...] = kv4
        win_o[...] = win
        q_o[...] = q.astype(BF16).astype(F32)
        g_o[...] = gates

    zrq = seg(U_RQ, 4)
    zrk = seg(U_RK, 4)
    for c in range(4):
        sl = slice(c * LANES, (c + 1) * LANES)
        rq_o[:, sl] = _rope(unit(zrq, c), cr, srf, srb, ret_half)
        rk_o[:, sl] = _rope(unit(zrk, c), cr, srf, srb, ret_half) * (RET_DK ** -0.5)
    rv_o[...] = seg(U_RV, 4)
    rg_o[...] = seg(U_RG, 4)


def _rope_tables(pos, theta, n_rot):
    half = n_rot // 2
    inv = theta ** (-np.arange(half, dtype=np.float64) / half)
    ang = np.asarray(pos, dtype=np.float64)[:, None] * inv[None, :]
    cos, sin = np.cos(ang), np.sin(ang)
    n = len(pos)
    rest = HEAD_DIM - n_rot
    c = np.concatenate([cos, cos, np.ones((n, rest))], 1)
    s_fwd = np.concatenate([-sin, np.zeros((n, half + rest))], 1)
    s_bwd = np.concatenate([np.zeros((n, half)), sin, np.zeros((n, rest))], 1)
    rep = LANES // HEAD_DIM
    return tuple(jnp.asarray(np.tile(t, (1, rep)), dtype=F32) for t in (c, s_fwd, s_bwd))


def _project(x2d, w_perm, tables, *, tm, seq, prompt):
    rows = x2d.shape[0]
    n_tab_tiles = tables[0].shape[0] // tm
    row_spec = lambda w: pl.BlockSpec((tm, w), lambda i: (i, 0))
    tab_spec = pl.BlockSpec((tm, LANES), lambda i: (i % n_tab_tiles, 0))
    if prompt:
        assert tm == ROW_TILE and seq % KEY_TILE == 0
        batch, tpb = rows // seq, seq // tm
        feat_spec = lambda f: pl.BlockSpec((None, f, tm), lambda i: (i // tpb, 0, i % tpb))
        outs = [(feat_spec(4 * LANES), (batch, 4 * LANES, seq), F32), (feat_spec(2 * LANES), (batch, 2 * LANES, seq), F32)]
    else:
        outs = [(row_spec(4 * LANES), (rows, 4 * LANES), F32), (row_spec(2 * LANES), (rows, 2 * LANES), F32)]
    outs += [(row_spec(4 * LANES), (rows, 4 * LANES), F32)] * 4
    if prompt:
        outs += [(feat_spec(4 * LANES), (batch, 4 * LANES, seq), BF16),
                 (feat_spec(LANES), (batch, LANES, seq), F32),
                 (row_spec(2 * LANES), (rows, 2 * LANES), BF16),
                 (pl.BlockSpec((None, LANES + ONES_ROWS, tm), lambda i: (i, 0, 0)), (rows // tm, LANES + ONES_ROWS, tm), BF16),
                 (row_spec(LANES), (rows, LANES), BF16),
                 (pl.BlockSpec((tm // LANES, LANES, LANES), lambda i: (i, 0, 0)), (rows // LANES, LANES, LANES), BF16)]
    else:
        outs += [(row_spec(4 * LANES), (rows, 4 * LANES), F32), (row_spec(LANES), (rows, LANES), F32)]
    return pl.pallas_call(
        functools.partial(_proj_kernel, tm=tm, seq=seq, prompt=prompt),
        grid=(rows // tm,),
        in_specs=[row_spec(D_MODEL), _const_spec(w_perm.shape)] + [tab_spec] * 6,
        out_specs=[s for s, _, _ in outs],
        out_shape=[jax.ShapeDtypeStruct(shape, dt) for _, shape, dt in outs],
        compiler_params=_cparams("parallel"),
        name="proj_prompt" if prompt else "proj_decode",
    )(x2d, w_perm, *tables)


def _cmp_kernel(pn_ref, po_ref, src_ref, wlo_ref, whi_ref, plo_ref, phi_ref, out_ref,
                stage, slab, sem, *, n_pages):
    b = pl.program_id(0)
    nb = pl.num_programs(0)
    cur = b % 2
    nxt = 1 - cur
    chunks_per_page = PAGE_SIZE // CMP_STRIDE
    page_rows = chunks_per_page * CHUNK_PITCH
    n_chunks = n_pages * chunks_per_page
    assert n_pages % CMP_STRIDE == 0
    pages_per_r = n_pages // CMP_STRIDE

    def page_copy(bb, sl, p):
        return pltpu.make_async_copy(
            src_ref.at[pn_ref[bb, p], pl.ds(0, 2 * LANES), pl.ds(pl.multiple_of(po_ref[bb, p], PAGE_SIZE), PAGE_SIZE)],
            stage.at[sl, p],
            sem.at[sl])

    def start_all(bb, sl):
        for p in range(n_pages):
            page_copy(bb, sl, p).start()

    def wait_all(bb, sl):
        for p in range(n_pages):
            page_copy(bb, sl, p).wait()

    def transpose_pages(sl, first, count):
        for u in range(count):
            p = first + u
            r0 = pl.multiple_of(p * page_rows, SUBLANES)
            for kv in range(2):
                x = jnp.transpose(stage[sl, p, kv * LANES:(kv + 1) * LANES, :])
                for c in range(chunks_per_page):
                    slab[sl, kv, pl.ds(r0 + c * CHUNK_PITCH, CMP_STRIDE), :] = x[c * CMP_STRIDE:(c + 1) * CMP_STRIDE]

    def compress_rows(with_next):
        acc_lo = jnp.zeros((n_chunks, 2 * LANES), F32)
        acc_hi = jnp.zeros((n_chunks, 2 * LANES), F32)
        for r in range(CMP_STRIDE):
            xr = jnp.concatenate([slab[cur, kv, pl.ds(r, n_chunks, stride=CHUNK_PITCH), :] for kv in range(2)], axis=1)
            acc_lo = acc_lo + _dot((xr + plo_ref[r:r + 1, :]).astype(BF16), wlo_ref[r])
            acc_hi = acc_hi + _dot((xr + phi_ref[r:r + 1, :]).astype(BF16), whi_ref[r])
            if with_next:
                transpose_pages(nxt, r * pages_per_r, pages_per_r)
        shifted = pltpu.roll(acc_hi, n_chunks - 1, 0)
        last = lax.broadcasted_iota(jnp.int32, (n_chunks, 1), 0) == n_chunks - 1
        out_ref[...] = jnp.where(last, 0.0, acc_lo + shifted)

    @pl.when(b == 0)
    def _():
        start_all(0, 0)

        @pl.when(nb > 1)
        def _():
            start_all(1, 1)

        wait_all(0, 0)

        def first_rows(pp, carry):
            transpose_pages(0, pp * PAGES_PER_STEP, PAGES_PER_STEP)
            return carry

        lax.fori_loop(0, n_pages // PAGES_PER_STEP, first_rows, 0)

    @pl.when(b + 2 < nb)
    def _():
        start_all(b + 2, cur)

    @pl.when(b + 1 < nb)
    def _():
        wait_all(b + 1, nxt)
        compress_rows(True)

    @pl.when(b + 1 >= nb)
    def _():
        compress_rows(False)


def _compress(page_index, page_offset, src, wlo, whi, plo, phi):
    nb, n_pages = page_index.shape
    n_chunks = n_pages * PAGE_SIZE // CMP_STRIDE
    grid_spec = pltpu.PrefetchScalarGridSpec(
        num_scalar_prefetch=2,
        grid=(nb,),
        in_specs=[pl.BlockSpec(memory_space=pl.ANY),
                  pl.BlockSpec(wlo.shape, lambda b, *_: (0, 0, 0)),
                  pl.BlockSpec(whi.shape, lambda b, *_: (0, 0, 0)),
                  pl.BlockSpec(plo.shape, lambda b, *_: (0, 0)),
                  pl.BlockSpec(phi.shape, lambda b, *_: (0, 0))],
        out_specs=pl.BlockSpec((None, n_chunks, 2 * LANES), lambda b, *_: (b, 0, 0)),
        scratch_shapes=[pltpu.VMEM((2, n_pages, 2 * LANES, PAGE_SIZE), F32),
                        pltpu.VMEM((2, 2, n_chunks * CHUNK_PITCH, LANES), F32),
                        pltpu.SemaphoreType.DMA((2,))],
    )
    return pl.pallas_call(
        functools.partial(_cmp_kernel, n_pages=n_pages),
        grid_spec=grid_spec,
        out_shape=jax.ShapeDtypeStruct((nb, n_chunks, 2 * LANES), F32),
        compiler_params=_cparams("arbitrary"),
        name="compress",
    )(page_index, page_offset, src, wlo, whi, plo, phi)


def _compress_weights(w_k, w_v, pos_k, pos_v):
    def bd(wk_r, wv_r):
        z = jnp.zeros((HEAD_DIM, HEAD_DIM), F32)
        rows = [[wk_r, z, z, z], [z, wk_r, z, z], [z, z, wv_r, z], [z, z, z, wv_r]]
        return jnp.block(rows)

    w_all = jax.vmap(bd)(w_k, w_v).astype(BF16)
    p_all = jnp.concatenate([pos_k, pos_k, pos_v, pos_v], axis=1)
    return w_all[:CMP_STRIDE], w_all[CMP_STRIDE:], p_all[:CMP_STRIDE], p_all[CMP_STRIDE:]


def _masked_softmax(s, mask):
    s = jnp.where(mask, s, NEG)
    m = jnp.max(s, -1, keepdims=True)
    p = jnp.exp(s - m) * mask.astype(F32)
    return p / jnp.maximum(jnp.sum(p, -1, keepdims=True), 1e-30)


def _split3(x):
    hi = x.astype(BF16)
    r1 = x - hi.astype(F32)
    mid = r1.astype(BF16)
    lo = (r1 - mid.astype(F32)).astype(BF16)
    return hi, mid, lo


def _importance_matrix(n_cmp):
    n = np.arange(n_cmp)[:, None]
    j = np.arange(N_SEL_BLOCKS)[None, :]
    inner = (n >= 4 * j) & (n <= 4 * j + 2)
    edge = (n == 4 * j - 1) | (n == 4 * j + 3)
    return np.where(inner, 2.0, np.where(edge, 1.0, 0.0))


def _softmax_over_keys(s, mask):
    s = jnp.where(mask, s, NEG)
    m = jnp.max(s, 0, keepdims=True)
    p = jnp.exp2(s - m)
    inv = jnp.where(m > 0.5 * NEG, 1.0 / jnp.maximum(jnp.sum(p, 0, keepdims=True), 1e-30), 0.0)
    return p * inv


def _nsa_kernel(qt_ref, gt_ref, kcvc_ref, kaug_ref, vst_ref, kw_ref, vwt_ref, mt_ref, out_ref,
                kc_s, vct_s, m_s, acc_s, s_buf, tmax_s, qaug_s, ocw_s):
    i = pl.program_id(1)
    t0 = i * Q_TILE
    n_cmp = kcvc_ref.shape[0]

    @pl.when(i == 0)
    def _():
        kcvc = kcvc_ref[...]
        kc_s[...] = kcvc[:, 0:LANES].astype(BF16)
        vct_s[...] = jnp.transpose(kcvc[:, LANES:2 * LANES]).astype(BF16)

    lane = lax.broadcasted_iota(jnp.int32, (1, Q_TILE), 1)
    tq = jnp.concatenate([t0 + lane] * N_NSA_HEADS, axis=1)
    low_rows = lax.broadcasted_iota(jnp.int32, (LANES, 1), 0) < HEAD_DIM
    qt = qt_ref[...]
    zero = jnp.zeros((), BF16)
    q_all = jnp.concatenate([jnp.where(low_rows if g == 0 else ~low_rows, qt[c * LANES:(c + 1) * LANES], zero)
                             for g in range(N_KV) for c in range(HPG)], axis=1)

    qaug_s[0:LANES, :] = q_all
    gt = gt_ref[...]
    gk = [jnp.concatenate([gt[3 * h + k:3 * h + k + 1] for h in range(N_NSA_HEADS)], axis=1) for k in range(3)]
    cur = jnp.concatenate([(t0 + lane) >> 6] * N_KV, axis=1)

    def compressed_and_select(n_act, j_act):
        cend = lax.broadcasted_iota(jnp.int32, (n_act, 1), 0) * CMP_STRIDE + (CMP_BLOCK - 1)
        p_c = _softmax_over_keys(_dot(kc_s[0:n_act, :], q_all), cend <= tq)
        ocw_s[...] = gk[0] * _dot(vct_s[:, 0:n_act], p_c.astype(BF16))
        psum = jnp.concatenate([sum(p_c[:, (g * HPG + c) * Q_TILE:(g * HPG + c + 1) * Q_TILE] for c in range(HPG))
                                for g in range(N_KV)], axis=1)
        mt = mt_ref[0:j_act, 0:n_act]
        imp_t = sum(_dot(mt, part) for part in _split3(psum))
        blk = lax.broadcasted_iota(jnp.int32, (j_act, 1), 0)
        blk_f = blk.astype(F32)
        forced = (blk == 0) | (blk == cur) | (blk == cur - 1)
        score = jnp.where(forced, BELOW_NEG, jnp.where(blk <= cur, imp_t, NEG))
        bias = jnp.where(forced, 0.0, NEG)
        for _ in range(TOP_N - N_FORCED):
            best = jnp.max(score, axis=0, keepdims=True)
            first = jnp.min(jnp.where(score == best, blk_f, float(N_SEL_BLOCKS)), axis=0, keepdims=True)
            pick = blk_f == first
            bias = jnp.where(pick, 0.0, bias)
            score = jnp.where(pick, BELOW_NEG, score)
        bias = bias.astype(BF16)
        qaug_s[LANES:LANES + j_act, :] = jnp.concatenate(
            [bias[:, g * Q_TILE:(g + 1) * Q_TILE] for g in range(N_KV) for _ in range(HPG)], axis=1)
        if j_act < N_SEL_BLOCKS:
            qaug_s[LANES + j_act:, :] = jnp.full((N_SEL_BLOCKS - j_act, qaug_s.shape[1]), NEG, BF16)

    q_blocks_per_variant = CMP_VARIANT_ROWS * CMP_STRIDE // Q_TILE
    n_variants = n_cmp // CMP_VARIANT_ROWS
    for v in range(n_variants):
        @pl.when(i // q_blocks_per_variant == v)
        def _(v=v):
            compressed_and_select((v + 1) * CMP_VARIANT_ROWS, min(N_SEL_BLOCKS, (v + 1) * CMP_VARIANT_ROWS * CMP_STRIDE // SEL_BLOCK))

    m_s[...] = jnp.full(m_s.shape, NEG, F32)
    acc_s[...] = jnp.zeros(acc_s.shape, F32)

    def scores(kt, causal):
        k0 = pl.multiple_of(kt * KEY_TILE, KEY_TILE)
        s = _dot(kaug_ref[pl.ds(k0, KEY_TILE), :], qaug_s[...])
        if causal:
            kpos = k0 + lax.broadcasted_iota(jnp.int32, (KEY_TILE, 1), 0)
            s = jnp.where(kpos <= tq, s, NEG)
        s_buf[...] = s
        tmax_s[...] = jnp.max(s, 0, keepdims=True)

    def accumulate(kt):
        m_old = m_s[...]
        m_new = jnp.maximum(m_old, tmax_s[...])
        alpha = jnp.exp2(m_old - m_new)
        p = jnp.exp2(s_buf[...] - m_new).astype(BF16)
        n_sub = KEY_TILE // ROW_TILE
        pv = sum(_dot(vst_ref[kt * n_sub + j], p[j * ROW_TILE:(j + 1) * ROW_TILE]) for j in range(n_sub))
        acc_s[...] = alpha * acc_s[...] + pv
        m_s[...] = m_new

    last = (t0 + Q_TILE - 1) // KEY_TILE

    @pl.when(last == 0)
    def _():
        scores(0, True)

    @pl.when(last > 0)
    def _():
        scores(0, False)

        def body(kt, carry):
            accumulate(kt)
            scores(kt + 1, False)
            return carry

        lax.fori_loop(0, last - 1, body, 0)
        accumulate(last - 1)
        scores(last, True)

    accumulate(last)
    o_s = acc_s[0:LANES, :] * (1.0 / acc_s[LANES:LANES + 1, :])

    w0 = pl.multiple_of(jnp.maximum(t0 - WINDOW, 0), Q_TILE)
    n_win = WINDOW + Q_TILE
    dist = tq - (w0 + lax.broadcasted_iota(jnp.int32, (n_win, 1), 0))
    p_w = _softmax_over_keys(_dot(kw_ref[pl.ds(w0, n_win), :], q_all), (dist >= 0) & (dist < WINDOW))
    w_blk = w0 // LANES
    vwt = jnp.concatenate([vwt_ref[w_blk + j] for j in range(n_win // LANES)], axis=1)
    o_w = _dot(vwt, p_w.astype(BF16))

    o = ocw_s[...] + gk[2] * o_w + gk[1] * o_s
    half = HPG * Q_TILE
    comb = jnp.where(low_rows, o[:, 0:half], o[:, half:2 * half])
    for c in range(HPG):
        out_ref[:, c * LANES:(c + 1) * LANES] = jnp.transpose(comb[:, c * Q_TILE:(c + 1) * Q_TILE]).astype(BF16)


def _nsa_prompt(qt, gt, kcvc, kaug, vst, kw, vwt, mt, *, batch, seq):
    nq = seq // Q_TILE
    n_cols = N_NSA_HEADS * Q_TILE
    n_cmp = kcvc.shape[1]
    col_spec = lambda f: pl.BlockSpec((None, f, Q_TILE), lambda b, i: (b, 0, i))
    seq_spec = lambda w: pl.BlockSpec((seq, w), lambda b, i: (b, 0))
    return pl.pallas_call(
        _nsa_kernel,
        grid=(batch, nq),
        in_specs=[col_spec(4 * LANES), col_spec(LANES),
                  pl.BlockSpec((None, n_cmp, 2 * LANES), lambda b, i: (b, 0, 0)),
                  seq_spec(2 * LANES),
                  pl.BlockSpec((seq // ROW_TILE, LANES + ONES_ROWS, ROW_TILE), lambda b, i: (b, 0, 0)),
                  seq_spec(LANES),
                  pl.BlockSpec((seq // LANES, LANES, LANES), lambda b, i: (b, 0, 0)),
                  _const_spec(mt.shape)],
        out_specs=pl.BlockSpec((Q_TILE, 4 * LANES), lambda b, i: (b * nq + i, 0)),
        out_shape=jax.ShapeDtypeStruct((batch * seq, 4 * LANES), BF16),
        scratch_shapes=[pltpu.VMEM((n_cmp, LANES), BF16), pltpu.VMEM((LANES, n_cmp), BF16),
                        pltpu.VMEM((1, n_cols), F32), pltpu.VMEM((LANES + ONES_ROWS, n_cols), F32),
                        pltpu.VMEM((KEY_TILE, n_cols), F32), pltpu.VMEM((1, n_cols), F32),
                        pltpu.VMEM((2 * LANES, n_cols), BF16), pltpu.VMEM((LANES, n_cols), F32)],
        compiler_params=_cparams("parallel", "arbitrary"),
        name="nsa_prompt",
    )(qt, gt, kcvc, kaug, vst, kw, vwt, mt)


def _head_sum(a, low):
    s0 = jnp.sum(jnp.where(low, a, 0.0), -1, keepdims=True)
    s1 = jnp.sum(jnp.where(low, 0.0, a), -1, keepdims=True)
    return jnp.where(low, s0, s1)


def _gated_group_norm(o, rg, norm_g, low):
    mu = _head_sum(o, low) * (1.0 / RET_DV)
    d = o - mu
    var = _head_sum(d * d, low) * (1.0 / RET_DV)
    gn = d * lax.rsqrt(var + LN_EPS) * norm_g
    return (rg * jax.nn.sigmoid(rg)) * gn


def _ret_kernel(rq_ref, rk_ref, rv_ref, rg_ref, decay_ref, qdec_ref, kdec_ref, gcl_ref, bd_ref, ng_ref,
                ret_o, st_o, st_s):
    c = pl.program_id(1)

    @pl.when(c == 0)
    def _():
        st_s[...] = jnp.zeros(st_s.shape, F32)

    low = lax.broadcasted_iota(jnp.int32, (1, LANES), 1) < RET_DK
    zero = jnp.zeros((), BF16)
    for j in range(RET_CHUNKS_PER_STEP):
        rows = slice(j * RET_CHUNK, (j + 1) * RET_CHUNK)
        for p in range(N_PAIR):
            sl = slice(p * LANES, (p + 1) * LANES)
            q, k, v = rq_ref[rows, sl], rk_ref[rows, sl], rv_ref[rows, sl]
            qb, kb, vb = q.astype(BF16), k.astype(BF16), v.astype(BF16)
            in0 = _dot_nt(jnp.where(low, qb, zero), kb) * decay_ref[2 * p]
            in1 = _dot_nt(jnp.where(low, zero, qb), kb) * decay_ref[2 * p + 1]
            inner = jnp.concatenate([in0.astype(BF16), in1.astype(BF16)], axis=1)
            v_split = jnp.concatenate([jnp.where(low, vb, zero), jnp.where(low, zero, vb)], axis=0)
            state = st_s[p]
            o = _dot(inner, v_split) + _dot((q * qdec_ref[:, sl]).astype(BF16), state.astype(BF16))
            kd_t = jnp.transpose(k * kdec_ref[:, sl]).astype(BF16)
            st_s[p] = state * gcl_ref[:, sl] + _dot(kd_t, vb) * bd_ref[...]
            ret_o[rows, sl] = _gated_group_norm(o, rg_ref[rows, sl], ng_ref[:, sl], low).astype(BF16)

    @pl.when(c == pl.num_programs(1) - 1)
    def _():
        st_o[...] = st_s[...]


def _log_gamma():
    return np.log1p(-np.exp2(-5.0 - np.arange(N_RET_HEADS, dtype=np.float64)))


def _retention_tables(chunk):
    log_gamma = _log_gamma()
    i = np.arange(chunk, dtype=np.float64)
    diff = i[:, None] - i[None, :]
    decay = np.where(diff >= 0, np.exp(log_gamma[:, None, None] * np.maximum(diff, 0.0)), 0.0)
    per_lane = lambda t: np.repeat(t, RET_DK, axis=-1)
    qdec = per_lane(np.exp((i + 1.0)[:, None] * log_gamma[None, :]))
    kdec = per_lane(np.exp((chunk - 1.0 - i)[:, None] * log_gamma[None, :]))
    gcl = per_lane(np.exp(chunk * log_gamma)[None, :])
    r = np.arange(LANES)
    bd = (r[:, None] // RET_DK) == (r[None, :] // RET_DV)
    return tuple(jnp.asarray(t, dtype=F32) for t in (decay, qdec, kdec, gcl, bd))


def _retention_prompt(rq, rk, rv, rg, tables, norm_g, *, batch, seq):
    decay, qdec, kdec, gcl, bd = tables
    step_rows = RET_CHUNK * RET_CHUNKS_PER_STEP
    nc = seq // step_rows
    row_spec = pl.BlockSpec((step_rows, RET_WIDTH), lambda b, c: (b * nc + c, 0))
    return pl.pallas_call(
        _ret_kernel,
        grid=(batch, nc),
        in_specs=[row_spec] * 4 + [_const_spec(decay.shape), _const_spec(qdec.shape), _const_spec(kdec.shape),
                                   _const_spec(gcl.shape), _const_spec(bd.shape), _const_spec(norm_g.shape)],
        out_specs=[row_spec, pl.BlockSpec((None, N_PAIR, LANES, LANES), lambda b, c: (b, 0, 0, 0))],
        out_shape=[jax.ShapeDtypeStruct((batch * seq, RET_WIDTH), BF16),
                   jax.ShapeDtypeStruct((batch, N_PAIR, LANES, LANES), F32)],
        scratch_shapes=[pltpu.VMEM((N_PAIR, LANES, LANES), F32)],
        compiler_params=_cparams("parallel", "arbitrary"),
        name="retention",
    )(rq, rk, rv, rg, decay, qdec, kdec, gcl, bd, norm_g)


def _layer_norm(v, g, b):
    mu = jnp.mean(v, -1, keepdims=True)
    d = v - mu
    var = jnp.mean(d * d, -1, keepdims=True)
    return d * lax.rsqrt(var + LN_EPS) * g + b


def _tail_kernel(x_ref, nsa_ref, ret_ref, wo_ref, g1_ref, b1_ref, wu_ref, wd_ref, g2_ref, b2_ref, y_ref, *, alpha):
    mix = _dot(nsa_ref[...], wo_ref[0:NSA_WIDTH, :]) + _dot(ret_ref[...], wo_ref[NSA_WIDTH:, :])
    x1 = _layer_norm(alpha * x_ref[...] + mix, g1_ref[...], b1_ref[...])
    x1b = x1.astype(BF16)
    f = jnp.zeros(x1.shape, F32)
    for j in range(D_FF // D_MODEL):
        sl = slice(j * D_MODEL, (j + 1) * D_MODEL)
        h = jnp.maximum(_dot(x1b, wu_ref[:, sl]), 0.0)
        f = f + _dot((h * h).astype(BF16), wd_ref[sl, :])
    y_ref[...] = _layer_norm(alpha * x1 + f, g2_ref[...], b2_ref[...])


def _tail(x2d, nsa, ret, wo, g1, b1, wu, wd, g2, b2, *, tm, alpha):
    rows = x2d.shape[0]
    row_spec = lambda w: pl.BlockSpec((tm, w), lambda i: (i, 0))
    consts = (wo, g1, b1, wu, wd, g2, b2)
    return pl.pallas_call(
        functools.partial(_tail_kernel, alpha=alpha),
        grid=(rows // tm,),
        in_specs=[row_spec(D_MODEL), row_spec(NSA_WIDTH), row_spec(RET_WIDTH)] + [_const_spec(c.shape) for c in consts],
        out_specs=row_spec(D_MODEL),
        out_shape=jax.ShapeDtypeStruct((rows, D_MODEL), F32),
        compiler_params=_cparams("parallel"),
        name="tail",
    )(x2d, nsa, ret, *consts)


DEC_ROWS = 16
N_DEC_SEL = TOP_N - 1


def _dec_stack_q(q, low):
    rows = lax.broadcasted_iota(jnp.int32, (DEC_ROWS, 1), 0)
    qs = jnp.zeros((DEC_ROWS, LANES), F32)
    for g in range(N_KV):
        for c in range(HPG):
            piece = jnp.where(low if g == 0 else ~low, q[:, c * LANES:(c + 1) * LANES], 0.0)
            qs = jnp.where(rows == g * HPG + c, piece, qs)
    return qs, rows


def _dec_score_kernel(q_ref, kcvc_ref, m_ref, oc_o, idx_o, score_s, *, past):
    lane = lax.broadcasted_iota(jnp.int32, (1, LANES), 1)
    low = lane < HEAD_DIM
    qs, rows = _dec_stack_q(q_ref[...], low)
    qsb = qs.astype(BF16)
    kcvc = kcvc_ref[...]
    kc = kcvc[:, 0:LANES].astype(BF16)
    vc = kcvc[:, LANES:2 * LANES].astype(BF16)
    n_cmp = kcvc.shape[0]
    cend = lax.broadcasted_iota(jnp.int32, (1, n_cmp), 1) * CMP_STRIDE + (CMP_BLOCK - 1)
    p = _masked_softmax(_dot_nt(qsb, kc), jnp.broadcast_to(cend <= past, (DEC_ROWS, n_cmp)))
    oc_o[...] = _dot(p.astype(BF16), vc)

    psum = jnp.zeros((DEC_ROWS, n_cmp), F32)
    for g in range(N_KV):
        acc = p[g * HPG:g * HPG + 1]
        for c in range(1, HPG):
            acc = acc + p[g * HPG + c:g * HPG + c + 1]
        psum = jnp.where(rows == g, acc, psum)
    m = m_ref[...]
    imp = sum(_dot(part, m) for part in _split3(psum))
    forced = (lane == 0) | (lane == N_SEL_BLOCKS - 1)
    b = pl.program_id(0)
    score_s[b] = imp + FORCE_BONUS * forced.astype(F32)

    @pl.when(b == pl.num_programs(0) - 1)
    def _():
        lane_f = lane.astype(F32)
        score = score_s[...].reshape(-1, LANES)
        picked = jnp.zeros(score.shape, F32)
        for r in range(N_DEC_SEL):
            best = jnp.max(score, -1, keepdims=True)
            first = jnp.min(jnp.where(score == best, lane_f, float(LANES)), -1, keepdims=True)
            picked = jnp.where(lane == r, first, picked)
            score = jnp.where(lane_f == first, BELOW_NEG, score)
        idx_o[...] = picked.astype(jnp.int32).reshape(idx_o.shape)


def _dec_score(q3, kcvc, m, *, past):
    nb = q3.shape[0]
    n_cmp = kcvc.shape[1]
    return pl.pallas_call(
        functools.partial(_dec_score_kernel, past=past),
        grid=(nb,),
        in_specs=[pl.BlockSpec((None, 1, 4 * LANES), lambda b: (b, 0, 0)),
                  pl.BlockSpec((None, n_cmp, 2 * LANES), lambda b: (b, 0, 0)),
                  _const_spec(m.shape)],
        out_specs=[pl.BlockSpec((None, DEC_ROWS, LANES), lambda b: (b, 0, 0)),
                   _const_spec((nb, DEC_ROWS, LANES))],
        out_shape=[jax.ShapeDtypeStruct((nb, DEC_ROWS, LANES), F32), jax.ShapeDtypeStruct((nb, DEC_ROWS, LANES), jnp.int32)],
        scratch_shapes=[pltpu.VMEM((nb, DEC_ROWS, LANES), F32)],
        compiler_params=_cparams("arbitrary"),
        name="dec_score",
    )(q3, kcvc, m)


def _round_bf16(x):
    return x.astype(BF16).astype(F32)


def _dec_attn_kernel(pt_ref, idx_ref, q_ref, g_ref, oc_ref, kvn_ref, wn_ref, cache_ref, cw_ref,
                     rq_ref, rk_ref, rv_ref, rg_ref, st_ref, gl_ref, ng_ref,
                     nsa_o, ret_o, win_o, st_o, kbuf, sem):
    b = pl.program_id(0)
    nb = pl.num_programs(0)
    slot = b % 2

    def blk_copy(bb, sl, g, k):
        j = idx_ref[bb, g, k]
        return pltpu.make_async_copy(
            cache_ref.at[pt_ref[bb, j >> 1], pl.ds(2 * LANES, 2 * LANES), :],
            kbuf.at[sl, g, :, pl.ds(k * PAGE_SIZE, PAGE_SIZE)],
            sem.at[sl])

    def start_all(bb, sl):
        for g in range(N_KV):
            for k in range(N_DEC_SEL):
                blk_copy(bb, sl, g, k).start()

    @pl.when(b == 0)
    def _():
        start_all(0, 0)

    @pl.when(b + 1 < nb)
    def _():
        start_all(b + 1, 1 - slot)

    lane = lax.broadcasted_iota(jnp.int32, (1, LANES), 1)
    low = lane < HEAD_DIM
    qs, rows = _dec_stack_q(q_ref[...], low)
    qsb = qs.astype(BF16)
    g0_rows = rows < HPG

    cw = cw_ref[...]
    wb = cw.shape[1]
    wn = wn_ref[...]
    col = lax.broadcasted_iota(jnp.int32, (1, wb), 1)
    valid = jnp.broadcast_to((wb - col) < WINDOW, (DEC_ROWS, wb))
    s_w = jnp.where(valid, _dot(qsb, cw[0:LANES].astype(BF16)), NEG)
    s_wn = jnp.sum(qs * _round_bf16(wn[:, 0:LANES]), -1, keepdims=True)
    m_w = jnp.maximum(jnp.max(s_w, -1, keepdims=True), s_wn)
    p_w = jnp.exp(s_w - m_w) * valid.astype(F32)
    p_wn = jnp.exp(s_wn - m_w)
    l_w = jnp.maximum(jnp.sum(p_w, -1, keepdims=True) + p_wn, 1e-30)
    o_w = _dot_nt((p_w / l_w).astype(BF16), cw[LANES:2 * LANES].astype(BF16)) + _round_bf16(p_wn / l_w) * _round_bf16(wn[:, LANES:2 * LANES])
    new_cols = jnp.concatenate(
        [jnp.concatenate([jnp.transpose(jnp.broadcast_to(wn[:, u * LANES:(u + 1) * LANES], (LANES, LANES)))] * (wb // LANES), axis=1)
         for u in range(2)], axis=0)
    win_o[...] = jnp.where(col == wb - 1, new_cols, pltpu.roll(cw, wb - 1, 1))

    for g in range(N_KV):
        for k in range(N_DEC_SEL):
            blk_copy(b, slot, g, k).wait()
    kvn = kvn_ref[...]
    s_g = [_dot(qsb, kbuf[slot, g, 0:LANES, :].astype(BF16)) for g in range(N_KV)]
    pos = lax.broadcasted_iota(jnp.int32, (1, N_DEC_SEL * PAGE_SIZE), 1)
    picked_half = []
    for g in range(N_KV):
        half = jnp.zeros(pos.shape, jnp.int32)
        for k in range(N_DEC_SEL):
            half = jnp.where(pos >> 7 == k, idx_ref[b, g, k] & 1, half)
        picked_half.append(half)
    in_block = ((pos >> 6) & 1) == jnp.where(g0_rows, picked_half[0], picked_half[1])
    s_s = jnp.where(in_block, jnp.where(g0_rows, s_g[0], s_g[1]), NEG)
    s_sn = jnp.sum(qs * _round_bf16(kvn[:, 2 * LANES:3 * LANES]), -1, keepdims=True)
    m_sel = jnp.maximum(jnp.max(s_s, -1, keepdims=True), s_sn)
    p_s = jnp.exp(s_s - m_sel)
    p_sn = jnp.exp(s_sn - m_sel)
    l_sel = jnp.maximum(jnp.sum(p_s, -1, keepdims=True) + p_sn, 1e-30)
    pn = (p_s / l_sel).astype(BF16)
    o_g = [_dot_nt(pn, kbuf[slot, g, LANES:2 * LANES, :].astype(BF16)) for g in range(N_KV)]
    o_s = jnp.where(g0_rows, o_g[0], o_g[1]) + _round_bf16(p_sn / l_sel) * _round_bf16(kvn[:, 3 * LANES:4 * LANES])

    gates = g_ref[...]
    gcol = [jnp.zeros((DEC_ROWS, 1), F32) for _ in range(3)]
    for h in range(N_NSA_HEADS):
        for k in range(3):
            gcol[k] = jnp.where(rows == h, gates[:, 3 * h + k:3 * h + k + 1], gcol[k])
    o = gcol[0] * oc_ref[...] + gcol[1] * o_s + gcol[2] * o_w
    for c in range(HPG):
        nsa_o[:, c * LANES:(c + 1) * LANES] = jnp.where(low, o[c:c + 1], o[HPG + c:HPG + c + 1])

    top = lax.broadcasted_iota(jnp.int32, (LANES, 1), 0) < RET_DK
    for p in range(N_PAIR):
        sl = slice(p * LANES, (p + 1) * LANES)
        q, k, v, gl = rq_ref[:, sl], rk_ref[:, sl], rv_ref[:, sl], gl_ref[:, sl]
        state = st_ref[p]
        inner = _head_sum(q * k, low)

        def per_row(row):
            return jnp.transpose(jnp.broadcast_to(row, (LANES, LANES)))[:, 0:RET_DV]

        v_rows = jnp.where(top, v[:, 0:RET_DV], v[:, RET_DV:2 * RET_DV])
        qs_state = per_row(q * gl) * state
        cross = jnp.concatenate([jnp.sum(qs_state[0:RET_DK], axis=0, keepdims=True),
                                 jnp.sum(qs_state[RET_DK:2 * RET_DK], axis=0, keepdims=True)], axis=1)
        o_r = inner * v + cross
        st_o[p] = per_row(gl) * state + per_row(k) * v_rows
        ret_o[:, sl] = _gated_group_norm(o_r, rg_ref[:, sl], ng_ref[:, sl], low)


def _dec_attn(page_table, idx, q3, gates3, oc, kvn3, wn3, cache4, cwin, rq3, rk3, rv3, rg3, state, gl, ng):
    nb = q3.shape[0]
    wb = cwin.shape[2]
    row3 = lambda w: pl.BlockSpec((None, 1, w), lambda b, *_: (b, 0, 0))
    st_spec = pl.BlockSpec((None, N_PAIR, 2 * RET_DK, RET_DV), lambda b, *_: (b, 0, 0, 0))
    win_spec = pl.BlockSpec((None, 2 * LANES, wb), lambda b, *_: (b, 0, 0))
    grid_spec = pltpu.PrefetchScalarGridSpec(
        num_scalar_prefetch=2,
        grid=(nb,),
        in_specs=[row3(4 * LANES), row3(LANES), pl.BlockSpec((None, DEC_ROWS, LANES), lambda b, *_: (b, 0, 0)),
                  row3(4 * LANES), row3(2 * LANES), pl.BlockSpec(memory_space=pl.ANY), win_spec,
                  row3(4 * LANES), row3(4 * LANES), row3(4 * LANES), row3(4 * LANES), st_spec,
                  pl.BlockSpec(gl.shape, lambda b, *_: (0, 0)), pl.BlockSpec(ng.shape, lambda b, *_: (0, 0))],
        out_specs=[row3(4 * LANES), row3(4 * LANES), win_spec, st_spec],
        scratch_shapes=[pltpu.VMEM((2, N_KV, 2 * LANES, N_DEC_SEL * PAGE_SIZE), F32), pltpu.SemaphoreType.DMA((2,))],
    )
    return pl.pallas_call(
        _dec_attn_kernel,
        grid_spec=grid_spec,
        out_shape=[jax.ShapeDtypeStruct((nb, 1, 4 * LANES), F32), jax.ShapeDtypeStruct((nb, 1, 4 * LANES), F32),
                   jax.ShapeDtypeStruct((nb, 2 * LANES, wb), F32), jax.ShapeDtypeStruct(state.shape, F32)],
        compiler_params=_cparams("arbitrary"),
        name="dec_attn",
    )(page_table, idx, q3, gates3, oc, kvn3, wn3, cache4, cwin, rq3, rk3, rv3, rg3, state, gl, ng)


def _permute_w_in(w):
    p0 = NSA_WIDTH
    p1 = p0 + 6 * KV_WIDTH
    p2 = p1 + 3 * N_NSA_HEADS
    wq = w[:, :p0].reshape(D_MODEL, N_KV, HPG, HEAD_DIM).transpose(0, 2, 1, 3).reshape(D_MODEL, p0)
    w_gates = jnp.pad(w[:, p1:p2], ((0, 0), (0, LANES - 3 * N_NSA_HEADS)))
    return jnp.concatenate([wq, w[:, p0:p1], w[:, p2:], w_gates], axis=1).astype(BF16)


def _permute_w_o(w):
    w_nsa = w[:NSA_WIDTH].reshape(N_KV, HPG, HEAD_DIM, D_MODEL).transpose(1, 0, 2, 3).reshape(NSA_WIDTH, D_MODEL)
    return jnp.concatenate([w_nsa, w[NSA_WIDTH:]], axis=0).astype(BF16)


def kernel(x_prompt, x_sample, cache_kv, cache_win, state_ret, page_table, w_in, w_cmp_k, w_cmp_v, pos_cmp_k, pos_cmp_v, ret_norm_g, w_o, ln1_g, ln1_b, w_up, w_down, ln2_g, ln2_b):
    depth = w_in.shape[0]
    assert depth == 1 and x_sample.shape[1] == 1
    batch, seq, _ = x_prompt.shape
    nb = x_sample.shape[0]
    n_pages = page_table.shape[1]
    past = n_pages * PAGE_SIZE
    assert past // SEL_BLOCK == N_SEL_BLOCKS and cache_win.shape[2] == WINDOW
    alpha = (2 * depth) ** 0.25

    w_perm = _permute_w_in(w_in[0])
    wo = _permute_w_o(w_o[0])
    wu, wd = w_up[0].astype(BF16), w_down[0].astype(BF16)
    vec = lambda a: a[0].reshape(1, -1)
    cmp_w = _compress_weights(w_cmp_k[0], w_cmp_v[0], pos_cmp_k[0], pos_cmp_v[0])
    tail_consts = (wo, vec(ln1_g), vec(ln1_b), wu, wd, vec(ln2_g), vec(ln2_b))
    norm_g = vec(ret_norm_g)

    tm = ROW_TILE
    pos_p = np.arange(seq)
    tabs_p = _rope_tables(pos_p, ROPE_THETA, ROPE_DIMS) + _rope_tables(pos_p, RET_THETA, RET_DK)
    x2d = x_prompt.reshape(batch * seq, D_MODEL)
    kv4t, wint, rq, rk, rv, rg, qt, gt, kaug, vst, kw, vwt = _project(x2d, w_perm, tabs_p, tm=tm, seq=seq, prompt=True)
    seq_pages = seq // PAGE_SIZE
    pidx_p = jnp.broadcast_to(jnp.arange(batch, dtype=jnp.int32)[:, None], (batch, seq_pages))
    poff_p = jnp.broadcast_to(jnp.arange(seq_pages, dtype=jnp.int32)[None, :] * PAGE_SIZE, (batch, seq_pages))
    kcvc_p = _compress(pidx_p, poff_p, kv4t, *cmp_w)
    o_nsa = _nsa_prompt(qt, gt, kcvc_p, kaug, vst, kw, vwt, jnp.asarray(_importance_matrix(seq // CMP_STRIDE).T, dtype=BF16), batch=batch, seq=seq)
    ret, st_p = _retention_prompt(rq, rk, rv, rg, _retention_tables(RET_CHUNK), norm_g, batch=batch, seq=seq)
    y_p = _tail(x2d, o_nsa, ret, *tail_consts, tm=tm, alpha=alpha)

    y_prompt = y_p.reshape(batch, seq, D_MODEL)
    to_rows = lambda a, n_slot: a.reshape(a.shape[0], n_slot, N_KV, HEAD_DIM, a.shape[-1]).transpose(0, 4, 1, 2, 3)[None]
    kv_prompt = to_rows(kv4t, 4)
    wlen = min(WINDOW, seq)
    win_prompt = to_rows(wint[:, :, seq - wlen:], 2)
    st6 = st_p.reshape(batch, N_PAIR, 2, RET_DK, 2, RET_DV)
    ret_prompt = jnp.stack([st6[:, :, 0, :, 0, :], st6[:, :, 1, :, 1, :]], axis=2).reshape(1, batch, N_RET_HEADS, RET_DK, RET_DV)

    pos_s = np.full((nb,), past)
    tabs_s = _rope_tables(pos_s, ROPE_THETA, ROPE_DIMS) + _rope_tables(pos_s, RET_THETA, RET_DK)
    xs2d = x_sample.reshape(nb, D_MODEL)
    kv4_d, win_d, rq_d, rk_d, rv_d, rg_d, q_d, gates_d = _project(xs2d, w_perm, tabs_s, tm=nb, seq=seq, prompt=False)
    cache_t = cache_kv[0].transpose(0, 2, 3, 4, 1).reshape(cache_kv.shape[1], 4 * KV_WIDTH, PAGE_SIZE)
    cwin_t = cache_win[0].transpose(0, 2, 3, 4, 1).reshape(nb, 2 * KV_WIDTH, WINDOW)
    kcvc_d = _compress(page_table, jnp.zeros_like(page_table), cache_t, *cmp_w)
    row3 = lambda a: a.reshape(nb, 1, a.shape[-1])
    q3 = row3(q_d)
    oc_d, idx_d = _dec_score(q3, kcvc_d, jnp.asarray(_importance_matrix(past // CMP_STRIDE), dtype=BF16), past=past)
    idx = idx_d[:, :N_KV, :TOP_N]
    state = state_ret[0].astype(F32).reshape(nb, N_PAIR, 2 * RET_DK, RET_DV)
    gl = jnp.asarray(np.repeat(np.exp(_log_gamma()), RET_DK)[None, :], dtype=F32)
    nsa_d, ret_d, win_st, st_d = _dec_attn(page_table, idx, q3, row3(gates_d), oc_d, row3(kv4_d), row3(win_d), cache_t, cwin_t,
                                          row3(rq_d), row3(rk_d), row3(rv_d), row3(rg_d), state, gl, norm_g)
    y_s = _tail(xs2d, nsa_d.reshape(nb, NSA_WIDTH).astype(BF16), ret_d.reshape(nb, RET_WIDTH).astype(BF16), *tail_consts, tm=nb, alpha=alpha)

    y_sample = y_s.reshape(nb, 1, D_MODEL)
    kv_sample = kv4_d.reshape(1, nb, 1, 4, N_KV, HEAD_DIM)
    win_sample = to_rows(win_st, 2)
    ret_sample = st_d.reshape(1, nb, N_RET_HEADS, RET_DK, RET_DV)
    return (y_prompt, y_sample, kv_prompt, kv_sample, win_prompt, win_sample, ret_prompt, ret_sample)
```
